```python
import jax, jax.numpy as jnp
from jax import lax
import numpy as np

D_MODEL = 1024
BATCH = 16
SEQ = 4096
DEPTH = 2

GRID_W = 64
CTX_LEN = 256
MIX_WIDTH = D_MODEL
GLA_WIDTH = MIX_WIDTH // 2
GLA_HEADS = 4
GLA_DV = GLA_WIDTH // GLA_HEADS
GLA_DK = GLA_DV // 2
GLA_KDIM = GLA_HEADS * GLA_DK
GLA_GATE_RANK = 16
GLA_GATE_NORM = 16.0
GLA_CHUNK = 64
NA_WIDTH = MIX_WIDTH - GLA_WIDTH
NA_HEADS = 8
NA_DH = NA_WIDTH // NA_HEADS
NA_WIN_H = 8
NA_WIN_W = 16
ROPE_BASE = 10000.0
D_FF = 11 * D_MODEL // 4
N_EXPERTS = 8
TOP_K = 2
D_FF_EXPERT = 7 * D_MODEL // 2
N_DENSE = (DEPTH + 1) // 2
N_MOE = DEPTH // 2
EPS = 1e-6
PROJ_SIZES = (GLA_KDIM, GLA_KDIM, GLA_WIDTH, GLA_WIDTH, 2 * GLA_GATE_RANK, NA_WIDTH, NA_WIDTH, NA_WIDTH)
PROJ_TOTAL = 2 * GLA_KDIM + 2 * GLA_WIDTH + 2 * GLA_GATE_RANK + 3 * NA_WIDTH

kernel_name = "hybrid_gla_natten_moe_dit"


def rmsnorm(x, g):
    xf = x.astype(jnp.float32)
    y = xf * lax.rsqrt(jnp.mean(xf * xf, axis=-1, keepdims=True) + EPS)
    return (y * g.astype(jnp.float32)).astype(x.dtype)


def modulation(cond, w, b):
    return jnp.split(jax.nn.silu(cond) @ w + b, 6, axis=-1)


def rope_1d(x, pos):
    half = x.shape[-1] // 2
    inv = ROPE_BASE ** (-jnp.arange(half, dtype=jnp.float32) / half)
    ang = pos.astype(jnp.float32)[:, None] * inv[None, :]
    cos = jnp.cos(ang)[:, None, :].astype(x.dtype)
    sin = jnp.sin(ang)[:, None, :].astype(x.dtype)
    x1, x2 = x[..., :half], x[..., half:]
    return jnp.concatenate([x1 * cos - x2 * sin, x1 * sin + x2 * cos], axis=-1)


def rope_2d(x, row, col):
    h = x.shape[-1] // 2
    return jnp.concatenate([rope_1d(x[..., :h], row), rope_1d(x[..., h:], col)], axis=-1)


def split_proj(p):
    return jnp.split(p, np.cumsum(PROJ_SIZES)[:-1].tolist(), axis=-1)


def gla_chunk(q, k, v, g, s0):
    B, H, L, dk = q.shape
    dv = v.shape[-1]
    C = GLA_CHUNK
    n = L // C
    q = q.reshape(B, H, n, C, dk)
    k = k.reshape(B, H, n, C, dk)
    v = v.reshape(B, H, n, C, dv)
    b = jnp.cumsum(g.reshape(B, H, n, C, dk), axis=3)
    b_ref = b[:, :, :, C // 2:C // 2 + 1, :]
    a = jnp.einsum('bhncd,bhnsd->bhncs', q * jnp.exp(b - b_ref), k * jnp.exp(b_ref - b))
    lower = jnp.tril(jnp.ones((C, C), dtype=bool))
    a = jnp.where(lower, a, 0.0)
    o = jnp.einsum('bhncs,bhnsv->bhncv', a, v)
    b_last = b[:, :, :, -1:, :]
    ds = jnp.einsum('bhnsd,bhnsv->bhndv', k * jnp.exp(b_last - b), v)
    decay = jnp.exp(b_last[:, :, :, 0, :])

    def step(s, inp):
        dec, d = inp
        return dec[..., None] * s + d, s

    s_fin, s_prev = lax.scan(step, s0, (jnp.moveaxis(decay, 2, 0), jnp.moveaxis(ds, 2, 0)))
    o = o + jnp.einsum('bhncd,nbhdv->bhncv', q * jnp.exp(b), s_prev)
    return o.reshape(B, H, L, dv), s_fin


def flip_seq(t):
    return jnp.flip(t, axis=2)


def bidir_gla(q, k, v, g_fwd, g_bwd, s_fwd0, s_bwd0):
    o_f, s_f = gla_chunk(q, k, v, g_fwd, s_fwd0)
    o_b, s_b = gla_chunk(flip_seq(q), flip_seq(k), flip_seq(v), flip_seq(g_bwd), s_bwd0)
    return o_f + flip_seq(o_b), s_f, s_b


def gla_inputs(q, k, v, lr, w_gate, b_gate, row, col):
    B, L, _ = q.shape
    q = q.reshape(B, L, GLA_HEADS, GLA_DK)
    k = k.reshape(B, L, GLA_HEADS, GLA_DK)
    if row is not None:
        q = rope_2d(q, row, col)
        k = rope_2d(k, row, col)
    q = q * GLA_DK ** -0.5
    v = v.reshape(B, L, GLA_HEADS, GLA_DV)
    lr_f, lr_b = jnp.split(lr, 2, axis=-1)

    def gate(z, w, bias):
        logit = (z @ w + bias).astype(jnp.float32)
        return (jax.nn.log_sigmoid(logit) / GLA_GATE_NORM).reshape(B, L, GLA_HEADS, GLA_DK)

    g_f = gate(lr_f, w_gate[0], b_gate[0])
    g_b = gate(lr_b, w_gate[1], b_gate[1])
    to_bhl = lambda t: jnp.swapaxes(t, 1, 2).astype(jnp.float32)
    return to_bhl(q), to_bhl(k), to_bhl(v), to_bhl(g_f), to_bhl(g_b)


def gla_output(o, r, gain):
    B, H, L, dv = o.shape
    o = rmsnorm(jnp.swapaxes(o, 1, 2), gain).reshape(B, L, H * dv).astype(r.dtype)
    return o * jax.nn.silu(r)


def neighbourhood_attention(q, k, v, kc, vc, rpb):
    B, L, H, dh = q.shape
    rows = L // GRID_W
    kh = min(NA_WIN_H, rows)
    cols = np.arange(GRID_W)
    cidx = np.clip(cols - NA_WIN_W // 2, 0, GRID_W - NA_WIN_W)[:, None] + np.arange(NA_WIN_W)[None, :]
    coff = cidx - cols[:, None] + NA_WIN_W - 1
    kg = k.reshape(B, rows, GRID_W, H, dh)
    vg = v.reshape(B, rows, GRID_W, H, dh)
    qg = jnp.moveaxis(q.reshape(B, rows, GRID_W, H, dh), 1, 0) * dh ** -0.5
    n_loc = kh * NA_WIN_W

    def row_block(args):
        q_r, r = args
        rs = jnp.clip(r - kh // 2, 0, rows - kh)
        kb = lax.dynamic_slice_in_dim(kg, rs, kh, axis=1)[:, :, cidx]
        vb = lax.dynamic_slice_in_dim(vg, rs, kh, axis=1)[:, :, cidx]
        roff = rs + jnp.arange(kh) - r + NA_WIN_H - 1
        bias = jnp.transpose(rpb[:, roff[:, None, None], coff[None]], (0, 2, 1, 3))
        s_loc = jnp.einsum('bqhd,biqjhd->bhqij', q_r, kb) + bias
        s_ctx = jnp.einsum('bqhd,bchd->bhqc', q_r, kc)
        s = jnp.concatenate([s_loc.reshape(B, H, GRID_W, n_loc), s_ctx], axis=-1).astype(jnp.float32)
        p = jax.nn.softmax(s, axis=-1).astype(v.dtype)
        p_loc = p[..., :n_loc].reshape(B, H, GRID_W, kh, NA_WIN_W)
        return (jnp.einsum('bhqij,biqjhd->bqhd', p_loc, vb)
                + jnp.einsum('bhqc,bchd->bqhd', p[..., n_loc:], vc))

    out = lax.map(row_block, (qg, jnp.arange(rows)))
    return jnp.moveaxis(out, 0, 1).reshape(B, L, H * dh)


def context_attention(q, k, v):
    B, Lc, H, dh = q.shape
    s = jnp.einsum('bqhd,bkhd->bhqk', q, k).astype(jnp.float32) * dh ** -0.5
    p = jax.nn.softmax(s, axis=-1).astype(v.dtype)
    return jnp.einsum('bhqk,bkhd->bqhd', p, v).reshape(B, Lc, H * dh)


def mixer(h, hc, w_in, w_out, gla_w_gate, gla_b_gate, gla_g_norm, na_rpb, with_ctx_out):
    B, L, _ = h.shape
    pos = jnp.arange(L)
    row, col = pos // GRID_W, pos % GRID_W
    gq, gk, gv, gr, glr, nq, nk, nv = split_proj(h @ w_in)
    cq, ck, cv, cr, clr, cnq, cnk, cnv = split_proj(hc @ w_in)
    s0 = jnp.zeros((B, GLA_HEADS, GLA_DK, GLA_DV), jnp.float32)
    oc, s_fwd, s_bwd = bidir_gla(*gla_inputs(cq, ck, cv, clr, gla_w_gate, gla_b_gate, None, None), s0, s0)
    ol, _, _ = bidir_gla(*gla_inputs(gq, gk, gv, glr, gla_w_gate, gla_b_gate, row, col), s_fwd, s_bwd)
    heads = lambda t: t.reshape(t.shape[0], t.shape[1], NA_HEADS, NA_DH)
    na_lat = neighbourhood_attention(heads(nq), heads(nk), heads(nv), heads(cnk), heads(cnv), na_rpb)
    y = jnp.concatenate([gla_output(ol, gr, gla_g_norm), na_lat], axis=-1) @ w_out
    if not with_ctx_out:
        return y, None
    na_ctx = context_attention(heads(cnq), heads(cnk), heads(cnv))
    yc = jnp.concatenate([gla_output(oc, cr, gla_g_norm), na_ctx], axis=-1) @ w_out
    return y, yc


def swiglu(h, wg, wu, wd):
    return (jax.nn.silu(h @ wg) * (h @ wu)) @ wd


def moe_swiglu(h, w_router, wg, wu, wd):
    def per_sample(t):
        logits = (t @ w_router).astype(jnp.float32)
        top_v, top_i = lax.top_k(logits, TOP_K)
        wts = jax.nn.softmax(top_v, axis=-1)
        combine = jnp.einsum('tk,tke->te', wts, jax.nn.one_hot(top_i, N_EXPERTS, dtype=jnp.float32)).astype(t.dtype)
        hid = jax.nn.silu(jnp.einsum('td,edf->tef', t, wg)) * jnp.einsum('td,edf->tef', t, wu)
        return jnp.einsum('tef,efd->td', hid * combine[:, :, None], wd)
    return lax.map(per_sample, h)


def setup_inputs(seed: int = 0) -> dict:
    key = jax.random.key(seed)
    ks = jax.random.split(key, 24)
    nrm = lambda k, shape, scale: jax.random.normal(k, shape, jnp.float32) * scale
    D = D_MODEL
    return {
        "x": nrm(ks[0], (BATCH, SEQ, D), 1.0),
        "c": nrm(ks[1], (BATCH, D), 1.0),
        "ctx": nrm(ks[2], (BATCH, CTX_LEN, D), 1.0),
        "c_ctx": nrm(ks[3], (D,), 1.0),
        "w_ada": nrm(ks[4], (DEPTH, D, 6 * D), 0.5 * D ** -0.5),
        "b_ada": nrm(ks[5], (DEPTH, 6 * D), 0.02),
        "g_pre_mix": 1.0 + nrm(ks[6], (DEPTH, D), 0.02),
        "g_post_mix": 1.0 + nrm(ks[7], (DEPTH, D), 0.02),
        "g_pre_ffn": 1.0 + nrm(ks[8], (DEPTH, D), 0.02),
        "g_post_ffn": 1.0 + nrm(ks[9], (DEPTH, D), 0.02),
        "w_in": nrm(ks[10], (DEPTH, D, PROJ_TOTAL), D ** -0.5),
        "gla_w_gate": nrm(ks[11], (DEPTH, 2, GLA_GATE_RANK, GLA_KDIM), GLA_GATE_RANK ** -0.5),
        "gla_b_gate": nrm(ks[12], (DEPTH, 2, GLA_KDIM), 0.1),
        "gla_g_norm": 1.0 + nrm(ks[13], (DEPTH, GLA_DV), 0.02),
        "na_rpb": nrm(ks[14], (DEPTH, NA_HEADS, 2 * NA_WIN_H - 1, 2 * NA_WIN_W - 1), 0.1),
        "w_out": nrm(ks[15], (DEPTH, MIX_WIDTH, D), MIX_WIDTH ** -0.5),
        "ffn_w_gate": nrm(ks[16], (N_DENSE, D, D_FF), D ** -0.5),
        "ffn_w_up": nrm(ks[17], (N_DENSE, D, D_FF), D ** -0.5),
        "ffn_w_down": nrm(ks[18], (N_DENSE, D_FF, D), D_FF ** -0.5),
        "moe_w_router": nrm(ks[19], (N_MOE, D, N_EXPERTS), D ** -0.5),
        "moe_w_gate": nrm(ks[20], (N_MOE, N_EXPERTS, D, D_FF_EXPERT), D ** -0.5),
        "moe_w_up": nrm(ks[21], (N_MOE, N_EXPERTS, D, D_FF_EXPERT), D ** -0.5),
        "moe_w_down": nrm(ks[22], (N_MOE, N_EXPERTS, D_FF_EXPERT, D), D_FF_EXPERT ** -0.5),
    }


def reference(x, c, ctx, c_ctx, w_ada, b_ada, g_pre_mix, g_post_mix, g_pre_ffn, g_post_ffn, w_in,
              gla_w_gate, gla_b_gate, gla_g_norm, na_rpb, w_out, ffn_w_gate, ffn_w_up, ffn_w_down,
              moe_w_router, moe_w_gate, moe_w_up, moe_w_down):
    def channel_mixer(i, h):
        j = i // 2
        if i % 2 == 0:
            return swiglu(h, ffn_w_gate[j], ffn_w_up[j], ffn_w_down[j])
        return moe_swiglu(h, moe_w_router[j], moe_w_gate[j], moe_w_up[j], moe_w_down[j])

    for i in range(DEPTH):
        last = i == DEPTH - 1
        sh1, sc1, gt1, sh2, sc2, gt2 = [m[:, None, :] for m in modulation(c, w_ada[i], b_ada[i])]
        csh1, csc1, cgt1, csh2, csc2, cgt2 = modulation(c_ctx, w_ada[i], b_ada[i])
        h = rmsnorm(x, g_pre_mix[i]) * (1.0 + sc1) + sh1
        hc = rmsnorm(ctx, g_pre_mix[i]) * (1.0 + csc1) + csh1
        y, yc = mixer(h, hc, w_in[i], w_out[i], gla_w_gate[i], gla_b_gate[i], gla_g_norm[i], na_rpb[i], not last)
        x = x + gt1 * rmsnorm(y, g_post_mix[i])
        h = rmsnorm(x, g_pre_ffn[i]) * (1.0 + sc2) + sh2
        x = x + gt2 * rmsnorm(channel_mixer(i, h), g_post_ffn[i])
        if not last:
            ctx = ctx + cgt1 * rmsnorm(yc, g_post_mix[i])
            hc = rmsnorm(ctx, g_pre_ffn[i]) * (1.0 + csc2) + csh2
            ctx = ctx + cgt2 * rmsnorm(channel_mixer(i, hc), g_post_ffn[i])
    return x
```

```python
import functools

import numpy as np
import jax
import jax.numpy as jnp
from jax import lax
from jax.experimental import pallas as pl
from jax.experimental.pallas import tpu as pltpu

F32 = jnp.float32
BF16 = jnp.bfloat16

GRID_W = 64
GLA_HEADS = 4
GLA_DV = 128
GLA_DK = 64
GLA_KDIM = GLA_HEADS * GLA_DK
GLA_WIDTH = GLA_HEADS * GLA_DV
GLA_GATE_RANK = 16
GLA_GATE_NORM = 16.0
NA_HEADS = 8
NA_DH = 64
NA_WIDTH = NA_HEADS * NA_DH
NA_WIN_H = 8
NA_WIN_W = 16
ROPE_BASE = 10000.0
N_EXPERTS = 8
EPS = 1e-6

V7X_LANES = 128
V7X_VMEM_BYTES = 64 * 1024 * 1024
V7X_VMEM_USABLE = V7X_VMEM_BYTES - 8 * 1024 * 1024

TOKEN_TILE = 512
GLA_CHUNK = 128
NA_ROWS_PER_STEP = 8
MASK_VALUE = -1e30
MOE_FF_CHUNK = 1792
LR_PAD = V7X_LANES


def _vmem_limit(estimate_bytes):
    return int(min(V7X_VMEM_USABLE, estimate_bytes * 5 // 4 + (4 << 20)))


def _params(semantics, vmem_estimate):
    return pltpu.CompilerParams(dimension_semantics=semantics, vmem_limit_bytes=_vmem_limit(vmem_estimate))


def _dot(a, b):
    return jnp.dot(a, b, preferred_element_type=F32)


def _dot_nt(a, b):
    return lax.dot_general(a, b, (((1,), (1,)), ((), ())), preferred_element_type=F32)


def _dot_tn(a, b):
    return lax.dot_general(a, b, (((0,), (0,)), ((), ())), preferred_element_type=F32)


def _split(x):
    hi = x.astype(BF16)
    lo = (x - hi.astype(F32)).astype(BF16)
    return hi, lo


def _dot3(a, b):
    ah, al = _split(a)
    bh, bl = _split(b)
    return _dot(ah, bh) + (_dot(al, bh) + _dot(ah, bl))


def _sigmoid(x):
    return 1.0 / (1.0 + jnp.exp(-x))


def _silu(x):
    return x * _sigmoid(x)


def _rms(x):
    return x * lax.rsqrt(jnp.mean(x * x, axis=-1, keepdims=True) + EPS)


def _mod_kernel(c_ref, w_ref, b_ref, o_ref):
    o_ref[0] = _dot3(_silu(c_ref[...]), w_ref[0]) + b_ref[0]


def _modulation(cond, w_ada, b_ada):
    depth, d, n = w_ada.shape
    rows = cond.shape[0]
    tn = 512
    out = pl.pallas_call(
        _mod_kernel,
        grid=(depth, n // tn),
        in_specs=[
            pl.BlockSpec((rows, d), lambda l, j: (0, 0)),
            pl.BlockSpec((1, d, tn), lambda l, j: (l, 0, j)),
            pl.BlockSpec((1, 1, tn), lambda l, j: (l, 0, j)),
        ],
        out_specs=pl.BlockSpec((1, rows, tn), lambda l, j: (l, 0, j)),
        out_shape=jax.ShapeDtypeStruct((depth, rows, n), F32),
        compiler_params=_params(("parallel", "parallel"), 3 * d * tn * 4 * 2),
        name="modulation",
    )(cond, w_ada, b_ada.reshape(depth, 1, n))
    return out.reshape(depth, rows, 6, d)


def _inproj_kernel(x_ref, mod_ref, g_ref, w_ref, qk_ref, gv_ref, gr_ref, nq_ref, nk_ref, nv_ref, lr_ref):
    m = mod_ref[0]
    h = (_rms(x_ref[0]) * g_ref[...] * (1.0 + m[1:2]) + m[0:1]).astype(BF16)
    col = 0
    for ref in (qk_ref, gv_ref, gr_ref, nq_ref, nk_ref, nv_ref, lr_ref):
        n = ref.shape[-1]
        ref[0] = _dot(h, w_ref[:, col:col + n]).astype(ref.dtype)
        col += n


def _inproj(x, mod, mod_row0, g, w):
    nb, rows, d = x.shape
    tm = min(TOKEN_TILE, rows)
    widths = (2 * GLA_KDIM, GLA_WIDTH, GLA_WIDTH, NA_WIDTH, NA_WIDTH, NA_WIDTH, LR_PAD)
    dtypes = (BF16,) * 6 + (F32,)
    tok = lambda n: pl.BlockSpec((1, tm, n), lambda b, i: (b, i, 0))
    est = 2 * (tm * d * 4 + d * w.shape[1] * 2 + sum(widths) * tm * 4) + tm * d * 8
    return pl.pallas_call(
        _inproj_kernel,
        grid=(nb, rows // tm),
        in_specs=[
            tok(d),
            pl.BlockSpec((1, 6, d), lambda b, i: (b + mod_row0, 0, 0)),
            pl.BlockSpec((1, d), lambda b, i: (0, 0)),
            pl.BlockSpec(w.shape, lambda b, i: (0, 0)),
        ],
        out_specs=[tok(n) for n in widths],
        out_shape=[jax.ShapeDtypeStruct((nb, rows, n), dt) for n, dt in zip(widths, dtypes)],
        compiler_params=_params(("parallel", "parallel"), est),
        name="inproj",
    )(x, mod, g.reshape(1, d), w)


def _gla_kernel(q_ref, k_ref, v_ref, r_ref, lr_ref, cq_ref, ck_ref, cv_ref, cr_ref, clr_ref,
                cos_ref, sin_ref, perm_ref, wg_ref, bg_ref, gain_ref,
                o_ref, oc_ref, of_ref, ocf_ref):
    C = GLA_CHUNK
    L = q_ref.shape[1]
    Lc = cq_ref.shape[1]
    row = lax.broadcasted_iota(jnp.int32, (C, C), 0)
    col = lax.broadcasted_iota(jnp.int32, (C, C), 1)
    lane = lax.broadcasted_iota(jnp.int32, (1, 2 * GLA_DK), 1)
    head0 = lane < GLA_DK
    st_row = lax.broadcasted_iota(jnp.int32, (2 * GLA_DV, 2 * GLA_DK), 0) // GLA_DV
    st_col = lax.broadcasted_iota(jnp.int32, (2 * GLA_DV, 2 * GLA_DK), 1) // GLA_DK
    blockdiag = st_row == st_col
    gain = gain_ref[...]
    perm = perm_ref[...]

    def chunk_step(refs, i, st, d, rope):
        rq, rk, rv, rlr = refs
        sl = pl.ds(pl.multiple_of(i * C, C), C)
        q = rq[0, sl, :]
        k = rk[0, sl, :]
        v = rv[0, sl, :]
        qf = q.astype(F32)
        kf = k.astype(F32)
        if rope:
            cos = cos_ref[sl, :]
            sin = sin_ref[sl, :]
            qf = qf * cos + _dot(q, perm) * sin
            kf = kf * cos + _dot(k, perm) * sin
        qf = qf * (GLA_DK ** -0.5)
        logit = _dot3(rlr[0, sl, :], wg_ref[d]) + bg_ref[d]
        g = (jnp.minimum(logit, 0.0) - jnp.log(1.0 + jnp.exp(-jnp.abs(logit)))) * (1.0 / GLA_GATE_NORM)
        tri = (row >= col) if d == 0 else (row <= col)
        tmat = jnp.where(tri, 1.0, 0.0).astype(BF16)
        gh, gl = _split(g)
        b = _dot(tmat, gh) + _dot(tmat, gl)
        b_mid = b[C // 2:C // 2 + 1]
        b_edge = b[C - 1:C] if d == 0 else b[0:1]
        qe = (qf * jnp.exp(b - b_mid)).astype(BF16)
        ke = (kf * jnp.exp(b_mid - b)).astype(BF16)
        zero = jnp.zeros_like(qe)
        lhs = jnp.concatenate([jnp.where(head0, qe, zero), jnp.where(head0, zero, qe)], axis=0)
        a = _dot_nt(lhs, ke)
        a = jnp.where(jnp.concatenate([tri, tri], axis=0), a, 0.0).astype(BF16)
        o = jnp.concatenate([_dot(a[:C], v[:, :GLA_DV]), _dot(a[C:], v[:, GLA_DV:])], axis=1)
        qb = (qf * jnp.exp(b)).astype(BF16)
        o = o + _dot_nt(qb, st.astype(BF16))
        kd = (kf * jnp.exp(b_edge - b)).astype(BF16)
        ds = _dot_tn(v, kd)
        st = st * jnp.exp(b_edge) + jnp.where(blockdiag, ds, 0.0)
        return o, st

    def finish(total, r):
        outs = []
        for h in range(2):
            oh = _rms(total[:, h * GLA_DV:(h + 1) * GLA_DV]) * gain
            outs.append(oh * _silu(r[:, h * GLA_DV:(h + 1) * GLA_DV].astype(F32)))
        return jnp.concatenate(outs, axis=1)

    def scan(refs, n, st, d, rope, acc_ref, r_ref_, out_ref):
        def body(j, st):
            i = j if d == 0 else n - 1 - j
            o, st = chunk_step(refs, i, st, d, rope)
            sl = pl.ds(pl.multiple_of(i * C, C), C)
            if d == 0:
                acc_ref[sl, :] = o
            else:
                out_ref[0, sl, :] = finish(acc_ref[sl, :] + o, r_ref_[0, sl, :]).astype(out_ref.dtype)
            return st
        return lax.fori_loop(0, n, body, st)

    st0 = jnp.zeros((2 * GLA_DV, 2 * GLA_DK), F32)
    ctx_refs = (cq_ref, ck_ref, cv_ref, clr_ref)
    lat_refs = (q_ref, k_ref, v_ref, lr_ref)
    s_fwd = scan(ctx_refs, Lc // C, st0, 0, False, ocf_ref, cr_ref, oc_ref)
    s_bwd = scan(ctx_refs, Lc // C, st0, 1, False, ocf_ref, cr_ref, oc_ref)
    scan(lat_refs, L // C, s_fwd, 0, True, of_ref, r_ref, o_ref)
    scan(lat_refs, L // C, s_bwd, 1, True, of_ref, r_ref, o_ref)


def _gla(qk, gv, gr, lr, cqk, cgv, cgr, clr, tables, wgp, bgp, gain):
    B, L, _ = qk.shape
    Lc = cqk.shape[1]
    cos, sin, perm = tables
    pair = 2 * GLA_DK
    pv = 2 * GLA_DV
    n_pair = GLA_HEADS // 2
    seq = lambda rows, n, off: pl.BlockSpec((1, rows, n), lambda b, p: (b, 0, p + off))
    const = lambda shape: pl.BlockSpec(shape, lambda b, p: (0,) * len(shape))
    est = (2 * (2 * L * pair * 2 + 2 * L * pv * 2 + L * LR_PAD * 4 + L * pv * 2 + 2 * L * pair * 4)
           + L * pv * 4 + (8 << 20))
    return pl.pallas_call(
        _gla_kernel,
        grid=(B, n_pair),
        in_specs=[
            seq(L, pair, 0), seq(L, pair, n_pair), seq(L, pv, 0), seq(L, pv, 0),
            pl.BlockSpec((1, L, LR_PAD), lambda b, p: (b, 0, 0)),
            seq(Lc, pair, 0), seq(Lc, pair, n_pair), seq(Lc, pv, 0), seq(Lc, pv, 0),
            pl.BlockSpec((1, Lc, LR_PAD), lambda b, p: (b, 0, 0)),
            const((L, pair)), const((L, pair)), const((pair, pair)),
            pl.BlockSpec((2, LR_PAD, pair), lambda b, p: (0, 0, p)),
            pl.BlockSpec((2, 1, pair), lambda b, p: (0, 0, p)),
            const((1, GLA_DV)),
        ],
        out_specs=[seq(L, pv, 0), seq(Lc, pv, 0)],
        out_shape=[jax.ShapeDtypeStruct((B, L, GLA_WIDTH), BF16), jax.ShapeDtypeStruct((B, Lc, GLA_WIDTH), BF16)],
        scratch_shapes=[pltpu.VMEM((L, pv), F32), pltpu.VMEM((Lc, pv), F32)],
        compiler_params=_params(("parallel", "parallel"), est),
        name="gla",
    )(qk, qk, gv, gr, lr, cqk, cqk, cgv, cgr, clr, cos, sin, perm, wgp, bgp, gain.reshape(1, GLA_DV))


def _rope_tables(L):
    pos = np.arange(L)
    half = GLA_DK // 4
    inv = ROPE_BASE ** (-np.arange(half, dtype=np.float64) / half)
    lane = np.arange(2 * GLA_DK)
    jj = lane % (GLA_DK // 2)
    use_col = (lane % GLA_DK) >= GLA_DK // 2
    p = np.where(use_col[None, :], (pos % GRID_W)[:, None], (pos // GRID_W)[:, None]).astype(np.float64)
    ang = p * inv[jj % half][None, :]
    first = jj < half
    cos = np.cos(ang)
    sin = np.where(first[None, :], -np.sin(ang), np.sin(ang))
    partner = np.where(first, lane + half, lane - half)
    perm = np.zeros((2 * GLA_DK, 2 * GLA_DK), np.float32)
    perm[partner, lane] = 1.0
    return jnp.asarray(cos, F32), jnp.asarray(sin, F32), jnp.asarray(perm, BF16)


def _softmax_pv(s_parts, v_parts):
    m = s_parts[0].max(axis=-1, keepdims=True)
    for s in s_parts[1:]:
        m = jnp.maximum(m, s.max(axis=-1, keepdims=True))
    den = None
    acc = None
    for s, v in zip(s_parts, v_parts):
        p = jnp.exp(s - m)
        den = p.sum(axis=-1, keepdims=True) if den is None else den + p.sum(axis=-1, keepdims=True)
        pv = _dot(p.astype(BF16), v)
        acc = pv if acc is None else acc + pv
    return acc / den


def _na_kernel(q_ref, k_ref, v_ref, ck_ref, cv_ref, bias_ref, o_ref):
    W = GRID_W
    rows = k_ref.shape[1] // W
    kh = bias_ref.shape[3] // W
    r0 = pl.program_id(1) * NA_ROWS_PER_STEP
    pair = 2 * NA_DH
    lane = lax.broadcasted_iota(jnp.int32, (1, pair), 1)
    head0 = lane < NA_DH
    scale = NA_DH ** -0.5

    def row_body(rr, carry):
        r = r0 + rr
        rs = jnp.clip(r - kh // 2, 0, rows - kh)
        dr = rs - r + NA_WIN_H - 1
        qs = pl.ds(pl.multiple_of(rr * W, W), W)
        ks = pl.ds(pl.multiple_of(rs * W, W), kh * W)
        outs = []
        for p in range(NA_HEADS // 2):
            ls = slice(p * pair, (p + 1) * pair)
            q = q_ref[0, qs, ls]
            zero = jnp.zeros_like(q)
            q2 = jnp.concatenate([jnp.where(head0, q, zero), jnp.where(head0, zero, q)], axis=0)
            v_loc = v_ref[0, ks, ls]
            v_ctx = cv_ref[0, :, ls]
            bias = jnp.concatenate([bias_ref[dr, 2 * p], bias_ref[dr, 2 * p + 1]], axis=0)
            s_loc = _dot_nt(q2, k_ref[0, ks, ls]) * scale + bias
            s_ctx = _dot_nt(q2, ck_ref[0, :, ls]) * scale
            o2 = _softmax_pv([s_loc, s_ctx], [v_loc, v_ctx])
            outs.append(jnp.where(head0, o2[:W], o2[W:]))
        o_ref[0, qs, :] = jnp.concatenate(outs, axis=1).astype(o_ref.dtype)
        return carry

    lax.fori_loop(0, NA_ROWS_PER_STEP, row_body, 0)


def _na(nq, nk, nv, cnk, cnv, bias):
    B, L, n = nq.shape
    Lc = cnk.shape[1]
    tq = NA_ROWS_PER_STEP * GRID_W
    full = lambda rows: pl.BlockSpec((1, rows, n), lambda b, i: (b, 0, 0))
    est = 2 * (2 * L * n * 2 + 2 * Lc * n * 2 + bias.size * 4 + 2 * tq * n * 2) + (8 << 20)
    return pl.pallas_call(
        _na_kernel,
        grid=(B, L // tq),
        in_specs=[
            pl.BlockSpec((1, tq, n), lambda b, i: (b, i, 0)),
            full(L), full(L), full(Lc), full(Lc),
            pl.BlockSpec(bias.shape, lambda b, i: (0, 0, 0, 0)),
        ],
        out_specs=pl.BlockSpec((1, tq, n), lambda b, i: (b, i, 0)),
        out_shape=jax.ShapeDtypeStruct((B, L, n), BF16),
        compiler_params=_params(("parallel", "parallel"), est),
        name="neighbourhood_attention",
    )(nq, nk, nv, cnk, cnv, bias)


def _na_bias_table(rpb, rows):
    kh = min(NA_WIN_H, rows)
    cols = np.arange(GRID_W)
    start = np.clip(cols - NA_WIN_W // 2, 0, GRID_W - NA_WIN_W)
    kc = np.arange(GRID_W)
    inside = (kc[None, :] >= start[:, None]) & (kc[None, :] < start[:, None] + NA_WIN_W)
    coff = np.clip(kc[None, :] - cols[:, None] + NA_WIN_W - 1, 0, 2 * NA_WIN_W - 2)
    n_dr = NA_WIN_H
    tabs = []
    for dri in range(n_dr):
        roff = dri + np.arange(kh)
        t = rpb[:, roff][:, :, coff]
        t = jnp.where(inside[None, None], t, MASK_VALUE)
        tabs.append(jnp.transpose(t, (0, 2, 1, 3)).reshape(NA_HEADS, GRID_W, kh * GRID_W))
    return jnp.stack(tabs, axis=0).astype(F32)


def _ctx_attn_kernel(q_ref, k_ref, v_ref, o_ref):
    pair = 2 * NA_DH
    lane = lax.broadcasted_iota(jnp.int32, (1, pair), 1)
    head0 = lane < NA_DH
    Lc = q_ref.shape[1]
    outs = []
    for p in range(NA_HEADS // 2):
        ls = slice(p * pair, (p + 1) * pair)
        q = q_ref[0, :, ls]
        zero = jnp.zeros_like(q)
        q2 = jnp.concatenate([jnp.where(head0, q, zero), jnp.where(head0, zero, q)], axis=0)
        s = _dot_nt(q2, k_ref[0, :, ls]) * (NA_DH ** -0.5)
        o2 = _softmax_pv([s], [v_ref[0, :, ls]])
        outs.append(jnp.where(head0, o2[:Lc], o2[Lc:]))
    o_ref[0] = jnp.concatenate(outs, axis=1).astype(o_ref.dtype)


def _ctx_attn(cnq, cnk, cnv):
    B, Lc, n = cnq.shape
    spec = pl.BlockSpec((1, Lc, n), lambda b: (b, 0, 0))
    return pl.pallas_call(
        _ctx_attn_kernel,
        grid=(B,),
        in_specs=[spec, spec, spec],
        out_specs=spec,
        out_shape=jax.ShapeDtypeStruct((B, Lc, n), BF16),
        compiler_params=_params(("parallel",), 8 * Lc * n * 2 + (8 << 20)),
        name="context_attention",
    )(cnq, cnk, cnv)


def _route(logits):
    lane = lax.broadcasted_iota(jnp.int32, logits.shape, 1)
    big = jnp.int32(logits.shape[1])
    t1 = logits.max(axis=-1, keepdims=True)
    i1 = jnp.where(logits == t1, lane, big).min(axis=-1, keepdims=True)
    rest = jnp.where(lane == i1, -jnp.inf, logits)
    t2 = rest.max(axis=-1, keepdims=True)
    i2 = jnp.where(rest == t2, lane, big).min(axis=-1, keepdims=True)
    e2 = jnp.exp(t2 - t1)
    w1 = 1.0 / (1.0 + e2)
    w2 = e2 / (1.0 + e2)
    return jnp.where(lane == i1, w1, 0.0) + jnp.where(lane == i2, w2, 0.0)


def _outproj_kernel(*refs, with_router):
    if with_router:
        ga_ref, na_ref, x_ref, mod_ref, wa_ref, wb_ref, gp_ref, gf_ref, wr_ref, x1_ref, h2_ref, cmb_ref = refs
    else:
        ga_ref, na_ref, x_ref, mod_ref, wa_ref, wb_ref, gp_ref, gf_ref, x1_ref, h2_ref = refs
    m = mod_ref[0]
    y = _dot(ga_ref[0], wa_ref[...]) + _dot(na_ref[0], wb_ref[...])
    x1 = x_ref[0] + m[2:3] * (_rms(y) * gp_ref[...])
    x1_ref[0] = x1
    h2 = _rms(x1) * gf_ref[...] * (1.0 + m[4:5]) + m[3:4]
    h2_ref[0] = h2.astype(h2_ref.dtype)
    if with_router:
        lane = lax.broadcasted_iota(jnp.int32, (1, wr_ref.shape[1]), 1)
        logits = jnp.where(lane < N_EXPERTS, _dot3(h2, wr_ref[...]), -jnp.inf)
        cmb_ref[0] = _route(logits)


def _outproj(ga, na, x, mod, mod_row0, w_out, g_post, g_ffn, w_router=None):
    nb, rows, d = x.shape
    tm = min(TOKEN_TILE, rows)
    with_router = w_router is not None
    tok = lambda n: pl.BlockSpec((1, tm, n), lambda b, i: (b, i, 0))
    const = lambda shape: pl.BlockSpec(shape, lambda b, i: (0,) * len(shape))
    wa = w_out[:GLA_WIDTH].astype(BF16)
    wb = w_out[GLA_WIDTH:].astype(BF16)
    args = [ga, na, x, mod, wa, wb, g_post.reshape(1, d), g_ffn.reshape(1, d)]
    in_specs = [tok(GLA_WIDTH), tok(NA_WIDTH), tok(d),
                pl.BlockSpec((1, 6, d), lambda b, i: (b + mod_row0, 0, 0)),
                const(wa.shape), const(wb.shape), const((1, d)), const((1, d))]
    out_specs = [tok(d), tok(d)]
    out_shape = [jax.ShapeDtypeStruct((nb, rows, d), F32), jax.ShapeDtypeStruct((nb, rows, d), BF16)]
    if with_router:
        wr = jnp.zeros((d, V7X_LANES), F32).at[:, :N_EXPERTS].set(w_router)
        args.append(wr)
        in_specs.append(const(wr.shape))
        out_specs.append(tok(V7X_LANES))
        out_shape.append(jax.ShapeDtypeStruct((nb, rows, V7X_LANES), F32))
    est = 2 * (tm * d * (4 + 4 + 2) + 2 * tm * GLA_WIDTH * 2 + d * d * 2) + 4 * tm * d * 4
    return pl.pallas_call(
        functools.partial(_outproj_kernel, with_router=with_router),
        grid=(nb, rows // tm),
        in_specs=in_specs,
        out_specs=out_specs,
        out_shape=out_shape,
        compiler_params=_params(("parallel", "parallel"), est),
        name="outproj",
    )(*args)


def _ffn_kernel(h_ref, x_ref, mod_ref, wg_ref, wu_ref, wd_ref, g_ref, o_ref):
    h = h_ref[0]
    hid = (_silu(_dot(h, wg_ref[...])) * _dot(h, wu_ref[...])).astype(BF16)
    y = _dot(hid, wd_ref[...])
    o_ref[0] = x_ref[0] + mod_ref[0][5:6] * (_rms(y) * g_ref[...])


def _ffn(h2, x1, mod, mod_row0, wg, wu, wd, g_post):
    nb, rows, d = x1.shape
    dff = wg.shape[1]
    tm = min(TOKEN_TILE, rows)
    tok = lambda n: pl.BlockSpec((1, tm, n), lambda b, i: (b, i, 0))
    const = lambda shape: pl.BlockSpec(shape, lambda b, i: (0,) * len(shape), pipeline_mode=pl.Buffered(1))
    est = 3 * d * dff * 2 + 2 * tm * d * (2 + 4 + 4) + 3 * tm * dff * 4 + tm * d * 4
    return pl.pallas_call(
        _ffn_kernel,
        grid=(nb, rows // tm),
        in_specs=[tok(d), tok(d), pl.BlockSpec((1, 6, d), lambda b, i: (b + mod_row0, 0, 0)),
                  const((d, dff)), const((d, dff)), const((dff, d)),
                  pl.BlockSpec((1, d), lambda b, i: (0, 0))],
        out_specs=tok(d),
        out_shape=jax.ShapeDtypeStruct((nb, rows, d), F32),
        compiler_params=_params(("parallel", "parallel"), est),
        name="swiglu_ffn",
    )(h2, x1, mod, wg.astype(BF16), wu.astype(BF16), wd.astype(BF16), g_post.reshape(1, d))


def _moe_kernel(h_ref, cmb_ref, x_ref, mod_ref, wg_ref, wu_ref, wd_ref, g_ref, o_ref, acc_ref):
    e = pl.program_id(2)
    c = pl.program_id(3)
    first = jnp.logical_and(e == 0, c == 0)
    last = jnp.logical_and(e == pl.num_programs(2) - 1, c == pl.num_programs(3) - 1)
    h = h_ref[0]
    cmb = cmb_ref[0]
    lane = lax.broadcasted_iota(jnp.int32, cmb.shape, 1)
    gate = jnp.where(lane == e, cmb, 0.0).sum(axis=-1, keepdims=True)
    hid = _silu(_dot(h, wg_ref[0])) * _dot(h, wu_ref[0]) * gate
    part = _dot(hid.astype(BF16), wd_ref[0])

    @pl.when(first)
    def _():
        acc_ref[...] = part

    @pl.when(jnp.logical_not(first))
    def _():
        acc_ref[...] += part

    @pl.when(last)
    def _():
        o_ref[0] = x_ref[0] + mod_ref[0][5:6] * (_rms(acc_ref[...]) * g_ref[...])


def _moe_dense(h2, cmb, x1, mod, mod_row0, wg, wu, wd, g_post):
    nb, rows, d = x1.shape
    n_e, _, dff = wg.shape
    tm = min(TOKEN_TILE, rows)
    fc = MOE_FF_CHUNK
    tok = lambda n: pl.BlockSpec((1, tm, n), lambda b, i, e, c: (b, i, 0))
    est = 2 * (3 * d * fc * 2 + tm * d * (2 + 4 + 4) + tm * V7X_LANES * 4) + tm * d * 4 + 3 * tm * fc * 4
    return pl.pallas_call(
        _moe_kernel,
        grid=(nb, rows // tm, n_e, dff // fc),
        in_specs=[tok(d), tok(V7X_LANES), tok(d),
                  pl.BlockSpec((1, 6, d), lambda b, i, e, c: (b + mod_row0, 0, 0)),
                  pl.BlockSpec((1, d, fc), lambda b, i, e, c: (e, 0, c)),
                  pl.BlockSpec((1, d, fc), lambda b, i, e, c: (e, 0, c)),
                  pl.BlockSpec((1, fc, d), lambda b, i, e, c: (e, c, 0)),
                  pl.BlockSpec((1, d), lambda b, i, e, c: (0, 0))],
        out_specs=tok(d),
        out_shape=jax.ShapeDtypeStruct((nb, rows, d), F32),
        scratch_shapes=[pltpu.VMEM((tm, d), F32)],
        compiler_params=_params(("parallel", "parallel", "arbitrary", "arbitrary"), est),
        name="moe_experts",
    )(h2, cmb, x1, mod, wg.astype(BF16), wu.astype(BF16), wd.astype(BF16), g_post.reshape(1, d))


def _pack_w_in(w):
    a = 2 * GLA_KDIM + 2 * GLA_WIDTH
    lr = 2 * GLA_GATE_RANK
    pad = jnp.zeros((w.shape[0], LR_PAD - lr), w.dtype)
    return jnp.concatenate([w[:, :a], w[:, a + lr:], w[:, a:a + lr], pad], axis=1).astype(BF16)


def _pack_gate(w_gate, b_gate):
    wgp = jnp.zeros((2, LR_PAD, GLA_KDIM), F32)
    for d in range(2):
        wgp = wgp.at[d, d * GLA_GATE_RANK:(d + 1) * GLA_GATE_RANK].set(w_gate[d])
    return wgp, b_gate.reshape(2, 1, GLA_KDIM)


def kernel(x, c, ctx, c_ctx, w_ada, b_ada, g_pre_mix, g_post_mix, g_pre_ffn, g_post_ffn, w_in,
           gla_w_gate, gla_b_gate, gla_g_norm, na_rpb, w_out, ffn_w_gate, ffn_w_up, ffn_w_down,
           moe_w_router, moe_w_gate, moe_w_up, moe_w_down):
    B, L, D = x.shape
    Lc = ctx.shape[1]
    depth = w_ada.shape[0]
    rows = L // GRID_W
    ctx_row = B
    n_cond = -(-(B + 1) // 8) * 8
    cond = jnp.zeros((n_cond, D), F32).at[:B].set(c).at[B].set(c_ctx)
    mod = _modulation(cond, w_ada, b_ada)
    tables = _rope_tables(L)
    ctx = ctx.reshape(1, B * Lc, D)

    for i in range(depth):
        last = i == depth - 1
        j = i // 2
        w = _pack_w_in(w_in[i])
        wgp, bgp = _pack_gate(gla_w_gate[i], gla_b_gate[i])
        qk, gv, gr, nq, nk, nv, lr = _inproj(x, mod[i], 0, g_pre_mix[i], w)
        cparts = _inproj(ctx, mod[i], ctx_row, g_pre_mix[i], w)
        cqk, cgv, cgr, cnq, cnk, cnv, clr = [t.reshape(B, Lc, t.shape[-1]) for t in cparts]
        ga, gac = _gla(qk, gv, gr, lr, cqk, cgv, cgr, clr, tables, wgp, bgp, gla_g_norm[i])
        na = _na(nq, nk, nv, cnk, cnv, _na_bias_table(na_rpb[i], rows))
        if i % 2 == 0:
            x1, h2 = _outproj(ga, na, x, mod[i], 0, w_out[i], g_post_mix[i], g_pre_ffn[i])
            x = _ffn(h2, x1, mod[i], 0, ffn_w_gate[j], ffn_w_up[j], ffn_w_down[j], g_post_ffn[i])
        else:
            x1, h2, cmb = _outproj(ga, na, x, mod[i], 0, w_out[i], g_post_mix[i], g_pre_ffn[i], moe_w_router[j])
            x = _moe_dense(h2, cmb, x1, mod[i], 0, moe_w_gate[j], moe_w_up[j], moe_w_down[j], g_post_ffn[i])
        if not last:
            nac = _ctx_attn(cnq, cnk, cnv)
            gac = gac.reshape(1, B * Lc, GLA_WIDTH)
            nac = nac.reshape(1, B * Lc, NA_WIDTH)
            if i % 2 == 0:
                c1, ch2 = _outproj(gac, nac, ctx, mod[i], ctx_row, w_out[i], g_post_mix[i], g_pre_ffn[i])
                ctx = _ffn(ch2, c1, mod[i], ctx_row, ffn_w_gate[j], ffn_w_up[j], ffn_w_down[j], g_post_ffn[i])
            else:
                c1, ch2, ccmb = _outproj(gac, nac, ctx, mod[i], ctx_row, w_out[i], g_post_mix[i], g_pre_ffn[i],
                                         moe_w_router[j])
                ctx = _moe_dense(ch2, ccmb, c1, mod[i], ctx_row, moe_w_gate[j], moe_w_up[j], moe_w_down[j], g_post_ffn[i])
    return x
```

```python
import functools

import numpy as np
import jax
import jax.numpy as jnp
from jax import lax
from jax.experimental import pallas as pl
from jax.experimental.pallas import tpu as pltpu
from jax.experimental.pallas import tpu_sc as plsc

F32 = jnp.float32
BF16 = jnp.bfloat16

GRID_W = 64
GLA_HEADS = 4
GLA_DV = 128
GLA_DK = 64
GLA_KDIM = GLA_HEADS * GLA_DK
GLA_WIDTH = GLA_HEADS * GLA_DV
GLA_GATE_RANK = 16
GLA_GATE_NORM = 16.0
NA_HEADS = 8
NA_DH = 64
NA_WIDTH = NA_HEADS * NA_DH
NA_WIN_H = 8
NA_WIN_W = 16
ROPE_BASE = 10000.0
N_EXPERTS = 8
EPS = 1e-6

V7X_LANES = 128
V7X_VMEM_BYTES = 64 * 1024 * 1024
V7X_VMEM_USABLE = V7X_VMEM_BYTES - 8 * 1024 * 1024

TOKEN_TILE = 512
GLA_CHUNK = 128
NA_ROWS_PER_STEP = 8
MASK_VALUE = -1e30
MOE_FF_CHUNK = 1792
MOE_ROW_TILE = 512
SC_ROW_WINDOW = 128
SC_ROW_WORDS = 256
LR_PAD = V7X_LANES


def _vmem_limit(estimate_bytes):
    return int(min(V7X_VMEM_USABLE, estimate_bytes * 5 // 4 + (4 << 20)))


def _params(semantics, vmem_estimate):
    return pltpu.CompilerParams(dimension_semantics=semantics, vmem_limit_bytes=_vmem_limit(vmem_estimate))


def _dot(a, b):
    return jnp.dot(a, b, preferred_element_type=F32)


def _dot_nt(a, b):
    return lax.dot_general(a, b, (((1,), (1,)), ((), ())), preferred_element_type=F32)


def _dot_tn(a, b):
    return lax.dot_general(a, b, (((0,), (0,)), ((), ())), preferred_element_type=F32)


def _split(x):
    hi = x.astype(BF16)
    lo = (x - hi.astype(F32)).astype(BF16)
    return hi, lo


def _dot3(a, b):
    ah, al = _split(a)
    bh, bl = _split(b)
    return _dot(ah, bh) + (_dot(al, bh) + _dot(ah, bl))


def _sigmoid(x):
    return 1.0 / (1.0 + jnp.exp(-x))


def _silu(x):
    return x * _sigmoid(x)


def _rms(x):
    return x * lax.rsqrt(jnp.mean(x * x, axis=-1, keepdims=True) + EPS)


def _mod_kernel(c_ref, w_ref, b_ref, o_ref):
    o_ref[0] = _dot3(_silu(c_ref[...]), w_ref[0]) + b_ref[0]


def _modulation(cond, w_ada, b_ada):
    depth, d, n = w_ada.shape
    rows = cond.shape[0]
    tn = 512
    out = pl.pallas_call(
        _mod_kernel,
        grid=(depth, n // tn),
        in_specs=[
            pl.BlockSpec((rows, d), lambda l, j: (0, 0)),
            pl.BlockSpec((1, d, tn), lambda l, j: (l, 0, j)),
            pl.BlockSpec((1, 1, tn), lambda l, j: (l, 0, j)),
        ],
        out_specs=pl.BlockSpec((1, rows, tn), lambda l, j: (l, 0, j)),
        out_shape=jax.ShapeDtypeStruct((depth, rows, n), F32),
        compiler_params=_params(("parallel", "parallel"), 3 * d * tn * 4 * 2),
        name="modulation",
    )(cond, w_ada, b_ada.reshape(depth, 1, n))
    return out.reshape(depth, rows, 6, d)


def _inproj_kernel(x_ref, mod_ref, g_ref, w_ref, qk_ref, gv_ref, gr_ref, nq_ref, nk_ref, nv_ref, lr_ref):
    m = mod_ref[0]
    h = (_rms(x_ref[0]) * g_ref[...] * (1.0 + m[1:2]) + m[0:1]).astype(BF16)
    col = 0
    for ref in (qk_ref, gv_ref, gr_ref, nq_ref, nk_ref, nv_ref, lr_ref):
        n = ref.shape[-1]
        ref[0] = _dot(h, w_ref[:, col:col + n]).astype(ref.dtype)
        col += n


def _inproj(x, mod, mod_row0, g, w):
    nb, rows, d = x.shape
    tm = min(TOKEN_TILE, rows)
    widths = (2 * GLA_KDIM, GLA_WIDTH, GLA_WIDTH, NA_WIDTH, NA_WIDTH, NA_WIDTH, LR_PAD)
    dtypes = (BF16,) * 6 + (F32,)
    tok = lambda n: pl.BlockSpec((1, tm, n), lambda b, i: (b, i, 0))
    est = 2 * (tm * d * 4 + d * w.shape[1] * 2 + sum(widths) * tm * 4) + tm * d * 8
    return pl.pallas_call(
        _inproj_kernel,
        grid=(nb, rows // tm),
        in_specs=[
            tok(d),
            pl.BlockSpec((1, 6, d), lambda b, i: (b + mod_row0, 0, 0)),
            pl.BlockSpec((1, d), lambda b, i: (0, 0)),
            pl.BlockSpec(w.shape, lambda b, i: (0, 0)),
        ],
        out_specs=[tok(n) for n in widths],
        out_shape=[jax.ShapeDtypeStruct((nb, rows, n), dt) for n, dt in zip(widths, dtypes)],
        compiler_params=_params(("parallel", "parallel"), est),
        name="inproj",
    )(x, mod, g.reshape(1, d), w)


def _gla_kernel(q_ref, k_ref, v_ref, r_ref, lr_ref, cq_ref, ck_ref, cv_ref, cr_ref, clr_ref,
                cos_ref, sin_ref, perm_ref, wg_ref, bg_ref, gain_ref,
                o_ref, oc_ref, of_ref, ocf_ref):
    C = GLA_CHUNK
    L = q_ref.shape[1]
    Lc = cq_ref.shape[1]
    row = lax.broadcasted_iota(jnp.int32, (C, C), 0)
    col = lax.broadcasted_iota(jnp.int32, (C, C), 1)
    lane = lax.broadcasted_iota(jnp.int32, (1, 2 * GLA_DK), 1)
    head0 = lane < GLA_DK
    st_row = lax.broadcasted_iota(jnp.int32, (2 * GLA_DV, 2 * GLA_DK), 0) // GLA_DV
    st_col = lax.broadcasted_iota(jnp.int32, (2 * GLA_DV, 2 * GLA_DK), 1) // GLA_DK
    blockdiag = st_row == st_col
    gain = gain_ref[...]
    perm = perm_ref[...]

    def chunk_step(refs, i, st, d, rope):
        rq, rk, rv, rlr = refs
        sl = pl.ds(pl.multiple_of(i * C, C), C)
        q = rq[0, sl, :]
        k = rk[0, sl, :]
        v = rv[0, sl, :]
        qf = q.astype(F32)
        kf = k.astype(F32)
        if rope:
            cos = cos_ref[sl, :]
            sin = sin_ref[sl, :]
            qf = qf * cos + _dot(q, perm) * sin
            kf = kf * cos + _dot(k, perm) * sin
        qf = qf * (GLA_DK ** -0.5)
        logit = _dot3(rlr[0, sl, :], wg_ref[d]) + bg_ref[d]
        g = (jnp.minimum(logit, 0.0) - jnp.log(1.0 + jnp.exp(-jnp.abs(logit)))) * (1.0 / GLA_GATE_NORM)
        tri = (row >= col) if d == 0 else (row <= col)
        tmat = jnp.where(tri, 1.0, 0.0).astype(BF16)
        gh, gl = _split(g)
        b = _dot(tmat, gh) + _dot(tmat, gl)
        b_mid = b[C // 2:C // 2 + 1]
        b_edge = b[C - 1:C] if d == 0 else b[0:1]
        qe = (qf * jnp.exp(b - b_mid)).astype(BF16)
        ke = (kf * jnp.exp(b_mid - b)).astype(BF16)
        zero = jnp.zeros_like(qe)
        lhs = jnp.concatenate([jnp.where(head0, qe, zero), jnp.where(head0, zero, qe)], axis=0)
        a = _dot_nt(lhs, ke)
        a = jnp.where(jnp.concatenate([tri, tri], axis=0), a, 0.0).astype(BF16)
        o = jnp.concatenate([_dot(a[:C], v[:, :GLA_DV]), _dot(a[C:], v[:, GLA_DV:])], axis=1)
        qb = (qf * jnp.exp(b)).astype(BF16)
        o = o + _dot_nt(qb, st.astype(BF16))
        kd = (kf * jnp.exp(b_edge - b)).astype(BF16)
        ds = _dot_tn(v, kd)
        st = st * jnp.exp(b_edge) + jnp.where(blockdiag, ds, 0.0)
        return o, st

    def finish(total, r):
        outs = []
        for h in range(2):
            oh = _rms(total[:, h * GLA_DV:(h + 1) * GLA_DV]) * gain
            outs.append(oh * _silu(r[:, h * GLA_DV:(h + 1) * GLA_DV].astype(F32)))
        return jnp.concatenate(outs, axis=1)

    def scan(refs, n, st, d, rope, acc_ref, r_ref_, out_ref):
        def body(j, st):
            i = j if d == 0 else n - 1 - j
            o, st = chunk_step(refs, i, st, d, rope)
            sl = pl.ds(pl.multiple_of(i * C, C), C)
            if d == 0:
                acc_ref[sl, :] = o
            else:
                out_ref[0, sl, :] = finish(acc_ref[sl, :] + o, r_ref_[0, sl, :]).astype(out_ref.dtype)
            return st
        return lax.fori_loop(0, n, body, st)

    st0 = jnp.zeros((2 * GLA_DV, 2 * GLA_DK), F32)
    ctx_refs = (cq_ref, ck_ref, cv_ref, clr_ref)
    lat_refs = (q_ref, k_ref, v_ref, lr_ref)
    s_fwd = scan(ctx_refs, Lc // C, st0, 0, False, ocf_ref, cr_ref, oc_ref)
    s_bwd = scan(ctx_refs, Lc // C, st0, 1, False, ocf_ref, cr_ref, oc_ref)
    scan(lat_refs, L // C, s_fwd, 0, True, of_ref, r_ref, o_ref)
    scan(lat_refs, L // C, s_bwd, 1, True, of_ref, r_ref, o_ref)


def _gla(qk, gv, gr, lr, cqk, cgv, cgr, clr, tables, wgp, bgp, gain):
    B, L, _ = qk.shape
    Lc = cqk.shape[1]
    cos, sin, perm = tables
    pair = 2 * GLA_DK
    pv = 2 * GLA_DV
    n_pair = GLA_HEADS // 2
    seq = lambda rows, n, off: pl.BlockSpec((1, rows, n), lambda b, p: (b, 0, p + off))
    const = lambda shape: pl.BlockSpec(shape, lambda b, p: (0,) * len(shape))
    est = (2 * (2 * L * pair * 2 + 2 * L * pv * 2 + L * LR_PAD * 4 + L * pv * 2 + 2 * L * pair * 4)
           + L * pv * 4 + (8 << 20))
    return pl.pallas_call(
        _gla_kernel,
        grid=(B, n_pair),
        in_specs=[
            seq(L, pair, 0), seq(L, pair, n_pair), seq(L, pv, 0), seq(L, pv, 0),
            pl.BlockSpec((1, L, LR_PAD), lambda b, p: (b, 0, 0)),
            seq(Lc, pair, 0), seq(Lc, pair, n_pair), seq(Lc, pv, 0), seq(Lc, pv, 0),
            pl.BlockSpec((1, Lc, LR_PAD), lambda b, p: (b, 0, 0)),
            const((L, pair)), const((L, pair)), const((pair, pair)),
            pl.BlockSpec((2, LR_PAD, pair), lambda b, p: (0, 0, p)),
            pl.BlockSpec((2, 1, pair), lambda b, p: (0, 0, p)),
            const((1, GLA_DV)),
        ],
        out_specs=[seq(L, pv, 0), seq(Lc, pv, 0)],
        out_shape=[jax.ShapeDtypeStruct((B, L, GLA_WIDTH), BF16), jax.ShapeDtypeStruct((B, Lc, GLA_WIDTH), BF16)],
        scratch_shapes=[pltpu.VMEM((L, pv), F32), pltpu.VMEM((Lc, pv), F32)],
        compiler_params=_params(("parallel", "parallel"), est),
        name="gla",
    )(qk, qk, gv, gr, lr, cqk, cqk, cgv, cgr, clr, cos, sin, perm, wgp, bgp, gain.reshape(1, GLA_DV))


def _rope_tables(L):
    pos = np.arange(L)
    half = GLA_DK // 4
    inv = ROPE_BASE ** (-np.arange(half, dtype=np.float64) / half)
    lane = np.arange(2 * GLA_DK)
    jj = lane % (GLA_DK // 2)
    use_col = (lane % GLA_DK) >= GLA_DK // 2
    p = np.where(use_col[None, :], (pos % GRID_W)[:, None], (pos // GRID_W)[:, None]).astype(np.float64)
    ang = p * inv[jj % half][None, :]
    first = jj < half
    cos = np.cos(ang)
    sin = np.where(first[None, :], -np.sin(ang), np.sin(ang))
    partner = np.where(first, lane + half, lane - half)
    perm = np.zeros((2 * GLA_DK, 2 * GLA_DK), np.float32)
    perm[partner, lane] = 1.0
    return jnp.asarray(cos, F32), jnp.asarray(sin, F32), jnp.asarray(perm, BF16)


def _softmax_pv(s_parts, v_parts):
    m = s_parts[0].max(axis=-1, keepdims=True)
    for s in s_parts[1:]:
        m = jnp.maximum(m, s.max(axis=-1, keepdims=True))
    den = None
    acc = None
    for s, v in zip(s_parts, v_parts):
        p = jnp.exp(s - m)
        den = p.sum(axis=-1, keepdims=True) if den is None else den + p.sum(axis=-1, keepdims=True)
        pv = _dot(p.astype(BF16), v)
        acc = pv if acc is None else acc + pv
    return acc / den


def _na_kernel(q_ref, k_ref, v_ref, ck_ref, cv_ref, bias_ref, o_ref):
    W = GRID_W
    rows = k_ref.shape[1] // W
    kh = bias_ref.shape[3] // W
    r0 = pl.program_id(1) * NA_ROWS_PER_STEP
    pair = 2 * NA_DH
    lane = lax.broadcasted_iota(jnp.int32, (1, pair), 1)
    head0 = lane < NA_DH
    scale = NA_DH ** -0.5

    def row_body(rr, carry):
        r = r0 + rr
        rs = jnp.clip(r - kh // 2, 0, rows - kh)
        dr = rs - r + NA_WIN_H - 1
        qs = pl.ds(pl.multiple_of(rr * W, W), W)
        ks = pl.ds(pl.multiple_of(rs * W, W), kh * W)
        outs = []
        for p in range(NA_HEADS // 2):
            ls = slice(p * pair, (p + 1) * pair)
            q = q_ref[0, qs, ls]
            zero = jnp.zeros_like(q)
            q2 = jnp.concatenate([jnp.where(head0, q, zero), jnp.where(head0, zero, q)], axis=0)
            v_loc = v_ref[0, ks, ls]
            v_ctx = cv_ref[0, :, ls]
            bias = jnp.concatenate([bias_ref[dr, 2 * p], bias_ref[dr, 2 * p + 1]], axis=0)
            s_loc = _dot_nt(q2, k_ref[0, ks, ls]) * scale + bias
            s_ctx = _dot_nt(q2, ck_ref[0, :, ls]) * scale
            o2 = _softmax_pv([s_loc, s_ctx], [v_loc, v_ctx])
            outs.append(jnp.where(head0, o2[:W], o2[W:]))
        o_ref[0, qs, :] = jnp.concatenate(outs, axis=1).astype(o_ref.dtype)
        return carry

    lax.fori_loop(0, NA_ROWS_PER_STEP, row_body, 0)


def _na(nq, nk, nv, cnk, cnv, bias):
    B, L, n = nq.shape
    Lc = cnk.shape[1]
    tq = NA_ROWS_PER_STEP * GRID_W
    full = lambda rows: pl.BlockSpec((1, rows, n), lambda b, i: (b, 0, 0))
    est = 2 * (2 * L * n * 2 + 2 * Lc * n * 2 + bias.size * 4 + 2 * tq * n * 2) + (8 << 20)
    return pl.pallas_call(
        _na_kernel,
        grid=(B, L // tq),
        in_specs=[
            pl.BlockSpec((1, tq, n), lambda b, i: (b, i, 0)),
            full(L), full(L), full(Lc), full(Lc),
            pl.BlockSpec(bias.shape, lambda b, i: (0, 0, 0, 0)),
        ],
        out_specs=pl.BlockSpec((1, tq, n), lambda b, i: (b, i, 0)),
        out_shape=jax.ShapeDtypeStruct((B, L, n), BF16),
        compiler_params=_params(("parallel", "parallel"), est),
        name="neighbourhood_attention",
    )(nq, nk, nv, cnk, cnv, bias)


def _na_bias_table(rpb, rows):
    kh = min(NA_WIN_H, rows)
    cols = np.arange(GRID_W)
    start = np.clip(cols - NA_WIN_W // 2, 0, GRID_W - NA_WIN_W)
    kc = np.arange(GRID_W)
    inside = (kc[None, :] >= start[:, None]) & (kc[None, :] < start[:, None] + NA_WIN_W)
    coff = np.clip(kc[None, :] - cols[:, None] + NA_WIN_W - 1, 0, 2 * NA_WIN_W - 2)
    n_dr = NA_WIN_H
    tabs = []
    for dri in range(n_dr):
        roff = dri + np.arange(kh)
        t = rpb[:, roff][:, :, coff]
        t = jnp.where(inside[None, None], t, MASK_VALUE)
        tabs.append(jnp.transpose(t, (0, 2, 1, 3)).reshape(NA_HEADS, GRID_W, kh * GRID_W))
    return jnp.stack(tabs, axis=0).astype(F32)


def _ctx_attn_kernel(q_ref, k_ref, v_ref, o_ref):
    pair = 2 * NA_DH
    lane = lax.broadcasted_iota(jnp.int32, (1, pair), 1)
    head0 = lane < NA_DH
    Lc = q_ref.shape[1]
    outs = []
    for p in range(NA_HEADS // 2):
        ls = slice(p * pair, (p + 1) * pair)
        q = q_ref[0, :, ls]
        zero = jnp.zeros_like(q)
        q2 = jnp.concatenate([jnp.where(head0, q, zero), jnp.where(head0, zero, q)], axis=0)
        s = _dot_nt(q2, k_ref[0, :, ls]) * (NA_DH ** -0.5)
        o2 = _softmax_pv([s], [v_ref[0, :, ls]])
        outs.append(jnp.where(head0, o2[:Lc], o2[Lc:]))
    o_ref[0] = jnp.concatenate(outs, axis=1).astype(o_ref.dtype)


def _ctx_attn(cnq, cnk, cnv):
    B, Lc, n = cnq.shape
    spec = pl.BlockSpec((1, Lc, n), lambda b: (b, 0, 0))
    return pl.pallas_call(
        _ctx_attn_kernel,
        grid=(B,),
        in_specs=[spec, spec, spec],
        out_specs=spec,
        out_shape=jax.ShapeDtypeStruct((B, Lc, n), BF16),
        compiler_params=_params(("parallel",), 8 * Lc * n * 2 + (8 << 20)),
        name="context_attention",
    )(cnq, cnk, cnv)


def _pack_bf16_pairs(x):
    n = x.shape[1] // 2
    lo = lax.bitcast_convert_type(x[:, :n].astype(BF16).astype(F32), jnp.int32)
    hi = lax.bitcast_convert_type(x[:, n:].astype(BF16).astype(F32), jnp.int32)
    return lax.shift_right_logical(lo, 16) | (hi & jnp.int32(-65536))


def _unpack_bf16_pairs(p):
    lo = lax.bitcast_convert_type(lax.shift_left(p, 16), F32)
    hi = lax.bitcast_convert_type(p & jnp.int32(-65536), F32)
    return jnp.concatenate([lo, hi], axis=1)


def _store_packed(refs, x):
    n = 2 * SC_ROW_WORDS
    for p, ref in enumerate(refs):
        ref[...] = _pack_bf16_pairs(x[:, p * n:(p + 1) * n]).reshape(ref.shape)


def _load_packed(refs):
    return jnp.concatenate([_unpack_bf16_pairs(ref[...].reshape(ref.shape[-2:])) for ref in refs], axis=1)


def _route(logits):
    lane = lax.broadcasted_iota(jnp.int32, logits.shape, 1)
    big = jnp.int32(logits.shape[1])
    t1 = logits.max(axis=-1, keepdims=True)
    i1 = jnp.where(logits == t1, lane, big).min(axis=-1, keepdims=True)
    rest = jnp.where(lane == i1, -jnp.inf, logits)
    t2 = rest.max(axis=-1, keepdims=True)
    i2 = jnp.where(rest == t2, lane, big).min(axis=-1, keepdims=True)
    e2 = jnp.exp(t2 - t1)
    return i1, i2, 1.0 / (1.0 + e2), e2 / (1.0 + e2)


def _outproj_kernel(*refs, with_router):
    if with_router:
        (ga_ref, na_ref, x_ref, mod_ref, wa_ref, wb_ref, gp_ref, gf_ref, wr_ref,
         x1_ref, gw_ref, eid_ref, cnt_ref, *h2_refs) = refs
    else:
        ga_ref, na_ref, x_ref, mod_ref, wa_ref, wb_ref, gp_ref, gf_ref, x1_ref, h2_ref = refs
    m = mod_ref[0]
    y = _dot(ga_ref[0], wa_ref[...]) + _dot(na_ref[0], wb_ref[...])
    x1 = x_ref[0] + m[2:3] * (_rms(y) * gp_ref[...])
    x1_ref[0] = x1
    h2 = _rms(x1) * gf_ref[...] * (1.0 + m[4:5]) + m[3:4]
    if not with_router:
        h2_ref[0] = h2.astype(h2_ref.dtype)
    else:
        _store_packed(h2_refs, h2)
        lane =lax.broadcasted_iota(jnp.int32, (h2.shape[0], wr_ref.shape[1]), 1)
        logits = jnp.where(lane < N_EXPERTS, _dot3(h2, wr_ref[...]), -jnp.inf)
        i1, i2, w1, w2 = _route(logits)
        gw_ref[0] = jnp.where(lane == 0, w1, jnp.where(lane == 1, w2, 0.0))
        eid_ref[0] = jnp.where(lane == 0, i1, jnp.where(lane == 1, i2, 0))
        chosen = jnp.where(lane == i1, 1.0, jnp.where(lane == i2, 1.0, 0.0))
        cnt_ref[0] = jnp.broadcast_to(chosen.sum(axis=0, keepdims=True), cnt_ref.shape[1:])


def _outproj(ga, na, x, mod, mod_row0, w_out, g_post, g_ffn, w_router=None):
    nb, rows, d = x.shape
    tm = min(TOKEN_TILE, rows)
    with_router = w_router is not None
    tok = lambda n: pl.BlockSpec((1, tm, n), lambda b, i: (b, i, 0))
    const = lambda shape: pl.BlockSpec(shape, lambda b, i: (0,) * len(shape))
    wa = w_out[:GLA_WIDTH].astype(BF16)
    wb = w_out[GLA_WIDTH:].astype(BF16)
    args = [ga, na, x, mod, wa, wb, g_post.reshape(1, d), g_ffn.reshape(1, d)]
    in_specs = [tok(GLA_WIDTH), tok(NA_WIDTH), tok(d),
                pl.BlockSpec((1, 6, d), lambda b, i: (b + mod_row0, 0, 0)),
                const(wa.shape), const(wb.shape), const((1, d)), const((1, d))]
    out_specs = [tok(d), tok(d)]
    out_shape = [jax.ShapeDtypeStruct((nb, rows, d), F32), jax.ShapeDtypeStruct((nb, rows, d), BF16)]
    if with_router:
        nt = rows // tm
        wr = jnp.zeros((d, V7X_LANES), F32).at[:, :N_EXPERTS].set(w_router)
        args.append(wr)
        in_specs.append(const(wr.shape))
        n_parts = d // (2 * SC_ROW_WORDS)
        out_specs = [tok(d), tok(V7X_LANES), tok(V7X_LANES),
                     pl.BlockSpec((1, 8, V7X_LANES), lambda b, i: (b * nt + i, 0, 0))] + [tok(SC_ROW_WORDS)] * n_parts
        out_shape = [out_shape[0],
                     jax.ShapeDtypeStruct((nb, rows, V7X_LANES), F32),
                     jax.ShapeDtypeStruct((nb, rows, V7X_LANES), jnp.int32),
                     jax.ShapeDtypeStruct((nb * nt, 8, V7X_LANES), F32)]
        out_shape += [jax.ShapeDtypeStruct((nb, rows, SC_ROW_WORDS), jnp.int32)] * n_parts
    est = 2 * (tm * d * (4 + 4 + 2) + 2 * tm * GLA_WIDTH * 2 + d * d * 2) + 4 * tm * d * 4
    return pl.pallas_call(
        functools.partial(_outproj_kernel, with_router=with_router),
        grid=(nb, rows // tm),
        in_specs=in_specs,
        out_specs=out_specs,
        out_shape=out_shape,
        compiler_params=_params(("parallel", "parallel"), est),
        name="outproj",
    )(*args)


def _ffn_kernel(h_ref, x_ref, mod_ref, wg_ref, wu_ref, wd_ref, g_ref, o_ref):
    h = h_ref[0]
    hid = (_silu(_dot(h, wg_ref[...])) * _dot(h, wu_ref[...])).astype(BF16)
    y = _dot(hid, wd_ref[...])
    o_ref[0] = x_ref[0] + mod_ref[0][5:6] * (_rms(y) * g_ref[...])


def _ffn(h2, x1, mod, mod_row0, wg, wu, wd, g_post):
    nb, rows, d = x1.shape
    dff = wg.shape[1]
    tm = min(TOKEN_TILE, rows)
    tok = lambda n: pl.BlockSpec((1, tm, n), lambda b, i: (b, i, 0))
    const = lambda shape: pl.BlockSpec(shape, lambda b, i: (0,) * len(shape), pipeline_mode=pl.Buffered(1))
    est = 3 * d * dff * 2 + 2 * tm * d * (2 + 4 + 4) + 3 * tm * dff * 4 + tm * d * 4
    return pl.pallas_call(
        _ffn_kernel,
        grid=(nb, rows // tm),
        in_specs=[tok(d), tok(d), pl.BlockSpec((1, 6, d), lambda b, i: (b + mod_row0, 0, 0)),
                  const((d, dff)), const((d, dff)), const((dff, d)),
                  pl.BlockSpec((1, d), lambda b, i: (0, 0))],
        out_specs=tok(d),
        out_shape=jax.ShapeDtypeStruct((nb, rows, d), F32),
        compiler_params=_params(("parallel", "parallel"), est),
        name="swiglu_ffn",
    )(h2, x1, mod, wg.astype(BF16), wu.astype(BF16), wd.astype(BF16), g_post.reshape(1, d))


def _slot_kernel(eid_ref, base_ref, pos_ref):
    eid = eid_ref[...]
    tm, lanes = eid.shape
    lane = lax.broadcasted_iota(jnp.int32, (tm, lanes), 1)
    i1 = eid[:, 0:1]
    i2 = eid[:, 1:2]
    chosen = jnp.where(lane == i1, 1.0, jnp.where(lane == i2, 1.0, 0.0)).astype(BF16)
    row = lax.broadcasted_iota(jnp.int32, (tm, tm), 0)
    col = lax.broadcasted_iota(jnp.int32, (tm, tm), 1)
    incl = jnp.where(row >= col, 1.0, 0.0).astype(BF16)
    slot = base_ref[0][0:1] + _dot(incl, chosen) - 1.0
    p1 = jnp.where(lane == i1, slot, 0.0).sum(axis=-1, keepdims=True).astype(jnp.int32)
    p2 = jnp.where(lane == i2, slot, 0.0).sum(axis=-1, keepdims=True).astype(jnp.int32)
    pos_ref[...] = jnp.where(lane == 0, p1, jnp.where(lane == 1, p2, 0))


def _slots(eid, tile_base, tm):
    t, lanes = eid.shape
    return pl.pallas_call(
        _slot_kernel,
        grid=(t // tm,),
        in_specs=[pl.BlockSpec((tm, lanes), lambda i: (i, 0)),
                  pl.BlockSpec((1, 8, lanes), lambda i: (i, 0, 0))],
        out_specs=pl.BlockSpec((tm, lanes), lambda i: (i, 0)),
        out_shape=jax.ShapeDtypeStruct((t, lanes), jnp.int32),
        compiler_params=_params(("parallel",), 8 * tm * lanes * 4 + 4 * tm * tm),
        name="moe_slots",
    )(eid, tile_base)


def _route_plan(cnt, tm_tokens):
    counts = cnt[:, 0, :N_EXPERTS].astype(jnp.int32)
    total = counts.sum(axis=0)
    padded = -(-total // MOE_ROW_TILE) * MOE_ROW_TILE
    start = jnp.cumsum(padded) - padded
    before = jnp.cumsum(counts, axis=0) - counts
    tile_base = (start[None, :] + before).astype(F32)
    tile_base = jnp.zeros((cnt.shape[0], 8, V7X_LANES), F32).at[:, :, :N_EXPERTS].set(tile_base[:, None, :])
    n_slots = tm_tokens * 2 + N_EXPERTS * MOE_ROW_TILE
    first_row = jnp.arange(n_slots // MOE_ROW_TILE, dtype=jnp.int32) * MOE_ROW_TILE
    tile_expert = jnp.minimum((first_row[:, None] >= (start + padded)[None, :]).sum(axis=1), N_EXPERTS - 1)
    n_valid = jnp.clip(start[tile_expert] + total[tile_expert] - first_row, 0, MOE_ROW_TILE)
    return tile_base, tile_expert.astype(jnp.int32), n_valid.astype(jnp.int32), n_slots


def _sc_mesh():
    return plsc.VectorSubcoreMesh(core_axis_name="core", subcore_axis_name="subcore")


def _scatter_rows(x, idx, n_out):
    t, w = x.shape
    n = idx.shape[0]
    win = SC_ROW_WINDOW
    n_blk = t // win

    @functools.partial(pl.kernel, out_type=jax.ShapeDtypeStruct((n_out, w), x.dtype), mesh=_sc_mesh(),
                       scratch_types=[], name="moe_dispatch")
    def scatter(x_hbm, i_hbm, o_hbm):
        def body(x_vmem, i_vmem):
            pltpu.sync_copy(x_vmem, o_hbm.at[i_vmem.at[0]])

        pltpu.emit_pipeline(
            body,
            grid=(n // win,),
            in_specs=[pl.BlockSpec((win, w), lambda i: (i % n_blk, 0)),
                      pl.BlockSpec((1, win), lambda i: (0, i))],
            out_specs=[],
            core_axis_name=("core", "subcore"),
            dimension_semantics=(pltpu.PARALLEL,),
        )(x_hbm, i_hbm)

    return scatter(x, idx.reshape(1, n))


def _gather_rows(x, idx):
    n = idx.shape[0]
    w = x.shape[1]
    win = SC_ROW_WINDOW

    @functools.partial(pl.kernel, out_type=jax.ShapeDtypeStruct((n, w), x.dtype), mesh=_sc_mesh(),
                       scratch_types=[], name="moe_combine_gather")
    def gather(x_hbm, i_hbm, o_hbm):
        def body(i_vmem, o_vmem):
            pltpu.sync_copy(x_hbm.at[i_vmem.at[0]], o_vmem)

        pltpu.emit_pipeline(
            body,
            grid=(n // win,),
            in_specs=[pl.BlockSpec((1, win), lambda i: (0, i))],
            out_specs=[pl.BlockSpec((win, w), lambda i: (i, 0))],
            core_axis_name=("core", "subcore"),
            dimension_semantics=(pltpu.PARALLEL,),
        )(i_hbm, o_hbm)

    return gather(x, idx.reshape(1, n))


def _experts_kernel(te_ref, nv_ref, *refs, n_parts):
    x_refs = refs[:n_parts]
    wg_ref, wu_ref, wd_ref = refs[n_parts:n_parts + 3]
    y_refs = refs[n_parts + 3:2 * n_parts + 3]
    acc_ref = refs[-1]
    i = pl.program_id(0)
    c = pl.program_id(1)
    last = c == pl.num_programs(1) - 1
    n_valid = nv_ref[i]

    @pl.when(n_valid > 0)
    def _():
        row = lax.broadcasted_iota(jnp.int32, (acc_ref.shape[0], 1), 0)
        x = jnp.where(row < n_valid, _load_packed(x_refs), 0.0).astype(BF16)
        hid = (_silu(_dot(x, wg_ref[0])) * _dot(x, wu_ref[0])).astype(BF16)
        part = _dot(hid, wd_ref[0])

        @pl.when(c == 0)
        def _():
            acc_ref[...] = part

        @pl.when(c > 0)
        def _():
            acc_ref[...] += part

        @pl.when(last)
        def _():
            _store_packed(y_refs, acc_ref[...])

    @pl.when(jnp.logical_and(n_valid == 0, last))
    def _():
        for ref in y_refs:
            ref[...] = jnp.zeros_like(ref)


def _experts(xs_parts, tile_expert, n_valid, wg, wu, wd):
    n_parts = len(xs_parts)
    n_slots, words = xs_parts[0].shape
    n_e, d, dff = wg.shape
    tm = MOE_ROW_TILE
    fc = MOE_FF_CHUNK
    nc = dff // fc
    chunk = lambda i, c: jnp.where(i % 2 == 0, c, nc - 1 - c)
    rows_spec = pl.BlockSpec((tm, words), lambda i, c, te, nv: (i, 0))
    est = 2 * (3 * d * fc * 2 + 2 * tm * d * 2) + tm * d * 4 + 3 * tm * fc * 4 + tm * d * 4
    grid_spec = pltpu.PrefetchScalarGridSpec(
        num_scalar_prefetch=2,
        grid=(n_slots // tm, nc),
        in_specs=[rows_spec] * n_parts + [
            pl.BlockSpec((1, d, fc), lambda i, c, te, nv: (te[i], 0, chunk(i, c))),
            pl.BlockSpec((1, d, fc), lambda i, c, te, nv: (te[i], 0, chunk(i, c))),
            pl.BlockSpec((1, fc, d), lambda i, c, te, nv: (te[i], chunk(i, c), 0))],
        out_specs=[rows_spec] * n_parts,
        scratch_shapes=[pltpu.VMEM((tm, d), F32)],
    )
    return pl.pallas_call(
        functools.partial(_experts_kernel, n_parts=n_parts),
        grid_spec=grid_spec,
        out_shape=[jax.ShapeDtypeStruct((n_slots, words), jnp.int32)] * n_parts,
        compiler_params=_params(("parallel", "arbitrary"), est),
        name="moe_experts",
    )(tile_expert, n_valid, *xs_parts, wg, wu, wd)


def _combine_kernel(*refs, n_parts):
    y1_refs = refs[:n_parts]
    y2_refs = refs[n_parts:2 * n_parts]
    gw_ref, x_ref, mod_ref, g_ref, o_ref = refs[2 * n_parts:]
    gw = gw_ref[0]
    y = gw[:, 0:1] * _load_packed(y1_refs) + gw[:, 1:2] * _load_packed(y2_refs)
    o_ref[0] = x_ref[0] + mod_ref[0][5:6] * (_rms(y) * g_ref[...])


def _combine(ys2_parts, gw, x1, mod, mod_row0, g_post):
    n_parts = len(ys2_parts)
    nb, rows, d = x1.shape
    words = ys2_parts[0].shape[-1]
    tm = min(TOKEN_TILE, rows)
    tok = lambda n: pl.BlockSpec((1, tm, n), lambda b, i: (b, i, 0))
    ysp = lambda k: pl.BlockSpec((1, tm, words), lambda b, i: (k * nb + b, i, 0))
    est = 2 * tm * (d * 2 * 4 + d * 4 + V7X_LANES * 4) + 3 * tm * d * 4
    return pl.pallas_call(
        functools.partial(_combine_kernel, n_parts=n_parts),
        grid=(nb, rows // tm),
        in_specs=[ysp(0)] * n_parts + [ysp(1)] * n_parts + [
            tok(V7X_LANES), tok(d),
            pl.BlockSpec((1, 6, d), lambda b, i: (b + mod_row0, 0, 0)),
            pl.BlockSpec((1, d), lambda b, i: (0, 0))],
        out_specs=tok(d),
        out_shape=jax.ShapeDtypeStruct((nb, rows, d), F32),
        compiler_params=_params(("parallel", "parallel"), est),
        name="moe_combine",
    )(*ys2_parts, *ys2_parts, gw, x1, mod, g_post.reshape(1, d))


def _moe_routed(gw, eid, cnt, *h2_parts, x1, mod, mod_row0, wg, wu, wd, g_post):
    nb, rows, d = x1.shape
    t = nb * rows
    tm = min(TOKEN_TILE, rows)
    tile_base, tile_expert, n_valid, n_slots = _route_plan(cnt, t)
    pos = _slots(eid.reshape(t, V7X_LANES), tile_base, tm)
    idx = jnp.concatenate([pos[:, 0], pos[:, 1]])
    xs = [_scatter_rows(h.reshape(t, h.shape[-1]), idx, n_slots) for h in h2_parts]
    ys = _experts(xs, tile_expert, n_valid, wg.astype(BF16), wu.astype(BF16), wd.astype(BF16))
    ys2 = [_gather_rows(y, idx).reshape(2 * nb, rows, y.shape[-1]) for y in ys]
    return _combine(ys2, gw, x1, mod, mod_row0, g_post)


def _pack_w_in(w):
    a = 2 * GLA_KDIM + 2 * GLA_WIDTH
    lr = 2 * GLA_GATE_RANK
    pad = jnp.zeros((w.shape[0], LR_PAD - lr), w.dtype)
    return jnp.concatenate([w[:, :a], w[:, a + lr:], w[:, a:a + lr], pad], axis=1).astype(BF16)


def _pack_gate(w_gate, b_gate):
    wgp = jnp.zeros((2, LR_PAD, GLA_KDIM), F32)
    for d in range(2):
        wgp = wgp.at[d, d * GLA_GATE_RANK:(d + 1) * GLA_GATE_RANK].set(w_gate[d])
    return wgp, b_gate.reshape(2, 1, GLA_KDIM)


def kernel(x, c, ctx, c_ctx, w_ada, b_ada, g_pre_mix, g_post_mix, g_pre_ffn, g_post_ffn, w_in,
           gla_w_gate, gla_b_gate, gla_g_norm, na_rpb, w_out, ffn_w_gate, ffn_w_up, ffn_w_down,
           moe_w_router, moe_w_gate, moe_w_up, moe_w_down):
    B, L, D = x.shape
    Lc = ctx.shape[1]
    depth = w_ada.shape[0]
    rows = L // GRID_W
    ctx_row = B
    n_cond = -(-(B + 1) // 8) * 8
    cond = jnp.zeros((n_cond, D), F32).at[:B].set(c).at[B].set(c_ctx)
    mod = _modulation(cond, w_ada, b_ada)
    tables = _rope_tables(L)
    ctx = ctx.reshape(1, B * Lc, D)

    for i in range(depth):
        last = i == depth - 1
        j = i // 2
        w = _pack_w_in(w_in[i])
        wgp, bgp = _pack_gate(gla_w_gate[i], gla_b_gate[i])
        qk, gv, gr, nq, nk, nv, lr = _inproj(x, mod[i], 0, g_pre_mix[i], w)
        cparts = _inproj(ctx, mod[i], ctx_row, g_pre_mix[i], w)
        cqk, cgv, cgr, cnq, cnk, cnv, clr = [t.reshape(B, Lc, t.shape[-1]) for t in cparts]
        ga, gac = _gla(qk, gv, gr, lr, cqk, cgv, cgr, clr, tables, wgp, bgp, gla_g_norm[i])
        na = _na(nq, nk, nv, cnk, cnv, _na_bias_table(na_rpb[i], rows))
        if i % 2 == 0:
            x1, h2 = _outproj(ga, na, x, mod[i], 0, w_out[i], g_post_mix[i], g_pre_ffn[i])
            x = _ffn(h2, x1, mod[i], 0, ffn_w_gate[j], ffn_w_up[j], ffn_w_down[j], g_post_ffn[i])
        else:
            x1, *routed = _outproj(ga, na, x, mod[i], 0, w_out[i], g_post_mix[i], g_pre_ffn[i], moe_w_router[j])
            x = _moe_routed(*routed, x1=x1, mod=mod[i], mod_row0=0, wg=moe_w_gate[j], wu=moe_w_up[j],
                            wd=moe_w_down[j], g_post=g_post_ffn[i])
        if not last:
            nac = _ctx_attn(cnq, cnk, cnv)
            gac = gac.reshape(1, B * Lc, GLA_WIDTH)
            nac = nac.reshape(1, B * Lc, NA_WIDTH)
            if i % 2 == 0:
                c1, ch2 = _outproj(gac, nac, ctx, mod[i], ctx_row, w_out[i], g_post_mix[i], g_pre_ffn[i])
                ctx = _ffn(ch2, c1, mod[i], ctx_row, ffn_w_gate[j], ffn_w_up[j], ffn_w_down[j], g_post_ffn[i])
            else:
                c1, *routed = _outproj(gac, nac, ctx, mod[i], ctx_row, w_out[i], g_post_mix[i], g_pre_ffn[i],
                                       moe_w_router[j])
                ctx = _moe_routed(*routed, x1=c1, mod=mod[i], mod_row0=ctx_row, wg=moe_w_gate[j], wu=moe_w_up[j],
                                  wd=moe_w_down[j], g_post=g_post_ffn[i])
    return x
```

```python
import functools

import numpy as np
import jax
import jax.numpy as jnp
from jax import lax
from jax.experimental import pallas as pl
from jax.experimental.pallas import tpu as pltpu
from jax.experimental.pallas import tpu_sc as plsc

F32 = jnp.float32
BF16 = jnp.bfloat16

GRID_W = 64
GLA_HEADS = 4
GLA_DV = 128
GLA_DK = 64
GLA_KDIM = GLA_HEADS * GLA_DK
GLA_WIDTH = GLA_HEADS * GLA_DV
GLA_GATE_RANK = 16
GLA_GATE_NORM = 16.0
NA_HEADS = 8
NA_DH = 64
NA_WIDTH = NA_HEADS * NA_DH
NA_WIN_H = 8
NA_WIN_W = 16
ROPE_BASE = 10000.0
N_EXPERTS = 8
EPS = 1e-6

V7X_LANES = 128
V7X_VMEM_BYTES = 64 * 1024 * 1024
V7X_VMEM_USABLE = V7X_VMEM_BYTES - 8 * 1024 * 1024

TOKEN_TILE = 512
GLA_CHUNK = 128
NA_ROWS_PER_STEP = 8
MASK_VALUE = -1e30
MOE_FF_CHUNK = 1792
MOE_ROW_TILE = 512
SC_ROW_WINDOW = 128
SC_ROW_WORDS = 256
LR_PAD = V7X_LANES


def _vmem_limit(estimate_bytes):
    return int(min(V7X_VMEM_USABLE, estimate_bytes * 5 // 4 + (4 << 20)))


def _params(semantics, vmem_estimate):
    return pltpu.CompilerParams(dimension_semantics=semantics, vmem_limit_bytes=_vmem_limit(vmem_estimate))


def _dot(a, b):
    return jnp.dot(a, b, preferred_element_type=F32)


def _dot_nt(a, b):
    return lax.dot_general(a, b, (((1,), (1,)), ((), ())), preferred_element_type=F32)


def _dot_tn(a, b):
    return lax.dot_general(a, b, (((0,), (0,)), ((), ())), preferred_element_type=F32)


def _split(x):
    hi = x.astype(BF16)
    lo = (x - hi.astype(F32)).astype(BF16)
    return hi, lo


def _dot3(a, b):
    ah, al = _split(a)
    bh, bl = _split(b)
    return _dot(ah, bh) + (_dot(al, bh) + _dot(ah, bl))


def _sigmoid(x):
    return 1.0 / (1.0 + jnp.exp(-x))


def _silu(x):
    return x * _sigmoid(x)


def _rms(x):
    return x * lax.rsqrt(jnp.mean(x * x, axis=-1, keepdims=True) + EPS)


def _mod_kernel(c_ref, w_ref, b_ref, o_ref):
    o_ref[0] = _dot3(_silu(c_ref[...]), w_ref[0]) + b_ref[0]


def _modulation(cond, w_ada, b_ada):
    depth, d, n = w_ada.shape
    rows = cond.shape[0]
    tn = 512
    out = pl.pallas_call(
        _mod_kernel,
        grid=(depth, n // tn),
        in_specs=[
            pl.BlockSpec((rows, d), lambda l, j: (0, 0)),
            pl.BlockSpec((1, d, tn), lambda l, j: (l, 0, j)),
            pl.BlockSpec((1, 1, tn), lambda l, j: (l, 0, j)),
        ],
        out_specs=pl.BlockSpec((1, rows, tn), lambda l, j: (l, 0, j)),
        out_shape=jax.ShapeDtypeStruct((depth, rows, n), F32),
        compiler_params=_params(("parallel", "parallel"), 3 * d * tn * 4 * 2),
        name="modulation",
    )(cond, w_ada, b_ada.reshape(depth, 1, n))
    return out.reshape(depth, rows, 6, d)


def _log_decay(logit):
    return (jnp.minimum(logit, 0.0) - jnp.log(1.0 + jnp.exp(-jnp.abs(logit)))) * (1.0 / GLA_GATE_NORM)


def _inproj_kernel(*refs, rope):
    if rope:
        x_ref, mod_ref, g_ref, w_ref, wg_ref, bg_ref, cos_ref, sin_ref = refs[:8]
    else:
        x_ref, mod_ref, g_ref, w_ref, wg_ref, bg_ref = refs[:6]
    qk_ref, gv_ref, gr_ref, nq_ref, nk_ref, nv_ref, bf_ref, bb_ref = refs[-8:]
    m = mod_ref[0]
    h = (_rms(x_ref[0]) * g_ref[...] * (1.0 + m[1:2]) + m[0:1]).astype(BF16)
    tm = h.shape[0]
    col = 0
    n = qk_ref.shape[-1]
    qk = _dot(h, w_ref[:, col:col + n])
    col += n
    lane = lax.broadcasted_iota(jnp.int32, (1, n), 1)
    if rope:
        reps = n // cos_ref.shape[-1]
        cos = jnp.concatenate([cos_ref[...]] * reps, axis=1)
        sin = jnp.concatenate([sin_ref[...]] * reps, axis=1)
        quarter = GLA_DK // 4
        first = (lane % (2 * quarter)) < quarter
        partner = jnp.where(first, pltpu.roll(qk, n - quarter, 1), pltpu.roll(qk, quarter, 1))
        qk = qk * cos + partner * sin
    qk_ref[0] = jnp.where(lane < GLA_KDIM, qk * (GLA_DK ** -0.5), qk).astype(qk_ref.dtype)
    for ref in (gv_ref, gr_ref, nq_ref, nk_ref, nv_ref):
        n = ref.shape[-1]
        ref[0] = _dot(h, w_ref[:, col:col + n]).astype(ref.dtype)
        col += n
    lr = _dot(h, w_ref[:, col:col + LR_PAD])
    C = GLA_CHUNK
    row = lax.broadcasted_iota(jnp.int32, (C, C), 0)
    colm = lax.broadcasted_iota(jnp.int32, (C, C), 1)
    for d, out_ref in enumerate((bf_ref, bb_ref)):
        g = _log_decay(_dot3(lr, wg_ref[d]) + bg_ref[d])
        tri = (row >= colm) if d == 0 else (row <= colm)
        tmat = jnp.where(tri, 1.0, 0.0).astype(BF16)
        gh, gl = _split(g)
        for c in range(tm // C):
            sl = slice(c * C, (c + 1) * C)
            out_ref[0, sl, :] = _dot(tmat, gh[sl]) + _dot(tmat, gl[sl])


def _inproj(x, mod, mod_row0, g, w, wgp, bgp, rope_tables=None):
    nb, rows, d = x.shape
    tm = min(TOKEN_TILE, rows)
    rope = rope_tables is not None
    widths = (2 * GLA_KDIM, GLA_WIDTH, GLA_WIDTH, NA_WIDTH, NA_WIDTH, NA_WIDTH, GLA_KDIM, GLA_KDIM)
    dtypes = (BF16,) * 6 + (F32, F32)
    tok = lambda n: pl.BlockSpec((1, tm, n), lambda b, i: (b, i, 0))
    const = lambda shape: pl.BlockSpec(shape, lambda b, i: (0,) * len(shape))
    args = [x, mod, g.reshape(1, d), w, wgp, bgp]
    in_specs = [tok(d), pl.BlockSpec((1, 6, d), lambda b, i: (b + mod_row0, 0, 0)), const((1, d)),
                const(w.shape), const(wgp.shape), const(bgp.shape)]
    if rope:
        args += list(rope_tables)
        in_specs += [pl.BlockSpec((tm, t.shape[1]), lambda b, i: (i, 0)) for t in rope_tables]
    est = 2 * (tm * d * 4 + d * w.shape[1] * 2 + sum(widths) * tm * 4) + tm * d * 8
    return pl.pallas_call(
        functools.partial(_inproj_kernel, rope=rope),
        grid=(nb, rows // tm),
        in_specs=in_specs,
        out_specs=[tok(n) for n in widths],
        out_shape=[jax.ShapeDtypeStruct((nb, rows, n), dt) for n, dt in zip(widths, dtypes)],
        compiler_params=_params(("parallel", "parallel"), est),
        name="inproj",
    )(*args)


def _gla_kernel(q_ref, k_ref, v_ref, r_ref, bf_ref, bb_ref, cq_ref, ck_ref, cv_ref, cr_ref, cbf_ref, cbb_ref,
                gain_ref, o_ref, oc_ref, of_ref, ocf_ref, st_ref):
    C = GLA_CHUNK
    L = q_ref.shape[1]
    Lc = cq_ref.shape[1]
    row = lax.broadcasted_iota(jnp.int32, (C, C), 0)
    col = lax.broadcasted_iota(jnp.int32, (C, C), 1)
    lane = lax.broadcasted_iota(jnp.int32, (1, 2 * GLA_DK), 1)
    head0 = lane < GLA_DK
    st_row = lax.broadcasted_iota(jnp.int32, (2 * GLA_DV, 2 * GLA_DK), 0) // GLA_DV
    st_col = lax.broadcasted_iota(jnp.int32, (2 * GLA_DV, 2 * GLA_DK), 1) // GLA_DK
    blockdiag = st_row == st_col
    gain = gain_ref[...]

    def chunk_step(refs, i, d, acc_ref):
        rq, rk, rv, rb = refs[0], refs[1], refs[2], refs[3 + d]
        sl = pl.ds(pl.multiple_of(i * C, C), C)
        v = rv[0, sl, :]
        qf = rq[0, sl, :].astype(F32)
        kf = rk[0, sl, :].astype(F32)
        b = rb[0, sl, :]
        tri = (row >= col) if d == 0 else (row <= col)
        b_mid = b[C // 2:C // 2 + 1]
        b_edge = b[C - 1:C] if d == 0 else b[0:1]
        qe = (qf * jnp.exp(b - b_mid)).astype(BF16)
        ke = (kf * jnp.exp(b_mid - b)).astype(BF16)
        zero = jnp.zeros_like(qe)
        lhs = jnp.concatenate([jnp.where(head0, qe, zero), jnp.where(head0, zero, qe)], axis=0)
        a = _dot_nt(lhs, ke)
        a = jnp.where(jnp.concatenate([tri, tri], axis=0), a, 0.0).astype(BF16)
        o = jnp.concatenate([_dot(a[:C], v[:, :GLA_DV]), _dot(a[C:], v[:, GLA_DV:])], axis=1)
        qb = (qf * jnp.exp(b)).astype(BF16)
        st = st_ref[d]
        o = o + _dot_nt(qb, st.astype(BF16))
        kd = (kf * jnp.exp(b_edge - b)).astype(BF16)
        ds = _dot_tn(v, kd)
        st_ref[d] = st * jnp.exp(b_edge) + jnp.where(blockdiag, ds, 0.0)
        acc_ref[d, sl, :] = o

    def finish(total, r):
        outs = []
        for h in range(2):
            oh = _rms(total[:, h * GLA_DV:(h + 1) * GLA_DV]) * gain
            outs.append(oh * _silu(r[:, h * GLA_DV:(h + 1) * GLA_DV].astype(F32)))
        return jnp.concatenate(outs, axis=1)

    def scan(refs, n, acc_ref, r_ref_, out_ref):
        def body(j, carry):
            chunk_step(refs, j, 0, acc_ref)
            chunk_step(refs, n - 1 - j, 1, acc_ref)
            return carry

        lax.fori_loop(0, n, body, 0)

        def fin(i, carry):
            sl = pl.ds(pl.multiple_of(i * C, C), C)
            out_ref[0, sl, :] = finish(acc_ref[0, sl, :] + acc_ref[1, sl, :], r_ref_[0, sl, :]).astype(out_ref.dtype)
            return carry

        lax.fori_loop(0, n, fin, 0)

    st_ref[...] = jnp.zeros_like(st_ref)
    scan((cq_ref, ck_ref, cv_ref, cbf_ref, cbb_ref), Lc // C, ocf_ref, cr_ref, oc_ref)
    scan((q_ref, k_ref, v_ref, bf_ref, bb_ref), L // C, of_ref, r_ref, o_ref)


def _gla(qk, gv, gr, bf, bb, cqk, cgv, cgr, cbf, cbb, gain):
    B, L, _ = qk.shape
    Lc = cqk.shape[1]
    pair = 2 * GLA_DK
    pv = 2 * GLA_DV
    n_pair = GLA_HEADS // 2
    seq = lambda rows, n, off: pl.BlockSpec((1, rows, n), lambda b, p: (b, 0, p + off))
    const = lambda shape: pl.BlockSpec(shape, lambda b, p: (0,) * len(shape))
    est = (2 * (2 * L * pair * 2 + 2 * L * pv * 2 + 2 * L * pair * 4 + L * pv * 2)
           + 2 * L * pv * 4 + (8 << 20))
    return pl.pallas_call(
        _gla_kernel,
        grid=(B, n_pair),
        in_specs=[
            seq(L, pair, 0), seq(L, pair, n_pair), seq(L, pv, 0), seq(L, pv, 0), seq(L, pair, 0), seq(L, pair, 0),
            seq(Lc, pair, 0), seq(Lc, pair, n_pair), seq(Lc, pv, 0), seq(Lc, pv, 0), seq(Lc, pair, 0),
            seq(Lc, pair, 0),
            const((1, GLA_DV)),
        ],
        out_specs=[seq(L, pv, 0), seq(Lc, pv, 0)],
        out_shape=[jax.ShapeDtypeStruct((B, L, GLA_WIDTH), BF16), jax.ShapeDtypeStruct((B, Lc, GLA_WIDTH), BF16)],
        scratch_shapes=[pltpu.VMEM((2, L, pv), F32), pltpu.VMEM((2, Lc, pv), F32), pltpu.VMEM((2, pv, pair), F32)],
        compiler_params=_params(("parallel", "parallel"), est),
        name="gla",
    )(qk, qk, gv, gr, bf, bb, cqk, cqk, cgv, cgr, cbf, cbb, gain.reshape(1, GLA_DV))


def _rope_tables(L):
    pos = np.arange(L)
    half = GLA_DK // 4
    inv = ROPE_BASE ** (-np.arange(half, dtype=np.float64) / half)
    lane = np.arange(2 * GLA_DK)
    jj = lane % (GLA_DK // 2)
    use_col = (lane % GLA_DK) >= GLA_DK // 2
    p = np.where(use_col[None, :], (pos % GRID_W)[:, None], (pos // GRID_W)[:, None]).astype(np.float64)
    ang = p * inv[jj % half][None, :]
    first = jj < half
    cos = np.cos(ang)
    sin = np.where(first[None, :], -np.sin(ang), np.sin(ang))
    return jnp.asarray(cos, F32), jnp.asarray(sin, F32)


def _softmax_pv(s_parts, v_parts):
    m = s_parts[0].max(axis=-1, keepdims=True)
    for s in s_parts[1:]:
        m = jnp.maximum(m, s.max(axis=-1, keepdims=True))
    den = None
    acc = None
    for s, v in zip(s_parts, v_parts):
        p = jnp.exp(s - m)
        den = p.sum(axis=-1, keepdims=True) if den is None else den + p.sum(axis=-1, keepdims=True)
        pv = _dot(p.astype(BF16), v)
        acc = pv if acc is None else acc + pv
    return acc / den


def _na_window_start():
    cols = np.arange(GRID_W)
    return np.clip(cols - NA_WIN_W // 2, 0, GRID_W - NA_WIN_W)


def _na_query_blocks():
    start = _na_window_start()
    n_tiles = GRID_W // NA_WIN_W
    blocks = []
    for o in range(GRID_W // 8):
        first = int(start[8 * o:8 * o + 8].min()) // NA_WIN_W
        last = int(start[8 * o:8 * o + 8].max() + NA_WIN_W - 1) // NA_WIN_W
        t0 = min(first, n_tiles - 2)
        assert t0 <= first and last < t0 + 2
        if blocks and blocks[-1][2] == t0:
            blocks[-1] = (blocks[-1][0], 8 * o + 8, t0)
        else:
            blocks.append((8 * o, 8 * o + 8, t0))
    return blocks


def _na_kernel(q_ref, k_ref, v_ref, ck_ref, cv_ref, bias_ref, o_ref):
    W = GRID_W
    TW = NA_WIN_W
    rows = k_ref.shape[1] // W
    kh = bias_ref.shape[3] // W
    tile = kh * TW
    n_keys = kh * W
    r0 = pl.program_id(1) * NA_ROWS_PER_STEP
    pair = 2 * NA_DH
    lane = lax.broadcasted_iota(jnp.int32, (1, pair), 1)
    head0 = lane < NA_DH
    scale = jnp.asarray(NA_DH ** -0.5, q_ref.dtype)
    blocks = _na_query_blocks()

    def window(ref, base, ls):
        return jnp.concatenate(
            [ref[0, pl.ds(pl.multiple_of(base + i * W + j * TW, TW), TW), ls]
             for j in range(W // TW) for i in range(kh)], axis=0)

    def row_body(rr, carry):
        r = r0 + rr
        rs = jnp.clip(r - kh // 2, 0, rows - kh)
        dr = rs - r + NA_WIN_H - 1
        qs = pl.ds(pl.multiple_of(rr * W, W), W)
        base = rs * W
        outs = []
        for p in range(NA_HEADS // 2):
            ls = slice(p * pair, (p + 1) * pair)
            q = q_ref[0, qs, ls] * scale
            zero = jnp.zeros_like(q)
            q2 = jnp.concatenate([jnp.where(head0, q, zero), jnp.where(head0, zero, q)], axis=0)
            s_loc = _dot_nt(q2, window(k_ref, base, ls))
            s_ctx = _dot_nt(q2, ck_ref[0, :, ls])
            p_loc, p_ctx, den = [], [], []
            for hh in range(2):
                for lo, hi, t0 in blocks:
                    rsl = slice(hh * W + lo, hh * W + hi)
                    lsl = slice(t0 * tile, (t0 + 2) * tile)
                    sl = s_loc[rsl, lsl] + bias_ref[2 * p + hh, dr, lo:hi, lsl]
                    sc = s_ctx[rsl]
                    m = jnp.maximum(sl.max(axis=-1, keepdims=True), sc.max(axis=-1, keepdims=True))
                    el = jnp.exp(sl - m)
                    ec = jnp.exp(sc - m)
                    den.append(el.sum(axis=-1, keepdims=True) + ec.sum(axis=-1, keepdims=True))
                    parts = [el]
                    if t0 > 0:
                        parts.insert(0, jnp.zeros((hi - lo, t0 * tile), F32))
                    if (t0 + 2) * tile < n_keys:
                        parts.append(jnp.zeros((hi - lo, n_keys - (t0 + 2) * tile), F32))
                    p_loc.append(jnp.concatenate(parts, axis=1))
                    p_ctx.append(ec)
            pv = (_dot(jnp.concatenate(p_loc, axis=0).astype(BF16), window(v_ref, base, ls))
                  + _dot(jnp.concatenate(p_ctx, axis=0).astype(BF16), cv_ref[0, :, ls]))
            o2 = pv / jnp.concatenate(den, axis=0)
            outs.append(jnp.where(head0, o2[:W], o2[W:]))
        o_ref[0, qs, :] = jnp.concatenate(outs, axis=1).astype(o_ref.dtype)
        return carry

    lax.fori_loop(0, NA_ROWS_PER_STEP, row_body, 0)


def _na(nq, nk, nv, cnk, cnv, bias):
    B, L, n = nq.shape
    Lc = cnk.shape[1]
    tq = NA_ROWS_PER_STEP * GRID_W
    full = lambda rows: pl.BlockSpec((1, rows, n), lambda b, i: (b, 0, 0))
    est = 2 * (2 * L * n * 2 + 2 * Lc * n * 2 + bias.size * 4 + 2 * tq * n * 2) + (8 << 20)
    return pl.pallas_call(
        _na_kernel,
        grid=(B, L // tq),
        in_specs=[
            pl.BlockSpec((1, tq, n), lambda b, i: (b, i, 0)),
            full(L), full(L), full(Lc), full(Lc),
            pl.BlockSpec(bias.shape, lambda b, i: (0, 0, 0, 0)),
        ],
        out_specs=pl.BlockSpec((1, tq, n), lambda b, i: (b, i, 0)),
        out_shape=jax.ShapeDtypeStruct((B, L, n), BF16),
        compiler_params=_params(("parallel", "parallel"), est),
        name="neighbourhood_attention",
    )(nq, nk, nv, cnk, cnv, bias)


def _na_bias_table(rpb, rows):
    kh = min(NA_WIN_H, rows)
    start = _na_window_start()
    kc = np.arange(GRID_W)
    inside = (kc[None, :] >= start[:, None]) & (kc[None, :] < start[:, None] + NA_WIN_W)
    sel = np.zeros((2 * NA_WIN_W - 1, GRID_W, GRID_W), np.float32)
    qq, kk = np.nonzero(inside)
    sel[kk - qq + NA_WIN_W - 1, qq, kk] = 1.0
    by_row = jnp.stack([rpb[:, d:d + kh, :] for d in range(NA_WIN_H)], axis=1)
    t = jnp.einsum('hdic,cqk->hdqik', by_row, jnp.asarray(sel), precision=lax.Precision.HIGHEST)
    t = t + jnp.asarray(np.where(inside, 0.0, MASK_VALUE), F32)[None, None, :, None, :]
    n_tiles = GRID_W // NA_WIN_W
    t = t.reshape(NA_HEADS, NA_WIN_H, GRID_W, kh, n_tiles, NA_WIN_W)
    return jnp.transpose(t, (0, 1, 2, 4, 3, 5)).reshape(NA_HEADS, NA_WIN_H, GRID_W, kh * GRID_W)


def _ctx_attn_kernel(q_ref, k_ref, v_ref, o_ref):
    pair = 2 * NA_DH
    lane = lax.broadcasted_iota(jnp.int32, (1, pair), 1)
    head0 = lane < NA_DH
    Lc = q_ref.shape[1]
    outs = []
    for p in range(NA_HEADS // 2):
        ls = slice(p * pair, (p + 1) * pair)
        q = q_ref[0, :, ls]
        zero = jnp.zeros_like(q)
        q2 = jnp.concatenate([jnp.where(head0, q, zero), jnp.where(head0, zero, q)], axis=0)
        s = _dot_nt(q2, k_ref[0, :, ls]) * (NA_DH ** -0.5)
        o2 = _softmax_pv([s], [v_ref[0, :, ls]])
        outs.append(jnp.where(head0, o2[:Lc], o2[Lc:]))
    o_ref[0] = jnp.concatenate(outs, axis=1).astype(o_ref.dtype)


def _ctx_attn(cnq, cnk, cnv):
    B, Lc, n = cnq.shape
    spec = pl.BlockSpec((1, Lc, n), lambda b: (b, 0, 0))
    return pl.pallas_call(
        _ctx_attn_kernel,
        grid=(B,),
        in_specs=[spec, spec, spec],
        out_specs=spec,
        out_shape=jax.ShapeDtypeStruct((B, Lc, n), BF16),
        compiler_params=_params(("parallel",), 8 * Lc * n * 2 + (8 << 20)),
        name="context_attention",
    )(cnq, cnk, cnv)


def _pack_bf16_pairs(x):
    n = x.shape[1] // 2
    lo = lax.bitcast_convert_type(x[:, :n].astype(BF16).astype(F32), jnp.int32)
    hi = lax.bitcast_convert_type(x[:, n:].astype(BF16).astype(F32), jnp.int32)
    return lax.shift_right_logical(lo, 16) | (hi & jnp.int32(-65536))


def _unpack_bf16_pairs(p):
    lo = lax.bitcast_convert_type(lax.shift_left(p, 16), F32)
    hi = lax.bitcast_convert_type(p & jnp.int32(-65536), F32)
    return jnp.concatenate([lo, hi], axis=1)


def _store_packed(refs, x):
    n = 2 * SC_ROW_WORDS
    for p, ref in enumerate(refs):
        ref[...] = _pack_bf16_pairs(x[:, p * n:(p + 1) * n]).reshape(ref.shape)


def _load_packed(refs):
    return jnp.concatenate([_unpack_bf16_pairs(ref[...].reshape(ref.shape[-2:])) for ref in refs], axis=1)


def _route(logits):
    lane = lax.broadcasted_iota(jnp.int32, logits.shape, 1)
    big = jnp.int32(logits.shape[1])
    t1 = logits.max(axis=-1, keepdims=True)
    i1 = jnp.where(logits == t1, lane, big).min(axis=-1, keepdims=True)
    rest = jnp.where(lane == i1, -jnp.inf, logits)
    t2 = rest.max(axis=-1, keepdims=True)
    i2 = jnp.where(rest == t2, lane, big).min(axis=-1, keepdims=True)
    e2 = jnp.exp(t2 - t1)
    return i1, i2, 1.0 / (1.0 + e2), e2 / (1.0 + e2)


def _outproj_kernel(*refs, with_router):
    if with_router:
        (ga_ref, na_ref, x_ref, mod_ref, wa_ref, wb_ref, gp_ref, gf_ref, wr_ref,
         x1_ref, gw_ref, eid_ref, cnt_ref, *h2_refs) = refs
    else:
        ga_ref, na_ref, x_ref, mod_ref, wa_ref, wb_ref, gp_ref, gf_ref, x1_ref, h2_ref = refs
    m = mod_ref[0]
    y = _dot(ga_ref[0], wa_ref[...]) + _dot(na_ref[0], wb_ref[...])
    x1 = x_ref[0] + m[2:3] * (_rms(y) * gp_ref[...])
    x1_ref[0] = x1
    h2 = _rms(x1) * gf_ref[...] * (1.0 + m[4:5]) + m[3:4]
    if not with_router:
        h2_ref[0] = h2.astype(h2_ref.dtype)
    else:
        _store_packed(h2_refs, h2)
        lane =lax.broadcasted_iota(jnp.int32, (h2.shape[0], wr_ref.shape[1]), 1)
        logits = jnp.where(lane < N_EXPERTS, _dot3(h2, wr_ref[...]), -jnp.inf)
        i1, i2, w1, w2 = _route(logits)
        gw_ref[0] = jnp.where(lane == 0, w1, jnp.where(lane == 1, w2, 0.0))
        eid_ref[0] = jnp.where(lane == 0, i1, jnp.where(lane == 1, i2, 0))
        chosen = jnp.where(lane == i1, 1.0, jnp.where(lane == i2, 1.0, 0.0))
        cnt_ref[0] = jnp.broadcast_to(chosen.sum(axis=0, keepdims=True), cnt_ref.shape[1:])


def _outproj(ga, na, x, mod, mod_row0, w_out, g_post, g_ffn, w_router=None):
    nb, rows, d = x.shape
    tm = min(TOKEN_TILE, rows)
    with_router = w_router is not None
    tok = lambda n: pl.BlockSpec((1, tm, n), lambda b, i: (b, i, 0))
    const = lambda shape: pl.BlockSpec(shape, lambda b, i: (0,) * len(shape))
    wa = w_out[:GLA_WIDTH].astype(BF16)
    wb = w_out[GLA_WIDTH:].astype(BF16)
    args = [ga, na, x, mod, wa, wb, g_post.reshape(1, d), g_ffn.reshape(1, d)]
    in_specs = [tok(GLA_WIDTH), tok(NA_WIDTH), tok(d),
                pl.BlockSpec((1, 6, d), lambda b, i: (b + mod_row0, 0, 0)),
                const(wa.shape), const(wb.shape), const((1, d)), const((1, d))]
    out_specs = [tok(d), tok(d)]
    out_shape = [jax.ShapeDtypeStruct((nb, rows, d), F32), jax.ShapeDtypeStruct((nb, rows, d), BF16)]
    if with_router:
        nt = rows // tm
        wr = jnp.zeros((d, V7X_LANES), F32).at[:, :N_EXPERTS].set(w_router)
        args.append(wr)
        in_specs.append(const(wr.shape))
        n_parts = d // (2 * SC_ROW_WORDS)
        out_specs = [tok(d), tok(V7X_LANES), tok(V7X_LANES),
                     pl.BlockSpec((1, 8, V7X_LANES), lambda b, i: (b * nt + i, 0, 0))] + [tok(SC_ROW_WORDS)] * n_parts
        out_shape = [out_shape[0],
                     jax.ShapeDtypeStruct((nb, rows, V7X_LANES), F32),
                     jax.ShapeDtypeStruct((nb, rows, V7X_LANES), jnp.int32),
                     jax.ShapeDtypeStruct((nb * nt, 8, V7X_LANES), F32)]
        out_shape += [jax.ShapeDtypeStruct((nb, rows, SC_ROW_WORDS), jnp.int32)] * n_parts
    est = 2 * (tm * d * (4 + 4 + 2) + 2 * tm * GLA_WIDTH * 2 + d * d * 2) + 4 * tm * d * 4
    return pl.pallas_call(
        functools.partial(_outproj_kernel, with_router=with_router),
        grid=(nb, rows // tm),
        in_specs=in_specs,
        out_specs=out_specs,
        out_shape=out_shape,
        compiler_params=_params(("parallel", "parallel"), est),
        name="outproj",
    )(*args)


def _ffn_kernel(h_ref, x_ref, mod_ref, wg_ref, wu_ref, wd_ref, g_ref, o_ref):
    h = h_ref[0]
    hid = (_silu(_dot(h, wg_ref[...])) * _dot(h, wu_ref[...])).astype(BF16)
    y = _dot(hid, wd_ref[...])
    o_ref[0] = x_ref[0] + mod_ref[0][5:6] * (_rms(y) * g_ref[...])


def _ffn(h2, x1, mod, mod_row0, wg, wu, wd, g_post):
    nb, rows, d = x1.shape
    dff = wg.shape[1]
    tm = min(TOKEN_TILE, rows)
    tok = lambda n: pl.BlockSpec((1, tm, n), lambda b, i: (b, i, 0))
    const = lambda shape: pl.BlockSpec(shape, lambda b, i: (0,) * len(shape), pipeline_mode=pl.Buffered(1))
    est = 3 * d * dff * 2 + 2 * tm * d * (2 + 4 + 4) + 3 * tm * dff * 4 + tm * d * 4
    return pl.pallas_call(
        _ffn_kernel,
        grid=(nb, rows // tm),
        in_specs=[tok(d), tok(d), pl.BlockSpec((1, 6, d), lambda b, i: (b + mod_row0, 0, 0)),
                  const((d, dff)), const((d, dff)), const((dff, d)),
                  pl.BlockSpec((1, d), lambda b, i: (0, 0))],
        out_specs=tok(d),
        out_shape=jax.ShapeDtypeStruct((nb, rows, d), F32),
        compiler_params=_params(("parallel", "parallel"), est),
        name="swiglu_ffn",
    )(h2, x1, mod, wg.astype(BF16), wu.astype(BF16), wd.astype(BF16), g_post.reshape(1, d))


def _slot_kernel(eid_ref, base_ref, pos_ref):
    eid = eid_ref[...]
    tm, lanes = eid.shape
    lane = lax.broadcasted_iota(jnp.int32, (tm, lanes), 1)
    i1 = eid[:, 0:1]
    i2 = eid[:, 1:2]
    chosen = jnp.where(lane == i1, 1.0, jnp.where(lane == i2, 1.0, 0.0)).astype(BF16)
    row = lax.broadcasted_iota(jnp.int32, (tm, tm), 0)
    col = lax.broadcasted_iota(jnp.int32, (tm, tm), 1)
    incl = jnp.where(row >= col, 1.0, 0.0).astype(BF16)
    slot = base_ref[0][0:1] + _dot(incl, chosen) - 1.0
    p1 = jnp.where(lane == i1, slot, 0.0).sum(axis=-1, keepdims=True).astype(jnp.int32)
    p2 = jnp.where(lane == i2, slot, 0.0).sum(axis=-1, keepdims=True).astype(jnp.int32)
    pos_ref[...] = jnp.where(lane == 0, p1, jnp.where(lane == 1, p2, 0))


def _slots(eid, tile_base, tm):
    t, lanes = eid.shape
    return pl.pallas_call(
        _slot_kernel,
        grid=(t // tm,),
        in_specs=[pl.BlockSpec((tm, lanes), lambda i: (i, 0)),
                  pl.BlockSpec((1, 8, lanes), lambda i: (i, 0, 0))],
        out_specs=pl.BlockSpec((tm, lanes), lambda i: (i, 0)),
        out_shape=jax.ShapeDtypeStruct((t, lanes), jnp.int32),
        compiler_params=_params(("parallel",), 8 * tm * lanes * 4 + 4 * tm * tm),
        name="moe_slots",
    )(eid, tile_base)


def _route_plan(cnt, tm_tokens):
    counts = cnt[:, 0, :N_EXPERTS].astype(jnp.int32)
    total = counts.sum(axis=0)
    padded = -(-total // MOE_ROW_TILE) * MOE_ROW_TILE
    start = jnp.cumsum(padded) - padded
    before = jnp.cumsum(counts, axis=0) - counts
    tile_base = (start[None, :] + before).astype(F32)
    tile_base = jnp.zeros((cnt.shape[0], 8, V7X_LANES), F32).at[:, :, :N_EXPERTS].set(tile_base[:, None, :])
    n_slots = tm_tokens * 2 + N_EXPERTS * MOE_ROW_TILE
    first_row = jnp.arange(n_slots // MOE_ROW_TILE, dtype=jnp.int32) * MOE_ROW_TILE
    tile_expert = jnp.minimum((first_row[:, None] >= (start + padded)[None, :]).sum(axis=1), N_EXPERTS - 1)
    n_valid = jnp.clip(start[tile_expert] + total[tile_expert] - first_row, 0, MOE_ROW_TILE)
    return tile_base, tile_expert.astype(jnp.int32), n_valid.astype(jnp.int32), n_slots


def _sc_mesh():
    return plsc.VectorSubcoreMesh(core_axis_name="core", subcore_axis_name="subcore")


def _scatter_rows(x, idx, n_out):
    t, w = x.shape
    n = idx.shape[0]
    win = SC_ROW_WINDOW
    n_blk = t // win

    @functools.partial(pl.kernel, out_type=jax.ShapeDtypeStruct((n_out, w), x.dtype), mesh=_sc_mesh(),
                       scratch_types=[], name="moe_dispatch")
    def scatter(x_hbm, i_hbm, o_hbm):
        def body(x_vmem, i_vmem):
            pltpu.sync_copy(x_vmem, o_hbm.at[i_vmem.at[0]])

        pltpu.emit_pipeline(
            body,
            grid=(n // win,),
            in_specs=[pl.BlockSpec((win, w), lambda i: (i % n_blk, 0)),
                      pl.BlockSpec((1, win), lambda i: (0, i))],
            out_specs=[],
            core_axis_name=("core", "subcore"),
            dimension_semantics=(pltpu.PARALLEL,),
        )(x_hbm, i_hbm)

    return scatter(x, idx.reshape(1, n))


def _gather_rows(x, idx):
    n = idx.shape[0]
    w = x.shape[1]
    win = SC_ROW_WINDOW

    @functools.partial(pl.kernel, out_type=jax.ShapeDtypeStruct((n, w), x.dtype), mesh=_sc_mesh(),
                       scratch_types=[], name="moe_combine_gather")
    def gather(x_hbm, i_hbm, o_hbm):
        def body(i_vmem, o_vmem):
            pltpu.sync_copy(x_hbm.at[i_vmem.at[0]], o_vmem)

        pltpu.emit_pipeline(
            body,
            grid=(n // win,),
            in_specs=[pl.BlockSpec((1, win), lambda i: (0, i))],
            out_specs=[pl.BlockSpec((win, w), lambda i: (i, 0))],
            core_axis_name=("core", "subcore"),
            dimension_semantics=(pltpu.PARALLEL,),
        )(i_hbm, o_hbm)

    return gather(x, idx.reshape(1, n))


def _experts_kernel(te_ref, nv_ref, *refs, n_parts):
    x_refs = refs[:n_parts]
    wg_ref, wu_ref, wd_ref = refs[n_parts:n_parts + 3]
    y_refs = refs[n_parts + 3:2 * n_parts + 3]
    acc_ref = refs[-1]
    i = pl.program_id(0)
    c = pl.program_id(1)
    last = c == pl.num_programs(1) - 1
    n_valid = nv_ref[i]

    @pl.when(n_valid > 0)
    def _():
        row = lax.broadcasted_iota(jnp.int32, (acc_ref.shape[0], 1), 0)
        x = jnp.where(row < n_valid, _load_packed(x_refs), 0.0).astype(BF16)
        hid = (_silu(_dot(x, wg_ref[0])) * _dot(x, wu_ref[0])).astype(BF16)
        part = _dot(hid, wd_ref[0])

        @pl.when(c == 0)
        def _():
            acc_ref[...] = part

        @pl.when(c > 0)
        def _():
            acc_ref[...] += part

        @pl.when(last)
        def _():
            _store_packed(y_refs, acc_ref[...])

    @pl.when(jnp.logical_and(n_valid == 0, last))
    def _():
        for ref in y_refs:
            ref[...] = jnp.zeros_like(ref)


def _experts(xs_parts, tile_expert, n_valid, wg, wu, wd):
    n_parts = len(xs_parts)
    n_slots, words = xs_parts[0].shape
    n_e, d, dff = wg.shape
    tm = MOE_ROW_TILE
    fc = MOE_FF_CHUNK
    nc = dff // fc
    chunk = lambda i, c: jnp.where(i % 2 == 0, c, nc - 1 - c)
    rows_spec = pl.BlockSpec((tm, words), lambda i, c, te, nv: (i, 0))
    est = 2 * (3 * d * fc * 2 + 2 * tm * d * 2) + tm * d * 4 + 3 * tm * fc * 4 + tm * d * 4
    grid_spec = pltpu.PrefetchScalarGridSpec(
        num_scalar_prefetch=2,
        grid=(n_slots // tm, nc),
        in_specs=[rows_spec] * n_parts + [
            pl.BlockSpec((1, d, fc), lambda i, c, te, nv: (te[i], 0, chunk(i, c))),
            pl.BlockSpec((1, d, fc), lambda i, c, te, nv: (te[i], 0, chunk(i, c))),
            pl.BlockSpec((1, fc, d), lambda i, c, te, nv: (te[i], chunk(i, c), 0))],
        out_specs=[rows_spec] * n_parts,
        scratch_shapes=[pltpu.VMEM((tm, d), F32)],
    )
    return pl.pallas_call(
        functools.partial(_experts_kernel, n_parts=n_parts),
        grid_spec=grid_spec,
        out_shape=[jax.ShapeDtypeStruct((n_slots, words), jnp.int32)] * n_parts,
        compiler_params=_params(("parallel", "arbitrary"), est),
        name="moe_experts",
    )(tile_expert, n_valid, *xs_parts, wg, wu, wd)


def _combine_kernel(*refs, n_parts):
    y1_refs = refs[:n_parts]
    y2_refs = refs[n_parts:2 * n_parts]
    gw_ref, x_ref, mod_ref, g_ref, o_ref = refs[2 * n_parts:]
    gw = gw_ref[0]
    y = gw[:, 0:1] * _load_packed(y1_refs) + gw[:, 1:2] * _load_packed(y2_refs)
    o_ref[0] = x_ref[0] + mod_ref[0][5:6] * (_rms(y) * g_ref[...])


def _combine(ys2_parts, gw, x1, mod, mod_row0, g_post):
    n_parts = len(ys2_parts)
    nb, rows, d = x1.shape
    words = ys2_parts[0].shape[-1]
    tm = min(TOKEN_TILE, rows)
    tok = lambda n: pl.BlockSpec((1, tm, n), lambda b, i: (b, i, 0))
    ysp = lambda k: pl.BlockSpec((1, tm, words), lambda b, i: (k * nb + b, i, 0))
    est = 2 * tm * (d * 2 * 4 + d * 4 + V7X_LANES * 4) + 3 * tm * d * 4
    return pl.pallas_call(
        functools.partial(_combine_kernel, n_parts=n_parts),
        grid=(nb, rows // tm),
        in_specs=[ysp(0)] * n_parts + [ysp(1)] * n_parts + [
            tok(V7X_LANES), tok(d),
            pl.BlockSpec((1, 6, d), lambda b, i: (b + mod_row0, 0, 0)),
            pl.BlockSpec((1, d), lambda b, i: (0, 0))],
        out_specs=tok(d),
        out_shape=jax.ShapeDtypeStruct((nb, rows, d), F32),
        compiler_params=_params(("parallel", "parallel"), est),
        name="moe_combine",
    )(*ys2_parts, *ys2_parts, gw, x1, mod, g_post.reshape(1, d))


def _moe_routed(gw, eid, cnt, *h2_parts, x1, mod, mod_row0, wg, wu, wd, g_post):
    nb, rows, d = x1.shape
    t = nb * rows
    tm = min(TOKEN_TILE, rows)
    tile_base, tile_expert, n_valid, n_slots = _route_plan(cnt, t)
    pos = _slots(eid.reshape(t, V7X_LANES), tile_base, tm)
    idx = jnp.concatenate([pos[:, 0], pos[:, 1]])
    xs = [_scatter_rows(h.reshape(t, h.shape[-1]), idx, n_slots) for h in h2_parts]
    ys = _experts(xs, tile_expert, n_valid, wg.astype(BF16), wu.astype(BF16), wd.astype(BF16))
    ys2 = [_gather_rows(y, idx).reshape(2 * nb, rows, y.shape[-1]) for y in ys]
    return _combine(ys2, gw, x1, mod, mod_row0, g_post)


def _pack_w_in(w):
    a = 2 * GLA_KDIM + 2 * GLA_WIDTH
    lr = 2 * GLA_GATE_RANK
    pad = jnp.zeros((w.shape[0], LR_PAD - lr), w.dtype)
    return jnp.concatenate([w[:, :a], w[:, a + lr:], w[:, a:a + lr], pad], axis=1).astype(BF16)


def _pack_gate(w_gate, b_gate):
    wgp = jnp.zeros((2, LR_PAD, GLA_KDIM), F32)
    for d in range(2):
        wgp = wgp.at[d, d * GLA_GATE_RANK:(d + 1) * GLA_GATE_RANK].set(w_gate[d])
    return wgp, b_gate.reshape(2, 1, GLA_KDIM)


def kernel(x, c, ctx, c_ctx, w_ada, b_ada, g_pre_mix, g_post_mix, g_pre_ffn, g_post_ffn, w_in,
           gla_w_gate, gla_b_gate, gla_g_norm, na_rpb, w_out, ffn_w_gate, ffn_w_up, ffn_w_down,
           moe_w_router, moe_w_gate, moe_w_up, moe_w_down):
    B, L, D = x.shape
    Lc = ctx.shape[1]
    depth = w_ada.shape[0]
    rows = L // GRID_W
    ctx_row = B
    n_cond = -(-(B + 1) // 8) * 8
    cond = jnp.zeros((n_cond, D), F32).at[:B].set(c).at[B].set(c_ctx)
    mod = _modulation(cond, w_ada, b_ada)
    tables = _rope_tables(L)
    ctx = ctx.reshape(1, B * Lc, D)

    for i in range(depth):
        last = i == depth - 1
        j = i // 2
        w = _pack_w_in(w_in[i])
        wgp, bgp = _pack_gate(gla_w_gate[i], gla_b_gate[i])
        qk, gv, gr, nq, nk, nv, bf, bb = _inproj(x, mod[i], 0, g_pre_mix[i], w, wgp, bgp, tables)
        cparts = _inproj(ctx, mod[i], ctx_row, g_pre_mix[i], w, wgp, bgp)
        cqk, cgv, cgr, cnq, cnk, cnv, cbf, cbb = [t.reshape(B, Lc, t.shape[-1]) for t in cparts]
        ga, gac = _gla(qk, gv, gr, bf, bb, cqk, cgv, cgr, cbf, cbb, gla_g_norm[i])
        na = _na(nq, nk, nv, cnk, cnv, _na_bias_table(na_rpb[i], rows))
        if i % 2 == 0:
            x1, h2 = _outproj(ga, na, x, mod[i], 0, w_out[i], g_post_mix[i], g_pre_ffn[i])
            x = _ffn(h2, x1, mod[i], 0, ffn_w_gate[j], ffn_w_up[j], ffn_w_down[j], g_post_ffn[i])
        else:
            x1, *routed = _outproj(ga, na, x, mod[i], 0, w_out[i], g_post_mix[i], g_pre_ffn[i], moe_w_router[j])
            x = _moe_routed(*routed, x1=x1, mod=mod[i], mod_row0=0, wg=moe_w_gate[j], wu=moe_w_up[j],
                            wd=moe_w_down[j], g_post=g_post_ffn[i])
        if not last:
            nac = _ctx_attn(cnq, cnk, cnv)
            gac = gac.reshape(1, B * Lc, GLA_WIDTH)
            nac = nac.reshape(1, B * Lc, NA_WIDTH)
            if i % 2 == 0:
                c1, ch2 = _outproj(gac, nac, ctx, mod[i], ctx_row, w_out[i], g_post_mix[i], g_pre_ffn[i])
                ctx = _ffn(ch2, c1, mod[i], ctx_row, ffn_w_gate[j], ffn_w_up[j], ffn_w_down[j], g_post_ffn[i])
            else:
                c1, *routed = _outproj(gac, nac, ctx, mod[i], ctx_row, w_out[i], g_post_mix[i], g_pre_ffn[i],
                                       moe_w_router[j])
                ctx = _moe_routed(*routed, x1=c1, mod=mod[i], mod_row0=ctx_row, wg=moe_w_gate[j], wu=moe_w_up[j],
                                  wd=moe_w_down[j], g_post=g_post_ffn[i])
    return x
```

```python
import functools

import numpy as np
import jax
import jax.numpy as jnp
from jax import lax
from jax.experimental import pallas as pl
from jax.experimental.pallas import tpu as pltpu
from jax.experimental.pallas import tpu_sc as plsc

F32 = jnp.float32
BF16 = jnp.bfloat16

GRID_W = 64
GLA_HEADS = 4
GLA_DV = 128
GLA_DK = 64
GLA_KDIM = GLA_HEADS * GLA_DK
GLA_WIDTH = GLA_HEADS * GLA_DV
GLA_GATE_RANK = 16
GLA_GATE_NORM = 16.0
NA_HEADS = 8
NA_DH = 64
NA_WIDTH = NA_HEADS * NA_DH
NA_WIN_H = 8
NA_WIN_W = 16
ROPE_BASE = 10000.0
N_EXPERTS = 8
EPS = 1e-6

V7X_LANES = 128
V7X_VMEM_BYTES = 64 * 1024 * 1024
V7X_VMEM_USABLE = V7X_VMEM_BYTES - 8 * 1024 * 1024

TOKEN_TILE = 512
GLA_CHUNK = 128
NA_ROWS_PER_STEP = 8
NA_ROWS_PER_ITER = 2
MASK_VALUE = -1e30
MOE_FF_CHUNK = 1792
MOE_ROW_TILE = 512
SC_ROW_WINDOW = 128
SC_ROW_WORDS = 256
LR_PAD = V7X_LANES


def _vmem_limit(estimate_bytes):
    return int(min(V7X_VMEM_USABLE, estimate_bytes * 5 // 4 + (4 << 20)))


def _params(semantics, vmem_estimate):
    return pltpu.CompilerParams(dimension_semantics=semantics, vmem_limit_bytes=_vmem_limit(vmem_estimate))


def _dot(a, b):
    return jnp.dot(a, b, preferred_element_type=F32)


def _dot_nt(a, b):
    return lax.dot_general(a, b, (((1,), (1,)), ((), ())), preferred_element_type=F32)


def _dot_tn(a, b):
    return lax.dot_general(a, b, (((0,), (0,)), ((), ())), preferred_element_type=F32)


def _split(x):
    hi = x.astype(BF16)
    lo = (x - hi.astype(F32)).astype(BF16)
    return hi, lo


def _dot3(a, b):
    ah, al = _split(a)
    bh, bl = _split(b)
    return _dot(ah, bh) + (_dot(al, bh) + _dot(ah, bl))


def _sigmoid(x):
    return 1.0 / (1.0 + jnp.exp(-x))


def _silu(x):
    return x * _sigmoid(x)


def _rms(x):
    return x * lax.rsqrt(jnp.mean(x * x, axis=-1, keepdims=True) + EPS)


def _mod_kernel(c_ref, w_ref, b_ref, o_ref):
    o_ref[0] = _dot3(_silu(c_ref[...]), w_ref[0]) + b_ref[0]


def _modulation(cond, w_ada, b_ada):
    depth, d, n = w_ada.shape
    rows = cond.shape[0]
    tn = 512
    out = pl.pallas_call(
        _mod_kernel,
        grid=(depth, n // tn),
        in_specs=[
            pl.BlockSpec((rows, d), lambda l, j: (0, 0)),
            pl.BlockSpec((1, d, tn), lambda l, j: (l, 0, j)),
            pl.BlockSpec((1, 1, tn), lambda l, j: (l, 0, j)),
        ],
        out_specs=pl.BlockSpec((1, rows, tn), lambda l, j: (l, 0, j)),
        out_shape=jax.ShapeDtypeStruct((depth, rows, n), F32),
        compiler_params=_params(("parallel", "parallel"), 3 * d * tn * 4 * 2),
        name="modulation",
    )(cond, w_ada, b_ada.reshape(depth, 1, n))
    return out.reshape(depth, rows, 6, d)


def _log_decay(logit):
    return (jnp.minimum(logit, 0.0) - jnp.log(1.0 + jnp.exp(-jnp.abs(logit)))) * (1.0 / GLA_GATE_NORM)


def _inproj_kernel(*refs, rope):
    if rope:
        x_ref, mod_ref, g_ref, w_ref, wg_ref, bg_ref, cos_ref, sin_ref = refs[:8]
    else:
        x_ref, mod_ref, g_ref, w_ref, wg_ref, bg_ref = refs[:6]
    qk_ref, gv_ref, gr_ref, nq_ref, nk_ref, nv_ref, bf_ref, bb_ref = refs[-8:]
    m = mod_ref[0]
    h = (_rms(x_ref[0]) * g_ref[...] * (1.0 + m[1:2]) + m[0:1]).astype(BF16)
    tm = h.shape[0]
    col = 0
    n = qk_ref.shape[-1]
    qk = _dot(h, w_ref[:, col:col + n])
    col += n
    lane = lax.broadcasted_iota(jnp.int32, (1, n), 1)
    if rope:
        reps = n // cos_ref.shape[-1]
        cos = jnp.concatenate([cos_ref[...]] * reps, axis=1)
        sin = jnp.concatenate([sin_ref[...]] * reps, axis=1)
        quarter = GLA_DK // 4
        first = (lane % (2 * quarter)) < quarter
        partner = jnp.where(first, pltpu.roll(qk, n - quarter, 1), pltpu.roll(qk, quarter, 1))
        qk = qk * cos + partner * sin
    qk_ref[0] = jnp.where(lane < GLA_KDIM, qk * (GLA_DK ** -0.5), qk).astype(qk_ref.dtype)
    for ref in (gv_ref, gr_ref, nq_ref, nk_ref, nv_ref):
        n = ref.shape[-1]
        ref[0] = _dot(h, w_ref[:, col:col + n]).astype(ref.dtype)
        col += n
    lr = _dot(h, w_ref[:, col:col + LR_PAD])
    C = GLA_CHUNK
    row = lax.broadcasted_iota(jnp.int32, (C, C), 0)
    colm = lax.broadcasted_iota(jnp.int32, (C, C), 1)
    for d, out_ref in enumerate((bf_ref, bb_ref)):
        g = _log_decay(_dot3(lr, wg_ref[d]) + bg_ref[d])
        tri = (row >= colm) if d == 0 else (row <= colm)
        tmat = jnp.where(tri, 1.0, 0.0).astype(BF16)
        gh, gl = _split(g)
        for c in range(tm // C):
            sl = slice(c * C, (c + 1) * C)
            out_ref[0, sl, :] = _dot(tmat, gh[sl]) + _dot(tmat, gl[sl])


def _inproj(x, mod, mod_row0, g, w, wgp, bgp, rope_tables=None):
    nb, rows, d = x.shape
    tm = min(TOKEN_TILE, rows)
    rope = rope_tables is not None
    widths = (2 * GLA_KDIM, GLA_WIDTH, GLA_WIDTH, NA_WIDTH, NA_WIDTH, NA_WIDTH, GLA_KDIM, GLA_KDIM)
    dtypes = (BF16,) * 6 + (F32, F32)
    tok = lambda n: pl.BlockSpec((1, tm, n), lambda b, i: (b, i, 0))
    const = lambda shape: pl.BlockSpec(shape, lambda b, i: (0,) * len(shape))
    args = [x, mod, g.reshape(1, d), w, wgp, bgp]
    in_specs = [tok(d), pl.BlockSpec((1, 6, d), lambda b, i: (b + mod_row0, 0, 0)), const((1, d)),
                const(w.shape), const(wgp.shape), const(bgp.shape)]
    if rope:
        args += list(rope_tables)
        in_specs += [pl.BlockSpec((tm, t.shape[1]), lambda b, i: (i, 0)) for t in rope_tables]
    est = 2 * (tm * d * 4 + d * w.shape[1] * 2 + sum(widths) * tm * 4) + tm * d * 8
    return pl.pallas_call(
        functools.partial(_inproj_kernel, rope=rope),
        grid=(nb, rows // tm),
        in_specs=in_specs,
        out_specs=[tok(n) for n in widths],
        out_shape=[jax.ShapeDtypeStruct((nb, rows, n), dt) for n, dt in zip(widths, dtypes)],
        compiler_params=_params(("parallel", "parallel"), est),
        name="inproj",
    )(*args)


def _gla_kernel(q_ref, k_ref, v_ref, r_ref, bf_ref, bb_ref, cq_ref, ck_ref, cv_ref, cr_ref, cbf_ref, cbb_ref,
                gain_ref, o_ref, oc_ref, of_ref, ocf_ref, st_ref):
    C = GLA_CHUNK
    L = q_ref.shape[1]
    Lc = cq_ref.shape[1]
    row = lax.broadcasted_iota(jnp.int32, (C, C), 0)
    col = lax.broadcasted_iota(jnp.int32, (C, C), 1)
    lane = lax.broadcasted_iota(jnp.int32, (1, 2 * GLA_DK), 1)
    head0 = lane < GLA_DK
    st_row = lax.broadcasted_iota(jnp.int32, (2 * GLA_DV, 2 * GLA_DK), 0) // GLA_DV
    st_col = lax.broadcasted_iota(jnp.int32, (2 * GLA_DV, 2 * GLA_DK), 1) // GLA_DK
    blockdiag = st_row == st_col
    gain = gain_ref[...]

    def chunk_steps(refs, chunks, acc_ref):
        rq, rk, rv = refs[:3]
        dirs = (0, 1)
        sls = [pl.ds(pl.multiple_of(i * C, C), C) for i in chunks]
        tris = [row >= col, row <= col]
        vs = [rv[0, sl, :] for sl in sls]
        qfs = [rq[0, sl, :].astype(F32) for sl in sls]
        kfs = [rk[0, sl, :].astype(F32) for sl in sls]
        bs = [refs[3 + d][0, sls[d], :] for d in dirs]
        b_mids = [b[C // 2:C // 2 + 1] for b in bs]
        b_edges = [bs[0][C - 1:C], bs[1][0:1]]
        qes = [(qfs[d] * jnp.exp(bs[d] - b_mids[d])).astype(BF16) for d in dirs]
        kes = [(kfs[d] * jnp.exp(b_mids[d] - bs[d])).astype(BF16) for d in dirs]
        zero = jnp.zeros_like(qes[0])
        lhss = [jnp.concatenate([jnp.where(head0, qe, zero), jnp.where(head0, zero, qe)], axis=0) for qe in qes]
        scores = [_dot_nt(lhss[d], kes[d]) for d in dirs]
        qbs = [(qfs[d] * jnp.exp(bs[d])).astype(BF16) for d in dirs]
        kds = [(kfs[d] * jnp.exp(b_edges[d] - bs[d])).astype(BF16) for d in dirs]
        sts = [st_ref[d] for d in dirs]
        inter = [_dot_nt(qbs[d], sts[d].astype(BF16)) for d in dirs]
        dss = [_dot_tn(vs[d], kds[d]) for d in dirs]
        ams = [jnp.where(jnp.concatenate([tris[d], tris[d]], axis=0), scores[d], 0.0).astype(BF16) for d in dirs]
        intra = [jnp.concatenate([_dot(ams[d][:C], vs[d][:, :GLA_DV]), _dot(ams[d][C:], vs[d][:, GLA_DV:])], axis=1)
                 for d in dirs]
        for d in dirs:
            st_ref[d] = sts[d] * jnp.exp(b_edges[d]) + jnp.where(blockdiag, dss[d], 0.0)
            acc_ref[d, sls[d], :] = intra[d] + inter[d]

    def finish(total, r):
        outs = []
        for h in range(2):
            oh = _rms(total[:, h * GLA_DV:(h + 1) * GLA_DV]) * gain
            outs.append(oh * _silu(r[:, h * GLA_DV:(h + 1) * GLA_DV].astype(F32)))
        return jnp.concatenate(outs, axis=1)

    def scan(refs, n, acc_ref, r_ref_, out_ref):
        def body(j, carry):
            chunk_steps(refs, (j, n - 1 - j), acc_ref)
            return carry

        lax.fori_loop(0, n, body, 0)

        def fin(i, carry):
            sl = pl.ds(pl.multiple_of(i * C, C), C)
            out_ref[0, sl, :] = finish(acc_ref[0, sl, :] + acc_ref[1, sl, :], r_ref_[0, sl, :]).astype(out_ref.dtype)
            return carry

        lax.fori_loop(0, n, fin, 0)

    st_ref[...] = jnp.zeros_like(st_ref)
    scan((cq_ref, ck_ref, cv_ref, cbf_ref, cbb_ref), Lc // C, ocf_ref, cr_ref, oc_ref)
    scan((q_ref, k_ref, v_ref, bf_ref, bb_ref), L // C, of_ref, r_ref, o_ref)


def _gla(qk, gv, gr, bf, bb, cqk, cgv, cgr, cbf, cbb, gain):
    B, L, _ = qk.shape
    Lc = cqk.shape[1]
    pair = 2 * GLA_DK
    pv = 2 * GLA_DV
    n_pair = GLA_HEADS // 2
    seq = lambda rows, n, off: pl.BlockSpec((1, rows, n), lambda b, p: (b, 0, p + off))
    const = lambda shape: pl.BlockSpec(shape, lambda b, p: (0,) * len(shape))
    est = (2 * (2 * L * pair * 2 + 2 * L * pv * 2 + 2 * L * pair * 4 + L * pv * 2)
           + 2 * L * pv * 4 + (8 << 20))
    return pl.pallas_call(
        _gla_kernel,
        grid=(B, n_pair),
        in_specs=[
            seq(L, pair, 0), seq(L, pair, n_pair), seq(L, pv, 0), seq(L, pv, 0), seq(L, pair, 0), seq(L, pair, 0),
            seq(Lc, pair, 0), seq(Lc, pair, n_pair), seq(Lc, pv, 0), seq(Lc, pv, 0), seq(Lc, pair, 0),
            seq(Lc, pair, 0),
            const((1, GLA_DV)),
        ],
        out_specs=[seq(L, pv, 0), seq(Lc, pv, 0)],
        out_shape=[jax.ShapeDtypeStruct((B, L, GLA_WIDTH), BF16), jax.ShapeDtypeStruct((B, Lc, GLA_WIDTH), BF16)],
        scratch_shapes=[pltpu.VMEM((2, L, pv), F32), pltpu.VMEM((2, Lc, pv), F32), pltpu.VMEM((2, pv, pair), F32)],
        compiler_params=_params(("parallel", "parallel"), est),
        name="gla",
    )(qk, qk, gv, gr, bf, bb, cqk, cqk, cgv, cgr, cbf, cbb, gain.reshape(1, GLA_DV))


def _rope_tables(L):
    pos = np.arange(L)
    half = GLA_DK // 4
    inv = ROPE_BASE ** (-np.arange(half, dtype=np.float64) / half)
    lane = np.arange(2 * GLA_DK)
    jj = lane % (GLA_DK // 2)
    use_col = (lane % GLA_DK) >= GLA_DK // 2
    p = np.where(use_col[None, :], (pos % GRID_W)[:, None], (pos // GRID_W)[:, None]).astype(np.float64)
    ang = p * inv[jj % half][None, :]
    first = jj < half
    cos = np.cos(ang)
    sin = np.where(first[None, :], -np.sin(ang), np.sin(ang))
    return jnp.asarray(cos, F32), jnp.asarray(sin, F32)


def _softmax_pv(s_parts, v_parts):
    m = s_parts[0].max(axis=-1, keepdims=True)
    for s in s_parts[1:]:
        m = jnp.maximum(m, s.max(axis=-1, keepdims=True))
    den = None
    acc = None
    for s, v in zip(s_parts, v_parts):
        p = jnp.exp(s - m)
        den = p.sum(axis=-1, keepdims=True) if den is None else den + p.sum(axis=-1, keepdims=True)
        pv = _dot(p.astype(BF16), v)
        acc = pv if acc is None else acc + pv
    return acc / den


def _na_window_start():
    cols = np.arange(GRID_W)
    return np.clip(cols - NA_WIN_W // 2, 0, GRID_W - NA_WIN_W)


def _na_kernel(q_ref, k_ref, v_ref, ck_ref, cv_ref, bias_ref, o_ref):
    W = GRID_W
    rows = k_ref.shape[1] // W
    n_loc = bias_ref.shape[2]
    kh = n_loc // W
    r0 = pl.program_id(1) * NA_ROWS_PER_STEP
    pair = 2 * NA_DH
    lane = lax.broadcasted_iota(jnp.int32, (1, pair), 1)
    head0 = lane < NA_DH
    scale = jnp.asarray(NA_DH ** -0.5, q_ref.dtype)

    n_pairs = NA_HEADS // 2
    lanes = [slice(p * pair, (p + 1) * pair) for p in range(n_pairs)]

    def rows_body(it, carry):
        units = []
        for j in range(NA_ROWS_PER_ITER):
            rr = it * NA_ROWS_PER_ITER + j
            r = r0 + rr
            rs = jnp.clip(r - kh // 2, 0, rows - kh)
            qs = pl.ds(pl.multiple_of(rr * W, W), W)
            ks = pl.ds(pl.multiple_of(rs * W, W), n_loc)
            units += [(qs, ks, rs - r + NA_WIN_H - 1, p) for p in range(n_pairs)]
        sts = []
        for qs, ks, dr, p in units:
            q = q_ref[0, qs, lanes[p]] * scale
            zero = jnp.zeros_like(q)
            q2 = jnp.concatenate([jnp.where(head0, q, zero), jnp.where(head0, zero, q)], axis=0)
            keys = jnp.concatenate([k_ref[0, ks, lanes[p]], ck_ref[0, :, lanes[p]]], axis=0)
            sts.append(_dot_nt(keys, q2))
        pts = []
        for (qs, ks, dr, p), st in zip(units, sts):
            st = jnp.concatenate([st[:n_loc] + bias_ref[p, dr], st[n_loc:]], axis=0)
            e = jnp.exp(st - st.max(axis=0, keepdims=True))
            pts.append((e * (1.0 / e.sum(axis=0, keepdims=True))).astype(BF16))
        outs = []
        for (qs, ks, dr, p), pt in zip(units, pts):
            vals = jnp.concatenate([v_ref[0, ks, lanes[p]], cv_ref[0, :, lanes[p]]], axis=0)
            o2 = _dot_tn(pt, vals)
            outs.append(jnp.where(head0, o2[:W], o2[W:]))
        for j in range(NA_ROWS_PER_ITER):
            qs = units[j * n_pairs][0]
            o_ref[0, qs, :] = jnp.concatenate(outs[j * n_pairs:(j + 1) * n_pairs], axis=1).astype(o_ref.dtype)
        return carry

    lax.fori_loop(0, NA_ROWS_PER_STEP // NA_ROWS_PER_ITER, rows_body, 0)


def _na(nq, nk, nv, cnk, cnv, bias):
    B, L, n = nq.shape
    Lc = cnk.shape[1]
    tq = NA_ROWS_PER_STEP * GRID_W
    full = lambda rows: pl.BlockSpec((1, rows, n), lambda b, i: (b, 0, 0))
    est = 2 * (2 * L * n * 2 + 2 * Lc * n * 2 + bias.size * 4 + 2 * tq * n * 2) + (8 << 20)
    return pl.pallas_call(
        _na_kernel,
        grid=(B, L // tq),
        in_specs=[
            pl.BlockSpec((1, tq, n), lambda b, i: (b, i, 0)),
            full(L), full(L), full(Lc), full(Lc),
            pl.BlockSpec(bias.shape, lambda b, i: (0, 0, 0, 0)),
        ],
        out_specs=pl.BlockSpec((1, tq, n), lambda b, i: (b, i, 0)),
        out_shape=jax.ShapeDtypeStruct((B, L, n), BF16),
        compiler_params=_params(("parallel", "parallel"), est),
        name="neighbourhood_attention",
    )(nq, nk, nv, cnk, cnv, bias)


def _na_bias_table(rpb, rows):
    kh = min(NA_WIN_H, rows)
    start = _na_window_start()
    kc = np.arange(GRID_W)
    inside = (kc[None, :] >= start[:, None]) & (kc[None, :] < start[:, None] + NA_WIN_W)
    sel = np.zeros((2 * NA_WIN_W - 1, GRID_W, GRID_W), np.float32)
    qq, kk = np.nonzero(inside)
    sel[kk - qq + NA_WIN_W - 1, qq, kk] = 1.0
    by_row = jnp.stack([rpb[:, d:d + kh, :] for d in range(NA_WIN_H)], axis=1)
    t = jnp.einsum('hdic,cqk->hdikq', by_row, jnp.asarray(sel), precision=lax.Precision.HIGHEST)
    t = t + jnp.asarray(np.where(inside, 0.0, MASK_VALUE).T, F32)[None, None, None, :, :]
    t = t.reshape(NA_HEADS // 2, 2, NA_WIN_H, kh * GRID_W, GRID_W)
    return jnp.transpose(t, (0, 2, 3, 1, 4)).reshape(NA_HEADS // 2, NA_WIN_H, kh * GRID_W, 2 * GRID_W)


def _ctx_attn_kernel(q_ref, k_ref, v_ref, o_ref):
    pair = 2 * NA_DH
    lane = lax.broadcasted_iota(jnp.int32, (1, pair), 1)
    head0 = lane < NA_DH
    Lc = q_ref.shape[1]
    outs = []
    for p in range(NA_HEADS // 2):
        ls = slice(p * pair, (p + 1) * pair)
        q = q_ref[0, :, ls]
        zero = jnp.zeros_like(q)
        q2 = jnp.concatenate([jnp.where(head0, q, zero), jnp.where(head0, zero, q)], axis=0)
        s = _dot_nt(q2, k_ref[0, :, ls]) * (NA_DH ** -0.5)
        o2 = _softmax_pv([s], [v_ref[0, :, ls]])
        outs.append(jnp.where(head0, o2[:Lc], o2[Lc:]))
    o_ref[0] = jnp.concatenate(outs, axis=1).astype(o_ref.dtype)


def _ctx_attn(cnq, cnk, cnv):
    B, Lc, n = cnq.shape
    spec = pl.BlockSpec((1, Lc, n), lambda b: (b, 0, 0))
    return pl.pallas_call(
        _ctx_attn_kernel,
        grid=(B,),
        in_specs=[spec, spec, spec],
        out_specs=spec,
        out_shape=jax.ShapeDtypeStruct((B, Lc, n), BF16),
        compiler_params=_params(("parallel",), 8 * Lc * n * 2 + (8 << 20)),
        name="context_attention",
    )(cnq, cnk, cnv)


def _pack_bf16_pairs(x):
    n = x.shape[1] // 2
    lo = lax.bitcast_convert_type(x[:, :n].astype(BF16).astype(F32), jnp.int32)
    hi = lax.bitcast_convert_type(x[:, n:].astype(BF16).astype(F32), jnp.int32)
    return lax.shift_right_logical(lo, 16) | (hi & jnp.int32(-65536))


def _unpack_bf16_pairs(p):
    lo = lax.bitcast_convert_type(lax.shift_left(p, 16), F32)
    hi = lax.bitcast_convert_type(p & jnp.int32(-65536), F32)
    return jnp.concatenate([lo, hi], axis=1)


def _store_packed(refs, x):
    n = 2 * SC_ROW_WORDS
    for p, ref in enumerate(refs):
        ref[...] = _pack_bf16_pairs(x[:, p * n:(p + 1) * n]).reshape(ref.shape)


def _load_packed(refs):
    return jnp.concatenate([_unpack_bf16_pairs(ref[...].reshape(ref.shape[-2:])) for ref in refs], axis=1)


def _route(logits):
    lane = lax.broadcasted_iota(jnp.int32, logits.shape, 1)
    big = jnp.int32(logits.shape[1])
    t1 = logits.max(axis=-1, keepdims=True)
    i1 = jnp.where(logits == t1, lane, big).min(axis=-1, keepdims=True)
    rest = jnp.where(lane == i1, -jnp.inf, logits)
    t2 = rest.max(axis=-1, keepdims=True)
    i2 = jnp.where(rest == t2, lane, big).min(axis=-1, keepdims=True)
    e2 = jnp.exp(t2 - t1)
    return i1, i2, 1.0 / (1.0 + e2), e2 / (1.0 + e2)


def _outproj_kernel(*refs, with_router):
    if with_router:
        (ga_ref, na_ref, x_ref, mod_ref, wa_ref, wb_ref, gp_ref, gf_ref, wr_ref,
         x1_ref, gw_ref, eid_ref, cnt_ref, *h2_refs) = refs
    else:
        ga_ref, na_ref, x_ref, mod_ref, wa_ref, wb_ref, gp_ref, gf_ref, x1_ref, h2_ref = refs
    m = mod_ref[0]
    y = _dot(ga_ref[0], wa_ref[...]) + _dot(na_ref[0], wb_ref[...])
    x1 = x_ref[0] + m[2:3] * (_rms(y) * gp_ref[...])
    x1_ref[0] = x1
    h2 = _rms(x1) * gf_ref[...] * (1.0 + m[4:5]) + m[3:4]
    if not with_router:
        h2_ref[0] = h2.astype(h2_ref.dtype)
    else:
        _store_packed(h2_refs, h2)
        lane =lax.broadcasted_iota(jnp.int32, (h2.shape[0], wr_ref.shape[1]), 1)
        logits = jnp.where(lane < N_EXPERTS, _dot3(h2, wr_ref[...]), -jnp.inf)
        i1, i2, w1, w2 = _route(logits)
        gw_ref[0] = jnp.where(lane == 0, w1, jnp.where(lane == 1, w2, 0.0))
        eid_ref[0] = jnp.where(lane == 0, i1, jnp.where(lane == 1, i2, 0))
        chosen = jnp.where(lane == i1, 1.0, jnp.where(lane == i2, 1.0, 0.0))
        cnt_ref[0] = jnp.broadcast_to(chosen.sum(axis=0, keepdims=True), cnt_ref.shape[1:])


def _outproj(ga, na, x, mod, mod_row0, w_out, g_post, g_ffn, w_router=None):
    nb, rows, d = x.shape
    tm = min(TOKEN_TILE, rows)
    with_router = w_router is not None
    tok = lambda n: pl.BlockSpec((1, tm, n), lambda b, i: (b, i, 0))
    const = lambda shape: pl.BlockSpec(shape, lambda b, i: (0,) * len(shape))
    wa = w_out[:GLA_WIDTH].astype(BF16)
    wb = w_out[GLA_WIDTH:].astype(BF16)
    args = [ga, na, x, mod, wa, wb, g_post.reshape(1, d), g_ffn.reshape(1, d)]
    in_specs = [tok(GLA_WIDTH), tok(NA_WIDTH), tok(d),
                pl.BlockSpec((1, 6, d), lambda b, i: (b + mod_row0, 0, 0)),
                const(wa.shape), const(wb.shape), const((1, d)), const((1, d))]
    out_specs = [tok(d), tok(d)]
    out_shape = [jax.ShapeDtypeStruct((nb, rows, d), F32), jax.ShapeDtypeStruct((nb, rows, d), BF16)]
    if with_router:
        nt = rows // tm
        wr = jnp.zeros((d, V7X_LANES), F32).at[:, :N_EXPERTS].set(w_router)
        args.append(wr)
        in_specs.append(const(wr.shape))
        n_parts = d // (2 * SC_ROW_WORDS)
        out_specs = [tok(d), tok(V7X_LANES), tok(V7X_LANES),
                     pl.BlockSpec((1, 8, V7X_LANES), lambda b, i: (b * nt + i, 0, 0))] + [tok(SC_ROW_WORDS)] * n_parts
        out_shape = [out_shape[0],
                     jax.ShapeDtypeStruct((nb, rows, V7X_LANES), F32),
                     jax.ShapeDtypeStruct((nb, rows, V7X_LANES), jnp.int32),
                     jax.ShapeDtypeStruct((nb * nt, 8, V7X_LANES), F32)]
        out_shape += [jax.ShapeDtypeStruct((nb, rows, SC_ROW_WORDS), jnp.int32)] * n_parts
    est = 2 * (tm * d * (4 + 4 + 2) + 2 * tm * GLA_WIDTH * 2 + d * d * 2) + 4 * tm * d * 4
    return pl.pallas_call(
        functools.partial(_outproj_kernel, with_router=with_router),
        grid=(nb, rows // tm),
        in_specs=in_specs,
        out_specs=out_specs,
        out_shape=out_shape,
        compiler_params=_params(("parallel", "parallel"), est),
        name="outproj",
    )(*args)


def _ffn_kernel(h_ref, x_ref, mod_ref, wg_ref, wu_ref, wd_ref, g_ref, o_ref):
    h = h_ref[0]
    hid = (_silu(_dot(h, wg_ref[...])) * _dot(h, wu_ref[...])).astype(BF16)
    y = _dot(hid, wd_ref[...])
    o_ref[0] = x_ref[0] + mod_ref[0][5:6] * (_rms(y) * g_ref[...])


def _ffn(h2, x1, mod, mod_row0, wg, wu, wd, g_post):
    nb, rows, d = x1.shape
    dff = wg.shape[1]
    tm = min(TOKEN_TILE, rows)
    tok = lambda n: pl.BlockSpec((1, tm, n), lambda b, i: (b, i, 0))
    const = lambda shape: pl.BlockSpec(shape, lambda b, i: (0,) * len(shape), pipeline_mode=pl.Buffered(1))
    est = 3 * d * dff * 2 + 2 * tm * d * (2 + 4 + 4) + 3 * tm * dff * 4 + tm * d * 4
    return pl.pallas_call(
        _ffn_kernel,
        grid=(nb, rows // tm),
        in_specs=[tok(d), tok(d), pl.BlockSpec((1, 6, d), lambda b, i: (b + mod_row0, 0, 0)),
                  const((d, dff)), const((d, dff)), const((dff, d)),
                  pl.BlockSpec((1, d), lambda b, i: (0, 0))],
        out_specs=tok(d),
        out_shape=jax.ShapeDtypeStruct((nb, rows, d), F32),
        compiler_params=_params(("parallel", "parallel"), est),
        name="swiglu_ffn",
    )(h2, x1, mod, wg.astype(BF16), wu.astype(BF16), wd.astype(BF16), g_post.reshape(1, d))


def _slot_kernel(eid_ref, base_ref, pos_ref):
    eid = eid_ref[...]
    tm, lanes = eid.shape
    lane = lax.broadcasted_iota(jnp.int32, (tm, lanes), 1)
    i1 = eid[:, 0:1]
    i2 = eid[:, 1:2]
    chosen = jnp.where(lane == i1, 1.0, jnp.where(lane == i2, 1.0, 0.0)).astype(BF16)
    row = lax.broadcasted_iota(jnp.int32, (tm, tm), 0)
    col = lax.broadcasted_iota(jnp.int32, (tm, tm), 1)
    incl = jnp.where(row >= col, 1.0, 0.0).astype(BF16)
    slot = base_ref[0][0:1] + _dot(incl, chosen) - 1.0
    p1 = jnp.where(lane == i1, slot, 0.0).sum(axis=-1, keepdims=True).astype(jnp.int32)
    p2 = jnp.where(lane == i2, slot, 0.0).sum(axis=-1, keepdims=True).astype(jnp.int32)
    pos_ref[...] = jnp.where(lane == 0, p1, jnp.where(lane == 1, p2, 0))


def _slots(eid, tile_base, tm):
    t, lanes = eid.shape
    return pl.pallas_call(
        _slot_kernel,
        grid=(t // tm,),
        in_specs=[pl.BlockSpec((tm, lanes), lambda i: (i, 0)),
                  pl.BlockSpec((1, 8, lanes), lambda i: (i, 0, 0))],
        out_specs=pl.BlockSpec((tm, lanes), lambda i: (i, 0)),
        out_shape=jax.ShapeDtypeStruct((t, lanes), jnp.int32),
        compiler_params=_params(("parallel",), 8 * tm * lanes * 4 + 4 * tm * tm),
        name="moe_slots",
    )(eid, tile_base)


def _route_plan(cnt, tm_tokens):
    counts = cnt[:, 0, :N_EXPERTS].astype(jnp.int32)
    total = counts.sum(axis=0)
    padded = -(-total // MOE_ROW_TILE) * MOE_ROW_TILE
    start = jnp.cumsum(padded) - padded
    before = jnp.cumsum(counts, axis=0) - counts
    tile_base = (start[None, :] + before).astype(F32)
    tile_base = jnp.zeros((cnt.shape[0], 8, V7X_LANES), F32).at[:, :, :N_EXPERTS].set(tile_base[:, None, :])
    n_slots = tm_tokens * 2 + N_EXPERTS * MOE_ROW_TILE
    first_row = jnp.arange(n_slots // MOE_ROW_TILE, dtype=jnp.int32) * MOE_ROW_TILE
    tile_expert = jnp.minimum((first_row[:, None] >= (start + padded)[None, :]).sum(axis=1), N_EXPERTS - 1)
    n_valid = jnp.clip(start[tile_expert] + total[tile_expert] - first_row, 0, MOE_ROW_TILE)
    return tile_base, tile_expert.astype(jnp.int32), n_valid.astype(jnp.int32), n_slots


def _sc_mesh():
    return plsc.VectorSubcoreMesh(core_axis_name="core", subcore_axis_name="subcore")


def _scatter_rows(x, idx, n_out):
    t, w = x.shape
    n = idx.shape[0]
    win = SC_ROW_WINDOW
    n_blk = t // win

    @functools.partial(pl.kernel, out_type=jax.ShapeDtypeStruct((n_out, w), x.dtype), mesh=_sc_mesh(),
                       scratch_types=[], name="moe_dispatch")
    def scatter(x_hbm, i_hbm, o_hbm):
        def body(x_vmem, i_vmem):
            pltpu.sync_copy(x_vmem, o_hbm.at[i_vmem.at[0]])

        pltpu.emit_pipeline(
            body,
            grid=(n // win,),
            in_specs=[pl.BlockSpec((win, w), lambda i: (i % n_blk, 0)),
                      pl.BlockSpec((1, win), lambda i: (0, i))],
            out_specs=[],
            core_axis_name=("core", "subcore"),
            dimension_semantics=(pltpu.PARALLEL,),
        )(x_hbm, i_hbm)

    return scatter(x, idx.reshape(1, n))


def _gather_rows(x, idx):
    n = idx.shape[0]
    w = x.shape[1]
    win = SC_ROW_WINDOW

    @functools.partial(pl.kernel, out_type=jax.ShapeDtypeStruct((n, w), x.dtype), mesh=_sc_mesh(),
                       scratch_types=[], name="moe_combine_gather")
    def gather(x_hbm, i_hbm, o_hbm):
        def body(i_vmem, o_vmem):
            pltpu.sync_copy(x_hbm.at[i_vmem.at[0]], o_vmem)

        pltpu.emit_pipeline(
            body,
            grid=(n // win,),
            in_specs=[pl.BlockSpec((1, win), lambda i: (0, i))],
            out_specs=[pl.BlockSpec((win, w), lambda i: (i, 0))],
            core_axis_name=("core", "subcore"),
            dimension_semantics=(pltpu.PARALLEL,),
        )(i_hbm, o_hbm)

    return gather(x, idx.reshape(1, n))


def _experts_kernel(te_ref, nv_ref, *refs, n_parts):
    x_refs = refs[:n_parts]
    wg_ref, wu_ref, wd_ref = refs[n_parts:n_parts + 3]
    y_refs = refs[n_parts + 3:2 * n_parts + 3]
    acc_ref = refs[-1]
    i = pl.program_id(0)
    c = pl.program_id(1)
    last = c == pl.num_programs(1) - 1
    n_valid = nv_ref[i]

    @pl.when(n_valid > 0)
    def _():
        row = lax.broadcasted_iota(jnp.int32, (acc_ref.shape[0], 1), 0)
        x = jnp.where(row < n_valid, _load_packed(x_refs), 0.0).astype(BF16)
        hid = (_silu(_dot(x, wg_ref[0])) * _dot(x, wu_ref[0])).astype(BF16)
        part = _dot(hid, wd_ref[0])

        @pl.when(c == 0)
        def _():
            acc_ref[...] = part

        @pl.when(c > 0)
        def _():
            acc_ref[...] += part

        @pl.when(last)
        def _():
            _store_packed(y_refs, acc_ref[...])

    @pl.when(jnp.logical_and(n_valid == 0, last))
    def _():
        for ref in y_refs:
            ref[...] = jnp.zeros_like(ref)


def _experts(xs_parts, tile_expert, n_valid, wg, wu, wd):
    n_parts = len(xs_parts)
    n_slots, words = xs_parts[0].shape
    n_e, d, dff = wg.shape
    tm = MOE_ROW_TILE
    fc = MOE_FF_CHUNK
    nc = dff // fc
    chunk = lambda i, c: jnp.where(i % 2 == 0, c, nc - 1 - c)
    rows_spec = pl.BlockSpec((tm, words), lambda i, c, te, nv: (i, 0))
    est = 2 * (3 * d * fc * 2 + 2 * tm * d * 2) + tm * d * 4 + 3 * tm * fc * 4 + tm * d * 4
    grid_spec = pltpu.PrefetchScalarGridSpec(
        num_scalar_prefetch=2,
        grid=(n_slots // tm, nc),
        in_specs=[rows_spec] * n_parts + [
            pl.BlockSpec((1, d, fc), lambda i, c, te, nv: (te[i], 0, chunk(i, c))),
            pl.BlockSpec((1, d, fc), lambda i, c, te, nv: (te[i], 0, chunk(i, c))),
            pl.BlockSpec((1, fc, d), lambda i, c, te, nv: (te[i], chunk(i, c), 0))],
        out_specs=[rows_spec] * n_parts,
        scratch_shapes=[pltpu.VMEM((tm, d), F32)],
    )
    return pl.pallas_call(
        functools.partial(_experts_kernel, n_parts=n_parts),
        grid_spec=grid_spec,
        out_shape=[jax.ShapeDtypeStruct((n_slots, words), jnp.int32)] * n_parts,
        compiler_params=_params(("parallel", "arbitrary"), est),
        name="moe_experts",
    )(tile_expert, n_valid, *xs_parts, wg, wu, wd)


def _combine_kernel(*refs, n_parts):
    y1_refs = refs[:n_parts]
    y2_refs = refs[n_parts:2 * n_parts]
    gw_ref, x_ref, mod_ref, g_ref, o_ref = refs[2 * n_parts:]
    gw = gw_ref[0]
    y = gw[:, 0:1] * _load_packed(y1_refs) + gw[:, 1:2] * _load_packed(y2_refs)
    o_ref[0] = x_ref[0] + mod_ref[0][5:6] * (_rms(y) * g_ref[...])


def _combine(ys2_parts, gw, x1, mod, mod_row0, g_post):
    n_parts = len(ys2_parts)
    nb, rows, d = x1.shape
    words = ys2_parts[0].shape[-1]
    tm = min(TOKEN_TILE, rows)
    tok = lambda n: pl.BlockSpec((1, tm, n), lambda b, i: (b, i, 0))
    ysp = lambda k: pl.BlockSpec((1, tm, words), lambda b, i: (k * nb + b, i, 0))
    est = 2 * tm * (d * 2 * 4 + d * 4 + V7X_LANES * 4) + 3 * tm * d * 4
    return pl.pallas_call(
        functools.partial(_combine_kernel, n_parts=n_parts),
        grid=(nb, rows // tm),
        in_specs=[ysp(0)] * n_parts + [ysp(1)] * n_parts + [
            tok(V7X_LANES), tok(d),
            pl.BlockSpec((1, 6, d), lambda b, i: (b + mod_row0, 0, 0)),
            pl.BlockSpec((1, d), lambda b, i: (0, 0))],
        out_specs=tok(d),
        out_shape=jax.ShapeDtypeStruct((nb, rows, d), F32),
        compiler_params=_params(("parallel", "parallel"), est),
        name="moe_combine",
    )(*ys2_parts, *ys2_parts, gw, x1, mod, g_post.reshape(1, d))


def _moe_routed(gw, eid, cnt, *h2_parts, x1, mod, mod_row0, wg, wu, wd, g_post):
    nb, rows, d = x1.shape
    t = nb * rows
    tm = min(TOKEN_TILE, rows)
    tile_base, tile_expert, n_valid, n_slots = _route_plan(cnt, t)
    pos = _slots(eid.reshape(t, V7X_LANES), tile_base, tm)
    idx = jnp.concatenate([pos[:, 0], pos[:, 1]])
    xs = [_scatter_rows(h.reshape(t, h.shape[-1]), idx, n_slots) for h in h2_parts]
    ys = _experts(xs, tile_expert, n_valid, wg.astype(BF16), wu.astype(BF16), wd.astype(BF16))
    ys2 = [_gather_rows(y, idx).reshape(2 * nb, rows, y.shape[-1]) for y in ys]
    return _combine(ys2, gw, x1, mod, mod_row0, g_post)


def _pack_w_in(w):
    a = 2 * GLA_KDIM + 2 * GLA_WIDTH
    lr = 2 * GLA_GATE_RANK
    pad = jnp.zeros((w.shape[0], LR_PAD - lr), w.dtype)
    return jnp.concatenate([w[:, :a], w[:, a + lr:], w[:, a:a + lr], pad], axis=1).astype(BF16)


def _pack_gate(w_gate, b_gate):
    wgp = jnp.zeros((2, LR_PAD, GLA_KDIM), F32)
    for d in range(2):
        wgp = wgp.at[d, d * GLA_GATE_RANK:(d + 1) * GLA_GATE_RANK].set(w_gate[d])
    return wgp, b_gate.reshape(2, 1, GLA_KDIM)


def kernel(x, c, ctx, c_ctx, w_ada, b_ada, g_pre_mix, g_post_mix, g_pre_ffn, g_post_ffn, w_in,
           gla_w_gate, gla_b_gate, gla_g_norm, na_rpb, w_out, ffn_w_gate, ffn_w_up, ffn_w_down,
           moe_w_router, moe_w_gate, moe_w_up, moe_w_down):
    B, L, D = x.shape
    Lc = ctx.shape[1]
    depth = w_ada.shape[0]
    rows = L // GRID_W
    ctx_row = B
    n_cond = -(-(B + 1) // 8) * 8
    cond = jnp.zeros((n_cond, D), F32).at[:B].set(c).at[B].set(c_ctx)
    mod = _modulation(cond, w_ada, b_ada)
    tables = _rope_tables(L)
    ctx = ctx.reshape(1, B * Lc, D)

    for i in range(depth):
        last = i == depth - 1
        j = i // 2
        w = _pack_w_in(w_in[i])
        wgp, bgp = _pack_gate(gla_w_gate[i], gla_b_gate[i])
        qk, gv, gr, nq, nk, nv, bf, bb = _inproj(x, mod[i], 0, g_pre_mix[i], w, wgp, bgp, tables)
        cparts = _inproj(ctx, mod[i], ctx_row, g_pre_mix[i], w, wgp, bgp)
        cqk, cgv, cgr, cnq, cnk, cnv, cbf, cbb = [t.reshape(B, Lc, t.shape[-1]) for t in cparts]
        ga, gac = _gla(qk, gv, gr, bf, bb, cqk, cgv, cgr, cbf, cbb, gla_g_norm[i])
        na = _na(nq, nk, nv, cnk, cnv, _na_bias_table(na_rpb[i], rows))
        if i % 2 == 0:
            x1, h2 = _outproj(ga, na, x, mod[i], 0, w_out[i], g_post_mix[i], g_pre_ffn[i])
            x = _ffn(h2, x1, mod[i], 0, ffn_w_gate[j], ffn_w_up[j], ffn_w_down[j], g_post_ffn[i])
        else:
            x1, *routed = _outproj(ga, na, x, mod[i], 0, w_out[i], g_post_mix[i], g_pre_ffn[i], moe_w_router[j])
            x = _moe_routed(*routed, x1=x1, mod=mod[i], mod_row0=0, wg=moe_w_gate[j], wu=moe_w_up[j],
                            wd=moe_w_down[j], g_post=g_post_ffn[i])
        if not last:
            nac = _ctx_attn(cnq, cnk, cnv)
            gac = gac.reshape(1, B * Lc, GLA_WIDTH)
            nac = nac.reshape(1, B * Lc, NA_WIDTH)
            if i % 2 == 0:
                c1, ch2 = _outproj(gac, nac, ctx, mod[i], ctx_row, w_out[i], g_post_mix[i], g_pre_ffn[i])
                ctx = _ffn(ch2, c1, mod[i], ctx_row, ffn_w_gate[j], ffn_w_up[j], ffn_w_down[j], g_post_ffn[i])
            else:
                c1, *routed = _outproj(gac, nac, ctx, mod[i], ctx_row, w_out[i], g_post_mix[i], g_pre_ffn[i],
                                       moe_w_router[j])
                ctx = _moe_routed(*routed, x1=c1, mod=mod[i], mod_row0=ctx_row, wg=moe_w_gate[j], wu=moe_w_up[j],
                                  wd=moe_w_down[j], g_post=g_post_ffn[i])
    return x
```

```python
import functools

import numpy as np
import jax
import jax.numpy as jnp
from jax import lax
from jax.experimental import pallas as pl
from jax.experimental.pallas import tpu as pltpu
from jax.experimental.pallas import tpu_sc as plsc

F32 = jnp.float32
BF16 = jnp.bfloat16

GRID_W = 64
GLA_HEADS = 4
GLA_DV = 128
GLA_DK = 64
GLA_KDIM = GLA_HEADS * GLA_DK
GLA_WIDTH = GLA_HEADS * GLA_DV
GLA_GATE_RANK = 16
GLA_GATE_NORM = 16.0
NA_HEADS = 8
NA_DH = 64
NA_WIDTH = NA_HEADS * NA_DH
NA_WIN_H = 8
NA_WIN_W = 16
ROPE_BASE = 10000.0
N_EXPERTS = 8
EPS = 1e-6

V7X_LANES = 128
V7X_VMEM_BYTES = 64 * 1024 * 1024
V7X_VMEM_USABLE = V7X_VMEM_BYTES - 8 * 1024 * 1024

TOKEN_TILE = 512
GLA_CHUNK = 128
NA_ROWS_PER_STEP = 8
NA_ROWS_PER_ITER = 2
MASK_VALUE = -1e30
MOE_FF_CHUNK = 1792
MOE_ROW_TILE = 512
SC_ROW_WINDOW = 128
CAST_BLOCK_BYTES = 4 << 20
SC_ROW_WORDS = 256
LR_PAD = V7X_LANES


def _vmem_limit(estimate_bytes):
    return int(min(V7X_VMEM_USABLE, estimate_bytes * 5 // 4 + (4 << 20)))


def _params(semantics, vmem_estimate):
    return pltpu.CompilerParams(dimension_semantics=semantics, vmem_limit_bytes=_vmem_limit(vmem_estimate))


def _dot(a, b):
    return jnp.dot(a, b, preferred_element_type=F32)


def _dot_nt(a, b):
    return lax.dot_general(a, b, (((1,), (1,)), ((), ())), preferred_element_type=F32)


def _dot_tn(a, b):
    return lax.dot_general(a, b, (((0,), (0,)), ((), ())), preferred_element_type=F32)


def _split(x):
    hi = x.astype(BF16)
    lo = (x - hi.astype(F32)).astype(BF16)
    return hi, lo


def _dot3(a, b):
    ah, al = _split(a)
    bh, bl = _split(b)
    return _dot(ah, bh) + (_dot(al, bh) + _dot(ah, bl))


def _sigmoid(x):
    return 1.0 / (1.0 + jnp.exp(-x))


def _silu(x):
    return x * _sigmoid(x)


def _rms(x):
    return x * lax.rsqrt(jnp.mean(x * x, axis=-1, keepdims=True) + EPS)


def _mod_kernel(c_ref, w_ref, b_ref, o_ref):
    o_ref[0] = _dot3(_silu(c_ref[...]), w_ref[0]) + b_ref[0]


def _modulation(cond, w_ada, b_ada):
    depth, d, n = w_ada.shape
    rows = cond.shape[0]
    tn = 512
    out = pl.pallas_call(
        _mod_kernel,
        grid=(depth, n // tn),
        in_specs=[
            pl.BlockSpec((rows, d), lambda l, j: (0, 0)),
            pl.BlockSpec((1, d, tn), lambda l, j: (l, 0, j)),
            pl.BlockSpec((1, 1, tn), lambda l, j: (l, 0, j)),
        ],
        out_specs=pl.BlockSpec((1, rows, tn), lambda l, j: (l, 0, j)),
        out_shape=jax.ShapeDtypeStruct((depth, rows, n), F32),
        compiler_params=_params(("parallel", "parallel"), 3 * d * tn * 4 * 2),
        name="modulation",
    )(cond, w_ada, b_ada.reshape(depth, 1, n))
    return out.reshape(depth, rows, 6, d)


def _log_decay(logit):
    return (jnp.minimum(logit, 0.0) - jnp.log(1.0 + jnp.exp(-jnp.abs(logit)))) * (1.0 / GLA_GATE_NORM)


def _inproj_kernel(*refs, rope):
    if rope:
        x_ref, mod_ref, g_ref, w_ref, wg_ref, bg_ref, cos_ref, sin_ref = refs[:8]
    else:
        x_ref, mod_ref, g_ref, w_ref, wg_ref, bg_ref = refs[:6]
    qk_ref, gv_ref, gr_ref, nq_ref, nk_ref, nv_ref, bf_ref, bb_ref = refs[-8:]
    m = mod_ref[0]
    h = (_rms(x_ref[0]) * g_ref[...] * (1.0 + m[1:2]) + m[0:1]).astype(BF16)
    tm = h.shape[0]
    plain_refs = (gv_ref, gr_ref, nq_ref, nk_ref, nv_ref)
    offs = np.cumsum([0, qk_ref.shape[-1]] + [r.shape[-1] for r in plain_refs])

    def project(k):
        ref = plain_refs[k]
        ref[0] = _dot(h, w_ref[:, offs[k + 1]:offs[k + 2]]).astype(ref.dtype)

    lr = _dot(h, w_ref[:, offs[-1]:offs[-1] + LR_PAD])
    logits = [_dot3(lr, wg_ref[d]) + bg_ref[d] for d in range(2)]
    project(0)
    C = GLA_CHUNK
    row = lax.broadcasted_iota(jnp.int32, (C, C), 0)
    colm = lax.broadcasted_iota(jnp.int32, (C, C), 1)
    for d, out_ref in enumerate((bf_ref, bb_ref)):
        gh, gl = _split(_log_decay(logits[d]))
        project(1 + 2 * d)
        tri = (row >= colm) if d == 0 else (row <= colm)
        tmat = jnp.where(tri, 1.0, 0.0).astype(BF16)
        for c in range(tm // C):
            sl = slice(c * C, (c + 1) * C)
            out_ref[0, sl, :] = _dot(tmat, gh[sl]) + _dot(tmat, gl[sl])
        project(2 + 2 * d)
    n = qk_ref.shape[-1]
    qk = _dot(h, w_ref[:, 0:n])
    lane = lax.broadcasted_iota(jnp.int32, (1, n), 1)
    if rope:
        reps = n // cos_ref.shape[-1]
        cos = jnp.concatenate([cos_ref[...]] * reps, axis=1)
        sin = jnp.concatenate([sin_ref[...]] * reps, axis=1)
        quarter = GLA_DK // 4
        first = (lane % (2 * quarter)) < quarter
        partner = jnp.where(first, pltpu.roll(qk, n - quarter, 1), pltpu.roll(qk, quarter, 1))
        qk = qk * cos + partner * sin
    qk_ref[0] = jnp.where(lane < GLA_KDIM, qk * (GLA_DK ** -0.5), qk).astype(qk_ref.dtype)


def _inproj(x, mod, mod_row0, g, w, wgp, bgp, rope_tables=None):
    nb, rows, d = x.shape
    tm = min(TOKEN_TILE, rows)
    rope = rope_tables is not None
    widths = (2 * GLA_KDIM, GLA_WIDTH, GLA_WIDTH, NA_WIDTH, NA_WIDTH, NA_WIDTH, GLA_KDIM, GLA_KDIM)
    dtypes = (BF16,) * 6 + (F32, F32)
    tok = lambda n: pl.BlockSpec((1, tm, n), lambda b, i: (b, i, 0))
    const = lambda shape: pl.BlockSpec(shape, lambda b, i: (0,) * len(shape))
    args = [x, mod, g.reshape(1, d), w, wgp, bgp]
    in_specs = [tok(d), pl.BlockSpec((1, 6, d), lambda b, i: (b + mod_row0, 0, 0)), const((1, d)),
                const(w.shape), const(wgp.shape), const(bgp.shape)]
    if rope:
        args += list(rope_tables)
        in_specs += [pl.BlockSpec((tm, t.shape[1]), lambda b, i: (i, 0)) for t in rope_tables]
    est = 2 * (tm * d * 4 + d * w.shape[1] * 2 + sum(widths) * tm * 4) + tm * d * 8
    return pl.pallas_call(
        functools.partial(_inproj_kernel, rope=rope),
        grid=(nb, rows // tm),
        in_specs=in_specs,
        out_specs=[tok(n) for n in widths],
        out_shape=[jax.ShapeDtypeStruct((nb, rows, n), dt) for n, dt in zip(widths, dtypes)],
        compiler_params=_params(("parallel", "parallel"), est),
        name="inproj",
    )(*args)


def _gla_kernel(q_ref, k_ref, v_ref, r_ref, bf_ref, bb_ref, cq_ref, ck_ref, cv_ref, cr_ref, cbf_ref, cbb_ref,
                gain_ref, o_ref, oc_ref, of_ref, ocf_ref, st_ref):
    C = GLA_CHUNK
    L = q_ref.shape[1]
    Lc = cq_ref.shape[1]
    row = lax.broadcasted_iota(jnp.int32, (C, C), 0)
    col = lax.broadcasted_iota(jnp.int32, (C, C), 1)
    lane = lax.broadcasted_iota(jnp.int32, (1, 2 * GLA_DK), 1)
    head0 = lane < GLA_DK
    st_row = lax.broadcasted_iota(jnp.int32, (2 * GLA_DV, 2 * GLA_DK), 0) // GLA_DV
    st_col = lax.broadcasted_iota(jnp.int32, (2 * GLA_DV, 2 * GLA_DK), 1) // GLA_DK
    blockdiag = st_row == st_col
    gain = gain_ref[...]

    def chunk_steps(refs, chunks, acc_ref):
        rq, rk, rv = refs[:3]
        dirs = (0, 1)
        sls = [pl.ds(pl.multiple_of(i * C, C), C) for i in chunks]
        tris = [row >= col, row <= col]
        vs = [rv[0, sl, :] for sl in sls]
        qfs = [rq[0, sl, :].astype(F32) for sl in sls]
        kfs = [rk[0, sl, :].astype(F32) for sl in sls]
        bs = [refs[3 + d][0, sls[d], :] for d in dirs]
        b_mids = [b[C // 2:C // 2 + 1] for b in bs]
        b_edges = [bs[0][C - 1:C], bs[1][0:1]]
        qes = [(qfs[d] * jnp.exp(bs[d] - b_mids[d])).astype(BF16) for d in dirs]
        kes = [(kfs[d] * jnp.exp(b_mids[d] - bs[d])).astype(BF16) for d in dirs]
        zero = jnp.zeros_like(qes[0])
        lhss = [jnp.concatenate([jnp.where(head0, qe, zero), jnp.where(head0, zero, qe)], axis=0) for qe in qes]
        scores = [_dot_nt(lhss[d], kes[d]) for d in dirs]
        qbs = [(qfs[d] * jnp.exp(bs[d])).astype(BF16) for d in dirs]
        kds = [(kfs[d] * jnp.exp(b_edges[d] - bs[d])).astype(BF16) for d in dirs]
        sts = [st_ref[d] for d in dirs]
        inter = [_dot_nt(qbs[d], sts[d].astype(BF16)) for d in dirs]
        dss = [_dot_tn(vs[d], kds[d]) for d in dirs]
        ams = [jnp.where(jnp.concatenate([tris[d], tris[d]], axis=0), scores[d], 0.0).astype(BF16) for d in dirs]
        intra = [jnp.concatenate([_dot(ams[d][:C], vs[d][:, :GLA_DV]), _dot(ams[d][C:], vs[d][:, GLA_DV:])], axis=1)
                 for d in dirs]
        for d in dirs:
            st_ref[d] = sts[d] * jnp.exp(b_edges[d]) + jnp.where(blockdiag, dss[d], 0.0)
            acc_ref[d, sls[d], :] = intra[d] + inter[d]

    def finish(total, r):
        outs = []
        for h in range(2):
            oh = _rms(total[:, h * GLA_DV:(h + 1) * GLA_DV]) * gain
            outs.append(oh * _silu(r[:, h * GLA_DV:(h + 1) * GLA_DV].astype(F32)))
        return jnp.concatenate(outs, axis=1)

    def scan(refs, n, acc_ref, r_ref_, out_ref):
        def body(j, carry):
            chunk_steps(refs, (j, n - 1 - j), acc_ref)
            return carry

        lax.fori_loop(0, n, body, 0)

        def fin(i, carry):
            sl = pl.ds(pl.multiple_of(i * C, C), C)
            out_ref[0, sl, :] = finish(acc_ref[0, sl, :] + acc_ref[1, sl, :], r_ref_[0, sl, :]).astype(out_ref.dtype)
            return carry

        lax.fori_loop(0, n, fin, 0)

    st_ref[...] = jnp.zeros_like(st_ref)
    scan((cq_ref, ck_ref, cv_ref, cbf_ref, cbb_ref), Lc // C, ocf_ref, cr_ref, oc_ref)
    scan((q_ref, k_ref, v_ref, bf_ref, bb_ref), L // C, of_ref, r_ref, o_ref)


def _gla(qk, gv, gr, bf, bb, cqk, cgv, cgr, cbf, cbb, gain):
    B, L, _ = qk.shape
    Lc = cqk.shape[1]
    pair = 2 * GLA_DK
    pv = 2 * GLA_DV
    n_pair = GLA_HEADS // 2
    seq = lambda rows, n, off: pl.BlockSpec((1, rows, n), lambda b, p: (b, 0, p + off))
    const = lambda shape: pl.BlockSpec(shape, lambda b, p: (0,) * len(shape))
    est = (2 * (2 * L * pair * 2 + 2 * L * pv * 2 + 2 * L * pair * 4 + L * pv * 2)
           + 2 * L * pv * 4 + (8 << 20))
    return pl.pallas_call(
        _gla_kernel,
        grid=(B, n_pair),
        in_specs=[
            seq(L, pair, 0), seq(L, pair, n_pair), seq(L, pv, 0), seq(L, pv, 0), seq(L, pair, 0), seq(L, pair, 0),
            seq(Lc, pair, 0), seq(Lc, pair, n_pair), seq(Lc, pv, 0), seq(Lc, pv, 0), seq(Lc, pair, 0),
            seq(Lc, pair, 0),
            const((1, GLA_DV)),
        ],
        out_specs=[seq(L, pv, 0), seq(Lc, pv, 0)],
        out_shape=[jax.ShapeDtypeStruct((B, L, GLA_WIDTH), BF16), jax.ShapeDtypeStruct((B, Lc, GLA_WIDTH), BF16)],
        scratch_shapes=[pltpu.VMEM((2, L, pv), F32), pltpu.VMEM((2, Lc, pv), F32), pltpu.VMEM((2, pv, pair), F32)],
        compiler_params=_params(("parallel", "parallel"), est),
        name="gla",
    )(qk, qk, gv, gr, bf, bb, cqk, cqk, cgv, cgr, cbf, cbb, gain.reshape(1, GLA_DV))


def _rope_tables(L):
    pos = np.arange(L)
    half = GLA_DK // 4
    inv = ROPE_BASE ** (-np.arange(half, dtype=np.float64) / half)
    lane = np.arange(2 * GLA_DK)
    jj = lane % (GLA_DK // 2)
    use_col = (lane % GLA_DK) >= GLA_DK // 2
    p = np.where(use_col[None, :], (pos % GRID_W)[:, None], (pos // GRID_W)[:, None]).astype(np.float64)
    ang = p * inv[jj % half][None, :]
    first = jj < half
    cos = np.cos(ang)
    sin = np.where(first[None, :], -np.sin(ang), np.sin(ang))
    return jnp.asarray(cos, F32), jnp.asarray(sin, F32)


def _softmax_pv(s_parts, v_parts):
    m = s_parts[0].max(axis=-1, keepdims=True)
    for s in s_parts[1:]:
        m = jnp.maximum(m, s.max(axis=-1, keepdims=True))
    den = None
    acc = None
    for s, v in zip(s_parts, v_parts):
        p = jnp.exp(s - m)
        den = p.sum(axis=-1, keepdims=True) if den is None else den + p.sum(axis=-1, keepdims=True)
        pv = _dot(p.astype(BF16), v)
        acc = pv if acc is None else acc + pv
    return acc / den


def _na_window_start():
    cols = np.arange(GRID_W)
    return np.clip(cols - NA_WIN_W // 2, 0, GRID_W - NA_WIN_W)


def _na_kernel(q_ref, k_ref, v_ref, ck_ref, cv_ref, bias_ref, o_ref):
    W = GRID_W
    rows = k_ref.shape[1] // W
    n_loc = bias_ref.shape[2]
    kh = n_loc // W
    r0 = pl.program_id(1) * NA_ROWS_PER_STEP
    pair = 2 * NA_DH
    lane = lax.broadcasted_iota(jnp.int32, (1, pair), 1)
    head0 = lane < NA_DH
    scale = jnp.asarray(NA_DH ** -0.5, q_ref.dtype)

    n_pairs = NA_HEADS // 2
    lanes = [slice(p * pair, (p + 1) * pair) for p in range(n_pairs)]

    def rows_body(it, carry):
        units = []
        for j in range(NA_ROWS_PER_ITER):
            rr = it * NA_ROWS_PER_ITER + j
            r = r0 + rr
            rs = jnp.clip(r - kh // 2, 0, rows - kh)
            qs = pl.ds(pl.multiple_of(rr * W, W), W)
            ks = pl.ds(pl.multiple_of(rs * W, W), n_loc)
            units += [(qs, ks, rs - r + NA_WIN_H - 1, p) for p in range(n_pairs)]
        sts = []
        for qs, ks, dr, p in units:
            q = q_ref[0, qs, lanes[p]] * scale
            zero = jnp.zeros_like(q)
            q2 = jnp.concatenate([jnp.where(head0, q, zero), jnp.where(head0, zero, q)], axis=0)
            keys = jnp.concatenate([k_ref[0, ks, lanes[p]], ck_ref[0, :, lanes[p]]], axis=0)
            sts.append(_dot_nt(keys, q2))
        pts = []
        for (qs, ks, dr, p), st in zip(units, sts):
            st = jnp.concatenate([st[:n_loc] + bias_ref[p, dr], st[n_loc:]], axis=0)
            e = jnp.exp(st - st.max(axis=0, keepdims=True))
            pts.append((e * (1.0 / e.sum(axis=0, keepdims=True))).astype(BF16))
        outs = []
        for (qs, ks, dr, p), pt in zip(units, pts):
            vals = jnp.concatenate([v_ref[0, ks, lanes[p]], cv_ref[0, :, lanes[p]]], axis=0)
            o2 = _dot_tn(pt, vals)
            outs.append(jnp.where(head0, o2[:W], o2[W:]))
        for j in range(NA_ROWS_PER_ITER):
            qs = units[j * n_pairs][0]
            o_ref[0, qs, :] = jnp.concatenate(outs[j * n_pairs:(j + 1) * n_pairs], axis=1).astype(o_ref.dtype)
        return carry

    lax.fori_loop(0, NA_ROWS_PER_STEP // NA_ROWS_PER_ITER, rows_body, 0)


def _na(nq, nk, nv, cnk, cnv, bias):
    B, L, n = nq.shape
    Lc = cnk.shape[1]
    tq = NA_ROWS_PER_STEP * GRID_W
    full = lambda rows: pl.BlockSpec((1, rows, n), lambda b, i: (b, 0, 0))
    est = 2 * (2 * L * n * 2 + 2 * Lc * n * 2 + bias.size * 4 + 2 * tq * n * 2) + (8 << 20)
    return pl.pallas_call(
        _na_kernel,
        grid=(B, L // tq),
        in_specs=[
            pl.BlockSpec((1, tq, n), lambda b, i: (b, i, 0)),
            full(L), full(L), full(Lc), full(Lc),
            pl.BlockSpec(bias.shape, lambda b, i: (0, 0, 0, 0)),
        ],
        out_specs=pl.BlockSpec((1, tq, n), lambda b, i: (b, i, 0)),
        out_shape=jax.ShapeDtypeStruct((B, L, n), BF16),
        compiler_params=_params(("parallel", "parallel"), est),
        name="neighbourhood_attention",
    )(nq, nk, nv, cnk, cnv, bias)


def _na_bias_table(rpb, rows):
    kh = min(NA_WIN_H, rows)
    start = _na_window_start()
    kc = np.arange(GRID_W)
    inside = (kc[None, :] >= start[:, None]) & (kc[None, :] < start[:, None] + NA_WIN_W)
    sel = np.zeros((2 * NA_WIN_W - 1, GRID_W, GRID_W), np.float32)
    qq, kk = np.nonzero(inside)
    sel[kk - qq + NA_WIN_W - 1, qq, kk] = 1.0
    by_row = jnp.stack([rpb[:, d:d + kh, :] for d in range(NA_WIN_H)], axis=1)
    t = jnp.einsum('hdic,cqk->hdikq', by_row, jnp.asarray(sel), precision=lax.Precision.HIGHEST)
    t = t + jnp.asarray(np.where(inside, 0.0, MASK_VALUE).T, F32)[None, None, None, :, :]
    t = t.reshape(NA_HEADS // 2, 2, NA_WIN_H, kh * GRID_W, GRID_W)
    return jnp.transpose(t, (0, 2, 3, 1, 4)).reshape(NA_HEADS // 2, NA_WIN_H, kh * GRID_W, 2 * GRID_W)


def _ctx_attn_kernel(q_ref, k_ref, v_ref, o_ref):
    pair = 2 * NA_DH
    lane = lax.broadcasted_iota(jnp.int32, (1, pair), 1)
    head0 = lane < NA_DH
    Lc = q_ref.shape[1]
    outs = []
    for p in range(NA_HEADS // 2):
        ls = slice(p * pair, (p + 1) * pair)
        q = q_ref[0, :, ls]
        zero = jnp.zeros_like(q)
        q2 = jnp.concatenate([jnp.where(head0, q, zero), jnp.where(head0, zero, q)], axis=0)
        s = _dot_nt(q2, k_ref[0, :, ls]) * (NA_DH ** -0.5)
        o2 = _softmax_pv([s], [v_ref[0, :, ls]])
        outs.append(jnp.where(head0, o2[:Lc], o2[Lc:]))
    o_ref[0] = jnp.concatenate(outs, axis=1).astype(o_ref.dtype)


def _ctx_attn(cnq, cnk, cnv):
    B, Lc, n = cnq.shape
    spec = pl.BlockSpec((1, Lc, n), lambda b: (b, 0, 0))
    return pl.pallas_call(
        _ctx_attn_kernel,
        grid=(B,),
        in_specs=[spec, spec, spec],
        out_specs=spec,
        out_shape=jax.ShapeDtypeStruct((B, Lc, n), BF16),
        compiler_params=_params(("parallel",), 8 * Lc * n * 2 + (8 << 20)),
        name="context_attention",
    )(cnq, cnk, cnv)


def _pack_bf16_pairs(x):
    n = x.shape[1] // 2
    lo = lax.bitcast_convert_type(x[:, :n].astype(BF16).astype(F32), jnp.int32)
    hi = lax.bitcast_convert_type(x[:, n:].astype(BF16).astype(F32), jnp.int32)
    return lax.shift_right_logical(lo, 16) | (hi & jnp.int32(-65536))


def _unpack_bf16_pairs(p):
    lo = lax.bitcast_convert_type(lax.shift_left(p, 16), F32)
    hi = lax.bitcast_convert_type(p & jnp.int32(-65536), F32)
    return jnp.concatenate([lo, hi], axis=1)


def _store_packed(refs, x):
    n = 2 * SC_ROW_WORDS
    for p, ref in enumerate(refs):
        ref[...] = _pack_bf16_pairs(x[:, p * n:(p + 1) * n]).reshape(ref.shape)


def _load_packed(refs):
    return jnp.concatenate([_unpack_bf16_pairs(ref[...].reshape(ref.shape[-2:])) for ref in refs], axis=1)


def _route(logits):
    lane = lax.broadcasted_iota(jnp.int32, logits.shape, 1)
    big = jnp.int32(logits.shape[1])
    t1 = logits.max(axis=-1, keepdims=True)
    i1 = jnp.where(logits == t1, lane, big).min(axis=-1, keepdims=True)
    rest = jnp.where(lane == i1, -jnp.inf, logits)
    t2 = rest.max(axis=-1, keepdims=True)
    i2 = jnp.where(rest == t2, lane, big).min(axis=-1, keepdims=True)
    e2 = jnp.exp(t2 - t1)
    return i1, i2, 1.0 / (1.0 + e2), e2 / (1.0 + e2)


def _outproj_kernel(*refs, dense_ffn):
    ga_ref, na_ref, x_ref, mod_ref, wa_ref, wb_ref, gp_ref, gf_ref = refs[:8]
    m = mod_ref[0]
    y = _dot(ga_ref[0], wa_ref[...]) + _dot(na_ref[0], wb_ref[...])
    x1 = x_ref[0] + m[2:3] * (_rms(y) * gp_ref[...])
    h2 = _rms(x1) * gf_ref[...] * (1.0 + m[4:5]) + m[3:4]
    if dense_ffn:
        wg_ref, wu_ref, wd_ref, g2_ref, o_ref = refs[8:]
        h = h2.astype(BF16)
        hid = (_silu(_dot(h, wg_ref[...])) * _dot(h, wu_ref[...])).astype(BF16)
        o_ref[0] = x1 + m[5:6] * (_rms(_dot(hid, wd_ref[...])) * g2_ref[...])
    else:
        wr_ref, x1_ref, gw_ref, eid_ref, cnt_ref, *h2_refs = refs[8:]
        x1_ref[0] = x1
        _store_packed(h2_refs, h2)
        tm = h2.shape[0]
        lanes = wr_ref.shape[1] // 2
        prod = _dot(jnp.concatenate(_split(h2), axis=0), wr_ref[...])
        logits = (prod[:tm, :lanes] + prod[tm:, :lanes]) + (prod[:tm, lanes:] + prod[tm:, lanes:])
        lane = lax.broadcasted_iota(jnp.int32, (tm, lanes), 1)
        logits = jnp.where(lane < N_EXPERTS, logits, -jnp.inf)
        i1, i2, w1, w2 = _route(logits)
        gw_ref[0] = jnp.where(lane == 0, w1, jnp.where(lane == 1, w2, 0.0))
        eid_ref[0] = jnp.where(lane == 0, i1, jnp.where(lane == 1, i2, 0))
        chosen = jnp.where(lane == i1, 1.0, jnp.where(lane == i2, 1.0, 0.0))
        cnt_ref[0] = jnp.broadcast_to(chosen.sum(axis=0, keepdims=True), cnt_ref.shape[1:])


def _outproj(ga, na, x, mod, mod_row0, w_out, g_post, g_ffn, *, ffn=None, w_router=None):
    nb, rows, d = x.shape
    tm = min(TOKEN_TILE, rows)
    dense_ffn = ffn is not None
    tok = lambda n: pl.BlockSpec((1, tm, n), lambda b, i: (b, i, 0))
    const = lambda shape: pl.BlockSpec(shape, lambda b, i: (0,) * len(shape))
    once = lambda shape: pl.BlockSpec(shape, lambda b, i: (0,) * len(shape), pipeline_mode=pl.Buffered(1))
    wa = w_out[:GLA_WIDTH].astype(BF16)
    wb = w_out[GLA_WIDTH:].astype(BF16)
    args = [ga, na, x, mod, wa, wb, g_post.reshape(1, d), g_ffn.reshape(1, d)]
    in_specs = [tok(GLA_WIDTH), tok(NA_WIDTH), tok(d),
                pl.BlockSpec((1, 6, d), lambda b, i: (b + mod_row0, 0, 0)),
                const(wa.shape), const(wb.shape), const((1, d)), const((1, d))]
    est = 2 * (tm * d * (4 + 4 + 2) + 2 * tm * GLA_WIDTH * 2 + d * d * 2) + 4 * tm * d * 4
    if dense_ffn:
        wg, wu, wd, g2 = ffn
        dff = wg.shape[1]
        args += [wg, wu, wd, g2.reshape(1, d)]
        in_specs += [once(wg.shape), once(wu.shape), once(wd.shape), const((1, d))]
        out_specs = tok(d)
        out_shape = jax.ShapeDtypeStruct((nb, rows, d), F32)
        est += 3 * d * dff * 2 + 3 * tm * dff * 4
    else:
        nt = rows // tm
        wr = jnp.zeros((d, V7X_LANES), F32).at[:, :N_EXPERTS].set(w_router)
        wr = jnp.concatenate(_split(wr), axis=1)
        args.append(wr)
        in_specs.append(const(wr.shape))
        n_parts = d // (2 * SC_ROW_WORDS)
        out_specs = [tok(d), tok(V7X_LANES), tok(V7X_LANES),
                     pl.BlockSpec((1, 8, V7X_LANES), lambda b, i: (b * nt + i, 0, 0))] + [tok(SC_ROW_WORDS)] * n_parts
        out_shape = [jax.ShapeDtypeStruct((nb, rows, d), F32),
                     jax.ShapeDtypeStruct((nb, rows, V7X_LANES), F32),
                     jax.ShapeDtypeStruct((nb, rows, V7X_LANES), jnp.int32),
                     jax.ShapeDtypeStruct((nb * nt, 8, V7X_LANES), F32)]
        out_shape += [jax.ShapeDtypeStruct((nb, rows, SC_ROW_WORDS), jnp.int32)] * n_parts
    return pl.pallas_call(
        functools.partial(_outproj_kernel, dense_ffn=dense_ffn),
        grid=(nb, rows // tm),
        in_specs=in_specs,
        out_specs=out_specs,
        out_shape=out_shape,
        compiler_params=_params(("parallel", "parallel"), est),
        name="outproj_ffn" if dense_ffn else "outproj_router",
    )(*args)


def _slot_kernel(eid_ref, base_ref, pos_ref):
    eid = eid_ref[...]
    tm, lanes = eid.shape
    lane = lax.broadcasted_iota(jnp.int32, (tm, lanes), 1)
    i1 = eid[:, 0:1]
    i2 = eid[:, 1:2]
    chosen = jnp.where(lane == i1, 1.0, jnp.where(lane == i2, 1.0, 0.0)).astype(BF16)
    row = lax.broadcasted_iota(jnp.int32, (tm, tm), 0)
    col = lax.broadcasted_iota(jnp.int32, (tm, tm), 1)
    incl = jnp.where(row >= col, 1.0, 0.0).astype(BF16)
    slot = base_ref[0][0:1] + _dot(incl, chosen) - 1.0
    p1 = jnp.where(lane == i1, slot, 0.0).sum(axis=-1, keepdims=True).astype(jnp.int32)
    p2 = jnp.where(lane == i2, slot, 0.0).sum(axis=-1, keepdims=True).astype(jnp.int32)
    pos_ref[...] = jnp.where(lane == 0, p1, jnp.where(lane == 1, p2, 0))


def _slots(eid, tile_base, tm):
    t, lanes = eid.shape
    return pl.pallas_call(
        _slot_kernel,
        grid=(t // tm,),
        in_specs=[pl.BlockSpec((tm, lanes), lambda i: (i, 0)),
                  pl.BlockSpec((1, 8, lanes), lambda i: (i, 0, 0))],
        out_specs=pl.BlockSpec((tm, lanes), lambda i: (i, 0)),
        out_shape=jax.ShapeDtypeStruct((t, lanes), jnp.int32),
        compiler_params=_params(("parallel",), 8 * tm * lanes * 4 + 4 * tm * tm),
        name="moe_slots",
    )(eid, tile_base)


def _route_plan(cnt, tm_tokens):
    counts = cnt[:, 0, :N_EXPERTS].astype(jnp.int32)
    total = counts.sum(axis=0)
    padded = -(-total // MOE_ROW_TILE) * MOE_ROW_TILE
    start = jnp.cumsum(padded) - padded
    before = jnp.cumsum(counts, axis=0) - counts
    tile_base = (start[None, :] + before).astype(F32)
    tile_base = jnp.zeros((cnt.shape[0], 8, V7X_LANES), F32).at[:, :, :N_EXPERTS].set(tile_base[:, None, :])
    n_slots = tm_tokens * 2 + N_EXPERTS * MOE_ROW_TILE
    first_row = jnp.arange(n_slots // MOE_ROW_TILE, dtype=jnp.int32) * MOE_ROW_TILE
    tile_expert = jnp.minimum((first_row[:, None] >= (start + padded)[None, :]).sum(axis=1), N_EXPERTS - 1)
    n_valid = jnp.clip(start[tile_expert] + total[tile_expert] - first_row, 0, MOE_ROW_TILE)
    return tile_base, tile_expert.astype(jnp.int32), n_valid.astype(jnp.int32), n_slots


def _sc_mesh():
    return plsc.VectorSubcoreMesh(core_axis_name="core", subcore_axis_name="subcore")


def _scatter_rows(x, idx, n_out):
    t, w = x.shape
    n = idx.shape[0]
    win = SC_ROW_WINDOW
    n_blk = t // win

    @functools.partial(pl.kernel, out_type=jax.ShapeDtypeStruct((n_out, w), x.dtype), mesh=_sc_mesh(),
                       scratch_types=[], name="moe_dispatch")
    def scatter(x_hbm, i_hbm, o_hbm):
        def body(x_vmem, i_vmem):
            pltpu.sync_copy(x_vmem, o_hbm.at[i_vmem.at[0]])

        pltpu.emit_pipeline(
            body,
            grid=(n // win,),
            in_specs=[pl.BlockSpec((win, w), lambda i: (i % n_blk, 0)),
                      pl.BlockSpec((1, win), lambda i: (0, i))],
            out_specs=[],
            core_axis_name=("core", "subcore"),
            dimension_semantics=(pltpu.PARALLEL,),
        )(x_hbm, i_hbm)

    return scatter(x, idx.reshape(1, n))


def _gather_rows(x, idx):
    n = idx.shape[0]
    w = x.shape[1]
    win = SC_ROW_WINDOW

    @functools.partial(pl.kernel, out_type=jax.ShapeDtypeStruct((n, w), x.dtype), mesh=_sc_mesh(),
                       scratch_types=[], name="moe_combine_gather")
    def gather(x_hbm, i_hbm, o_hbm):
        def body(i_vmem, o_vmem):
            pltpu.sync_copy(x_hbm.at[i_vmem.at[0]], o_vmem)

        pltpu.emit_pipeline(
            body,
            grid=(n // win,),
            in_specs=[pl.BlockSpec((1, win), lambda i: (0, i))],
            out_specs=[pl.BlockSpec((win, w), lambda i: (i, 0))],
            core_axis_name=("core", "subcore"),
            dimension_semantics=(pltpu.PARALLEL,),
        )(i_hbm, o_hbm)

    return gather(x, idx.reshape(1, n))


def _experts_kernel(te_ref, nv_ref, *refs, n_parts):
    x_refs = refs[:n_parts]
    wg_ref, wu_ref, wd_ref = refs[n_parts:n_parts + 3]
    y_refs = refs[n_parts + 3:2 * n_parts + 3]
    acc_ref = refs[-1]
    i = pl.program_id(0)
    c = pl.program_id(1)
    last = c == pl.num_programs(1) - 1
    n_valid = nv_ref[i]

    @pl.when(n_valid > 0)
    def _():
        row = lax.broadcasted_iota(jnp.int32, (acc_ref.shape[0], 1), 0)
        x = jnp.where(row < n_valid, _load_packed(x_refs), 0.0).astype(BF16)
        hid = (_silu(_dot(x, wg_ref[0])) * _dot(x, wu_ref[0])).astype(BF16)
        part = _dot(hid, wd_ref[0])

        @pl.when(c == 0)
        def _():
            acc_ref[...] = part

        @pl.when(c > 0)
        def _():
            acc_ref[...] += part

        @pl.when(last)
        def _():
            _store_packed(y_refs, acc_ref[...])

    @pl.when(jnp.logical_and(n_valid == 0, last))
    def _():
        for ref in y_refs:
            ref[...] = jnp.zeros_like(ref)


def _experts(xs_parts, tile_expert, n_valid, wg, wu, wd):
    n_parts = len(xs_parts)
    n_slots, words = xs_parts[0].shape
    n_e, d, dff = wg.shape
    tm = MOE_ROW_TILE
    fc = MOE_FF_CHUNK
    nc = dff // fc
    chunk = lambda i, c: jnp.where(i % 2 == 0, c, nc - 1 - c)
    rows_spec = pl.BlockSpec((tm, words), lambda i, c, te, nv: (i, 0))
    est = 2 * (3 * d * fc * 2 + 2 * tm * d * 2) + tm * d * 4 + 3 * tm * fc * 4 + tm * d * 4
    grid_spec = pltpu.PrefetchScalarGridSpec(
        num_scalar_prefetch=2,
        grid=(n_slots // tm, nc),
        in_specs=[rows_spec] * n_parts + [
            pl.BlockSpec((1, d, fc), lambda i, c, te, nv: (te[i], 0, chunk(i, c))),
            pl.BlockSpec((1, d, fc), lambda i, c, te, nv: (te[i], 0, chunk(i, c))),
            pl.BlockSpec((1, fc, d), lambda i, c, te, nv: (te[i], chunk(i, c), 0))],
        out_specs=[rows_spec] * n_parts,
        scratch_shapes=[pltpu.VMEM((tm, d), F32)],
    )
    return pl.pallas_call(
        functools.partial(_experts_kernel, n_parts=n_parts),
        grid_spec=grid_spec,
        out_shape=[jax.ShapeDtypeStruct((n_slots, words), jnp.int32)] * n_parts,
        compiler_params=_params(("parallel", "arbitrary"), est),
        name="moe_experts",
    )(tile_expert, n_valid, *xs_parts, wg, wu, wd)


def _combine_kernel(*refs, n_parts):
    y1_refs = refs[:n_parts]
    y2_refs = refs[n_parts:2 * n_parts]
    gw_ref, x_ref, mod_ref, g_ref, o_ref = refs[2 * n_parts:]
    gw = gw_ref[0]
    y = gw[:, 0:1] * _load_packed(y1_refs) + gw[:, 1:2] * _load_packed(y2_refs)
    o_ref[0] = x_ref[0] + mod_ref[0][5:6] * (_rms(y) * g_ref[...])


def _combine(ys2_parts, gw, x1, mod, mod_row0, g_post):
    n_parts = len(ys2_parts)
    nb, rows, d = x1.shape
    words = ys2_parts[0].shape[-1]
    tm = min(TOKEN_TILE, rows)
    tok = lambda n: pl.BlockSpec((1, tm, n), lambda b, i: (b, i, 0))
    ysp = lambda k: pl.BlockSpec((1, tm, words), lambda b, i: (k * nb + b, i, 0))
    est = 2 * tm * (d * 2 * 4 + d * 4 + V7X_LANES * 4) + 3 * tm * d * 4
    return pl.pallas_call(
        functools.partial(_combine_kernel, n_parts=n_parts),
        grid=(nb, rows // tm),
        in_specs=[ysp(0)] * n_parts + [ysp(1)] * n_parts + [
            tok(V7X_LANES), tok(d),
            pl.BlockSpec((1, 6, d), lambda b, i: (b + mod_row0, 0, 0)),
            pl.BlockSpec((1, d), lambda b, i: (0, 0))],
        out_specs=tok(d),
        out_shape=jax.ShapeDtypeStruct((nb, rows, d), F32),
        compiler_params=_params(("parallel", "parallel"), est),
        name="moe_combine",
    )(*ys2_parts, *ys2_parts, gw, x1, mod, g_post.reshape(1, d))


def _moe_routed(gw, eid, cnt, *h2_parts, x1, mod, mod_row0, experts, g_post):
    nb, rows, d = x1.shape
    t = nb * rows
    tm = min(TOKEN_TILE, rows)
    tile_base, tile_expert, n_valid, n_slots = _route_plan(cnt, t)
    pos = _slots(eid.reshape(t, V7X_LANES), tile_base, tm)
    idx = jnp.concatenate([pos[:, 0], pos[:, 1]])
    xs = [_scatter_rows(h.reshape(t, h.shape[-1]), idx, n_slots) for h in h2_parts]
    ys = _experts(xs, tile_expert, n_valid, *experts)
    ys2 = [_gather_rows(y, idx).reshape(2 * nb, rows, y.shape[-1]) for y in ys]
    return _combine(ys2, gw, x1, mod, mod_row0, g_post)


def _cast_kernel(w_ref, o_ref):
    o_ref[...] = w_ref[...].astype(o_ref.dtype)


def _to_bf16(w):
    shape = w.shape
    w3 = w.reshape((-1,) + shape[-2:])
    n, r, c = w3.shape
    rb = r
    while rb * c * 4 > CAST_BLOCK_BYTES and rb % 16 == 0:
        rb //= 2
    spec = pl.BlockSpec((1, rb, c), lambda e, i: (e, i, 0))
    out = pl.pallas_call(
        _cast_kernel,
        grid=(n, r // rb),
        in_specs=[spec],
        out_specs=spec,
        out_shape=jax.ShapeDtypeStruct(w3.shape, BF16),
        compiler_params=_params(("parallel", "parallel"), 2 * rb * c * 6),
        name="cast_bf16",
    )(w3)
    return out.reshape(shape)
def _pack_w_in(w):
    a = 2 * GLA_KDIM + 2 * GLA_WIDTH
    lr = 2 * GLA_GATE_RANK
    pad = jnp.zeros((w.shape[0], LR_PAD - lr), w.dtype)
    return jnp.concatenate([w[:, :a], w[:, a + lr:], w[:, a:a + lr], pad], axis=1).astype(BF16)


def _pack_gate(w_gate, b_gate):
    wgp = jnp.zeros((2, LR_PAD, GLA_KDIM), F32)
    for d in range(2):
        wgp = wgp.at[d, d * GLA_GATE_RANK:(d + 1) * GLA_GATE_RANK].set(w_gate[d])
    return wgp, b_gate.reshape(2, 1, GLA_KDIM)


def kernel(x, c, ctx, c_ctx, w_ada, b_ada, g_pre_mix, g_post_mix, g_pre_ffn, g_post_ffn, w_in,
           gla_w_gate, gla_b_gate, gla_g_norm, na_rpb, w_out, ffn_w_gate, ffn_w_up, ffn_w_down,
           moe_w_router, moe_w_gate, moe_w_up, moe_w_down):
    B, L, D = x.shape
    Lc = ctx.shape[1]
    depth = w_ada.shape[0]
    rows = L // GRID_W
    ctx_row = B
    n_cond = -(-(B + 1) // 8) * 8
    cond = jnp.zeros((n_cond, D), F32).at[:B].set(c).at[B].set(c_ctx)
    mod = _modulation(cond, w_ada, b_ada)
    tables = _rope_tables(L)
    ctx = ctx.reshape(1, B * Lc, D)

    for i in range(depth):
        last = i == depth - 1
        j = i // 2
        w = _pack_w_in(w_in[i])
        wgp, bgp = _pack_gate(gla_w_gate[i], gla_b_gate[i])
        qk, gv, gr, nq, nk, nv, bf, bb = _inproj(x, mod[i], 0, g_pre_mix[i], w, wgp, bgp, tables)
        cparts = _inproj(ctx, mod[i], ctx_row, g_pre_mix[i], w, wgp, bgp)
        cqk, cgv, cgr, cnq, cnk, cnv, cbf, cbb = [t.reshape(B, Lc, t.shape[-1]) for t in cparts]
        ga, gac = _gla(qk, gv, gr, bf, bb, cqk, cgv, cgr, cbf, cbb, gla_g_norm[i])
        na = _na(nq, nk, nv, cnk, cnv, _na_bias_table(na_rpb[i], rows))
        if i % 2 == 0:
            ffn = (_to_bf16(ffn_w_gate[j]), _to_bf16(ffn_w_up[j]), _to_bf16(ffn_w_down[j]), g_post_ffn[i])

            def mix_and_ffn(ga_, na_, x_, row0):
                return _outproj(ga_, na_, x_, mod[i], row0, w_out[i], g_post_mix[i], g_pre_ffn[i], ffn=ffn)
        else:
            experts = (_to_bf16(moe_w_gate[j]), _to_bf16(moe_w_up[j]), _to_bf16(moe_w_down[j]))

            def mix_and_ffn(ga_, na_, x_, row0):
                x1, *routed = _outproj(ga_, na_, x_, mod[i], row0, w_out[i], g_post_mix[i], g_pre_ffn[i],
                                       w_router=moe_w_router[j])
                return _moe_routed(*routed, x1=x1, mod=mod[i], mod_row0=row0, experts=experts, g_post=g_post_ffn[i])
        x = mix_and_ffn(ga, na, x, 0)
        if not last:
            nac = _ctx_attn(cnq, cnk, cnv)
            ctx = mix_and_ffn(gac.reshape(1, B * Lc, GLA_WIDTH), nac.reshape(1, B * Lc, NA_WIDTH), ctx, ctx_row)
    return x
```

```python
import functools

import numpy as np
import jax
import jax.numpy as jnp
from jax import lax
from jax.experimental import pallas as pl
from jax.experimental.pallas import tpu as pltpu
from jax.experimental.pallas import tpu_sc as plsc

F32 = jnp.float32
BF16 = jnp.bfloat16

GRID_W = 64
GLA_HEADS = 4
GLA_DV = 128
GLA_DK = 64
GLA_KDIM = GLA_HEADS * GLA_DK
GLA_WIDTH = GLA_HEADS * GLA_DV
GLA_GATE_RANK = 16
GLA_GATE_NORM = 16.0
NA_HEADS = 8
NA_DH = 64
NA_WIDTH = NA_HEADS * NA_DH
NA_WIN_H = 8
NA_WIN_W = 16
ROPE_BASE = 10000.0
N_EXPERTS = 8
EPS = 1e-6

V7X_LANES = 128
V7X_VMEM_BYTES = 64 * 1024 * 1024
V7X_VMEM_USABLE = V7X_VMEM_BYTES - 8 * 1024 * 1024

TOKEN_TILE = 512
GLA_CHUNK = 128
NA_ROWS_PER_STEP = 8
NA_ROWS_PER_ITER = 2
MASK_VALUE = -1e30
MOE_FF_CHUNK = 1792
MOE_ROW_TILE = 512
SC_ROW_WINDOW = 128
CAST_BLOCK_BYTES = 4 << 20
SC_ROW_WORDS = 256
LR_PAD = V7X_LANES


def _vmem_limit(estimate_bytes):
    return int(min(V7X_VMEM_USABLE, estimate_bytes * 5 // 4 + (4 << 20)))


def _params(semantics, vmem_estimate):
    return pltpu.CompilerParams(dimension_semantics=semantics, vmem_limit_bytes=_vmem_limit(vmem_estimate))


def _dot(a, b):
    return jnp.dot(a, b, preferred_element_type=F32)


def _dot_nt(a, b):
    return lax.dot_general(a, b, (((1,), (1,)), ((), ())), preferred_element_type=F32)


def _dot_tn(a, b):
    return lax.dot_general(a, b, (((0,), (0,)), ((), ())), preferred_element_type=F32)


def _split(x):
    hi = x.astype(BF16)
    lo = (x - hi.astype(F32)).astype(BF16)
    return hi, lo


def _dot3(a, b):
    ah, al = _split(a)
    bh, bl = _split(b)
    return _dot(ah, bh) + (_dot(al, bh) + _dot(ah, bl))


def _sigmoid(x):
    return 1.0 / (1.0 + jnp.exp(-x))


def _silu(x):
    return x * _sigmoid(x)


def _rms(x):
    return x * lax.rsqrt(jnp.mean(x * x, axis=-1, keepdims=True) + EPS)


def _mod_kernel(c_ref, w_ref, b_ref, o_ref):
    o_ref[0] = _dot3(_silu(c_ref[...]), w_ref[0]) + b_ref[0]


def _modulation(cond, w_ada, b_ada):
    depth, d, n = w_ada.shape
    rows = cond.shape[0]
    tn = 512
    out = pl.pallas_call(
        _mod_kernel,
        grid=(depth, n // tn),
        in_specs=[
            pl.BlockSpec((rows, d), lambda l, j: (0, 0)),
            pl.BlockSpec((1, d, tn), lambda l, j: (l, 0, j)),
            pl.BlockSpec((1, 1, tn), lambda l, j: (l, 0, j)),
        ],
        out_specs=pl.BlockSpec((1, rows, tn), lambda l, j: (l, 0, j)),
        out_shape=jax.ShapeDtypeStruct((depth, rows, n), F32),
        compiler_params=_params(("parallel", "parallel"), 3 * d * tn * 4 * 2),
        name="modulation",
    )(cond, w_ada, b_ada.reshape(depth, 1, n))
    return out.reshape(depth, rows, 6, d)


def _log_decay(logit):
    return (jnp.minimum(logit, 0.0) - jnp.log(1.0 + jnp.exp(-jnp.abs(logit)))) * (1.0 / GLA_GATE_NORM)


def _inproj_kernel(*refs, rope):
    if rope:
        x_ref, mod_ref, g_ref, w_ref, wg_ref, bg_ref, cos_ref, sin_ref = refs[:8]
    else:
        x_ref, mod_ref, g_ref, w_ref, wg_ref, bg_ref = refs[:6]
    qk_ref, gv_ref, gr_ref, nq_ref, nk_ref, nv_ref, bf_ref, bb_ref = refs[-8:]
    m = mod_ref[0]
    h = (_rms(x_ref[0]) * g_ref[...] * (1.0 + m[1:2]) + m[0:1]).astype(BF16)
    tm = h.shape[0]
    plain_refs = (gv_ref, gr_ref, nq_ref, nk_ref, nv_ref)
    offs = np.cumsum([0, qk_ref.shape[-1]] + [r.shape[-1] for r in plain_refs])

    def project(k):
        ref = plain_refs[k]
        ref[0] = _dot(h, w_ref[:, offs[k + 1]:offs[k + 2]]).astype(ref.dtype)

    lr = _dot(h, w_ref[:, offs[-1]:offs[-1] + LR_PAD])
    logits = [_dot3(lr, wg_ref[d]) + bg_ref[d] for d in range(2)]
    project(0)
    C = GLA_CHUNK
    row = lax.broadcasted_iota(jnp.int32, (C, C), 0)
    colm = lax.broadcasted_iota(jnp.int32, (C, C), 1)
    for d, out_ref in enumerate((bf_ref, bb_ref)):
        gh, gl = _split(_log_decay(logits[d]))
        project(1 + 2 * d)
        tri = (row >= colm) if d == 0 else (row <= colm)
        tmat = jnp.where(tri, 1.0, 0.0).astype(BF16)
        for c in range(tm // C):
            sl = slice(c * C, (c + 1) * C)
            out_ref[0, sl, :] = _dot(tmat, gh[sl]) + _dot(tmat, gl[sl])
        project(2 + 2 * d)
    n = qk_ref.shape[-1]
    qk = _dot(h, w_ref[:, 0:n])
    lane = lax.broadcasted_iota(jnp.int32, (1, n), 1)
    if rope:
        reps = n // cos_ref.shape[-1]
        cos = jnp.concatenate([cos_ref[...]] * reps, axis=1)
        sin = jnp.concatenate([sin_ref[...]] * reps, axis=1)
        quarter = GLA_DK // 4
        first = (lane % (2 * quarter)) < quarter
        partner = jnp.where(first, pltpu.roll(qk, n - quarter, 1), pltpu.roll(qk, quarter, 1))
        qk = qk * cos + partner * sin
    qk_ref[0] = jnp.where(lane < GLA_KDIM, qk * (GLA_DK ** -0.5), qk).astype(qk_ref.dtype)


def _inproj(x, mod, mod_row0, g, w, wgp, bgp, rope_tables=None):
    nb, rows, d = x.shape
    tm = min(TOKEN_TILE, rows)
    rope = rope_tables is not None
    widths = (2 * GLA_KDIM, GLA_WIDTH, GLA_WIDTH, NA_WIDTH, NA_WIDTH, NA_WIDTH, GLA_KDIM, GLA_KDIM)
    dtypes = (BF16,) * 6 + (F32, F32)
    tok = lambda n: pl.BlockSpec((1, tm, n), lambda b, i: (b, i, 0))
    const = lambda shape: pl.BlockSpec(shape, lambda b, i: (0,) * len(shape))
    args = [x, mod, g.reshape(1, d), w, wgp, bgp]
    in_specs = [tok(d), pl.BlockSpec((1, 6, d), lambda b, i: (b + mod_row0, 0, 0)), const((1, d)),
                const(w.shape), const(wgp.shape), const(bgp.shape)]
    if rope:
        args += list(rope_tables)
        in_specs += [pl.BlockSpec((tm, t.shape[1]), lambda b, i: (i, 0)) for t in rope_tables]
    est = 2 * (tm * d * 4 + d * w.shape[1] * 2 + sum(widths) * tm * 4) + tm * d * 8
    return pl.pallas_call(
        functools.partial(_inproj_kernel, rope=rope),
        grid=(nb, rows // tm),
        in_specs=in_specs,
        out_specs=[tok(n) for n in widths],
        out_shape=[jax.ShapeDtypeStruct((nb, rows, n), dt) for n, dt in zip(widths, dtypes)],
        compiler_params=_params(("parallel", "parallel"), est),
        name="inproj",
    )(*args)


def _gla_kernel(q_ref, k_ref, v_ref, r_ref, bf_ref, bb_ref, cq_ref, ck_ref, cv_ref, cr_ref, cbf_ref, cbb_ref,
                gain_ref, o_ref, oc_ref, of_ref, ocf_ref, st_ref):
    C = GLA_CHUNK
    L = q_ref.shape[1]
    Lc = cq_ref.shape[1]
    row = lax.broadcasted_iota(jnp.int32, (C, C), 0)
    col = lax.broadcasted_iota(jnp.int32, (C, C), 1)
    lane = lax.broadcasted_iota(jnp.int32, (1, 2 * GLA_DK), 1)
    head0 = lane < GLA_DK
    st_row = lax.broadcasted_iota(jnp.int32, (2 * GLA_DV, 2 * GLA_DK), 0) // GLA_DV
    st_col = lax.broadcasted_iota(jnp.int32, (2 * GLA_DV, 2 * GLA_DK), 1) // GLA_DK
    blockdiag = st_row == st_col
    gain = gain_ref[...]

    def chunk_steps(refs, chunks, acc_ref):
        rq, rk, rv = refs[:3]
        dirs = (0, 1)
        sls = [pl.ds(pl.multiple_of(i * C, C), C) for i in chunks]
        tris = [row >= col, row <= col]
        vs = [rv[0, sl, :] for sl in sls]
        qfs = [rq[0, sl, :].astype(F32) for sl in sls]
        kfs = [rk[0, sl, :].astype(F32) for sl in sls]
        bs = [refs[3 + d][0, sls[d], :] for d in dirs]
        b_mids = [b[C // 2:C // 2 + 1] for b in bs]
        b_edges = [bs[0][C - 1:C], bs[1][0:1]]
        qes = [(qfs[d] * jnp.exp(bs[d] - b_mids[d])).astype(BF16) for d in dirs]
        kes = [(kfs[d] * jnp.exp(b_mids[d] - bs[d])).astype(BF16) for d in dirs]
        zero = jnp.zeros_like(qes[0])
        lhss = [jnp.concatenate([jnp.where(head0, qe, zero), jnp.where(head0, zero, qe)], axis=0) for qe in qes]
        scores = [_dot_nt(lhss[d], kes[d]) for d in dirs]
        qbs = [(qfs[d] * jnp.exp(bs[d])).astype(BF16) for d in dirs]
        kds = [(kfs[d] * jnp.exp(b_edges[d] - bs[d])).astype(BF16) for d in dirs]
        sts = [st_ref[d] for d in dirs]
        inter = [_dot_nt(qbs[d], sts[d].astype(BF16)) for d in dirs]
        dss = [_dot_tn(vs[d], kds[d]) for d in dirs]
        ams = [jnp.where(jnp.concatenate([tris[d], tris[d]], axis=0), scores[d], 0.0).astype(BF16) for d in dirs]
        intra = [jnp.concatenate([_dot(ams[d][:C], vs[d][:, :GLA_DV]), _dot(ams[d][C:], vs[d][:, GLA_DV:])], axis=1)
                 for d in dirs]
        for d in dirs:
            st_ref[d] = sts[d] * jnp.exp(b_edges[d]) + jnp.where(blockdiag, dss[d], 0.0)
            acc_ref[d, sls[d], :] = intra[d] + inter[d]

    def finish(total, r):
        outs = []
        for h in range(2):
            oh = _rms(total[:, h * GLA_DV:(h + 1) * GLA_DV]) * gain
            outs.append(oh * _silu(r[:, h * GLA_DV:(h + 1) * GLA_DV].astype(F32)))
        return jnp.concatenate(outs, axis=1)

    def scan(refs, n, acc_ref, r_ref_, out_ref):
        def body(j, carry):
            chunk_steps(refs, (j, n - 1 - j), acc_ref)
            return carry

        lax.fori_loop(0, n, body, 0)

        def fin(i, carry):
            sl = pl.ds(pl.multiple_of(i * C, C), C)
            out_ref[0, sl, :] = finish(acc_ref[0, sl, :] + acc_ref[1, sl, :], r_ref_[0, sl, :]).astype(out_ref.dtype)
            return carry

        lax.fori_loop(0, n, fin, 0)

    st_ref[...] = jnp.zeros_like(st_ref)
    scan((cq_ref, ck_ref, cv_ref, cbf_ref, cbb_ref), Lc // C, ocf_ref, cr_ref, oc_ref)
    scan((q_ref, k_ref, v_ref, bf_ref, bb_ref), L // C, of_ref, r_ref, o_ref)


def _gla(qk, gv, gr, bf, bb, cqk, cgv, cgr, cbf, cbb, gain):
    B, L, _ = qk.shape
    Lc = cqk.shape[1]
    pair = 2 * GLA_DK
    pv = 2 * GLA_DV
    n_pair = GLA_HEADS // 2
    seq = lambda rows, n, off: pl.BlockSpec((1, rows, n), lambda b, p: (b, 0, p + off))
    const = lambda shape: pl.BlockSpec(shape, lambda b, p: (0,) * len(shape))
    est = (2 * (2 * L * pair * 2 + 2 * L * pv * 2 + 2 * L * pair * 4 + L * pv * 2)
           + 2 * L * pv * 4 + (8 << 20))
    return pl.pallas_call(
        _gla_kernel,
        grid=(B, n_pair),
        in_specs=[
            seq(L, pair, 0), seq(L, pair, n_pair), seq(L, pv, 0), seq(L, pv, 0), seq(L, pair, 0), seq(L, pair, 0),
            seq(Lc, pair, 0), seq(Lc, pair, n_pair), seq(Lc, pv, 0), seq(Lc, pv, 0), seq(Lc, pair, 0),
            seq(Lc, pair, 0),
            const((1, GLA_DV)),
        ],
        out_specs=[seq(L, pv, 0), seq(Lc, pv, 0)],
        out_shape=[jax.ShapeDtypeStruct((B, L, GLA_WIDTH), BF16), jax.ShapeDtypeStruct((B, Lc, GLA_WIDTH), BF16)],
        scratch_shapes=[pltpu.VMEM((2, L, pv), F32), pltpu.VMEM((2, Lc, pv), F32), pltpu.VMEM((2, pv, pair), F32)],
        compiler_params=_params(("parallel", "parallel"), est),
        name="gla",
    )(qk, qk, gv, gr, bf, bb, cqk, cqk, cgv, cgr, cbf, cbb, gain.reshape(1, GLA_DV))


def _rope_tables(L):
    pos = np.arange(L)
    half = GLA_DK // 4
    inv = ROPE_BASE ** (-np.arange(half, dtype=np.float64) / half)
    lane = np.arange(2 * GLA_DK)
    jj = lane % (GLA_DK // 2)
    use_col = (lane % GLA_DK) >= GLA_DK // 2
    p = np.where(use_col[None, :], (pos % GRID_W)[:, None], (pos // GRID_W)[:, None]).astype(np.float64)
    ang = p * inv[jj % half][None, :]
    first = jj < half
    cos = np.cos(ang)
    sin = np.where(first[None, :], -np.sin(ang), np.sin(ang))
    return jnp.asarray(cos, F32), jnp.asarray(sin, F32)


def _softmax_pv(s_parts, v_parts):
    m = s_parts[0].max(axis=-1, keepdims=True)
    for s in s_parts[1:]:
        m = jnp.maximum(m, s.max(axis=-1, keepdims=True))
    den = None
    acc = None
    for s, v in zip(s_parts, v_parts):
        p = jnp.exp(s - m)
        den = p.sum(axis=-1, keepdims=True) if den is None else den + p.sum(axis=-1, keepdims=True)
        pv = _dot(p.astype(BF16), v)
        acc = pv if acc is None else acc + pv
    return acc / den


def _na_window_start():
    cols = np.arange(GRID_W)
    return np.clip(cols - NA_WIN_W // 2, 0, GRID_W - NA_WIN_W)


def _na_kernel(q_ref, k_ref, v_ref, ck_ref, cv_ref, bias_ref, o_ref):
    W = GRID_W
    rows = k_ref.shape[1] // W
    n_loc = bias_ref.shape[2]
    kh = n_loc // W
    r0 = pl.program_id(1) * NA_ROWS_PER_STEP
    pair = 2 * NA_DH
    lane = lax.broadcasted_iota(jnp.int32, (1, pair), 1)
    head0 = lane < NA_DH
    scale = jnp.asarray(NA_DH ** -0.5, q_ref.dtype)

    n_pairs = NA_HEADS // 2
    lanes = [slice(p * pair, (p + 1) * pair) for p in range(n_pairs)]

    def rows_body(it, carry):
        units = []
        for j in range(NA_ROWS_PER_ITER):
            rr = it * NA_ROWS_PER_ITER + j
            r = r0 + rr
            rs = jnp.clip(r - kh // 2, 0, rows - kh)
            qs = pl.ds(pl.multiple_of(rr * W, W), W)
            ks = pl.ds(pl.multiple_of(rs * W, W), n_loc)
            units += [(qs, ks, rs - r + NA_WIN_H - 1, p) for p in range(n_pairs)]
        sts = []
        for qs, ks, dr, p in units:
            q = q_ref[0, qs, lanes[p]] * scale
            zero = jnp.zeros_like(q)
            q2 = jnp.concatenate([jnp.where(head0, q, zero), jnp.where(head0, zero, q)], axis=0)
            keys = jnp.concatenate([k_ref[0, ks, lanes[p]], ck_ref[0, :, lanes[p]]], axis=0)
            sts.append(_dot_nt(keys, q2))
        pts = []
        for (qs, ks, dr, p), st in zip(units, sts):
            st = jnp.concatenate([st[:n_loc] + bias_ref[p, dr], st[n_loc:]], axis=0)
            e = jnp.exp(st - st.max(axis=0, keepdims=True))
            pts.append((e * (1.0 / e.sum(axis=0, keepdims=True))).astype(BF16))
        outs = []
        for (qs, ks, dr, p), pt in zip(units, pts):
            vals = jnp.concatenate([v_ref[0, ks, lanes[p]], cv_ref[0, :, lanes[p]]], axis=0)
            o2 = _dot_tn(pt, vals)
            outs.append(jnp.where(head0, o2[:W], o2[W:]))
        for j in range(NA_ROWS_PER_ITER):
            qs = units[j * n_pairs][0]
            o_ref[0, qs, :] = jnp.concatenate(outs[j * n_pairs:(j + 1) * n_pairs], axis=1).astype(o_ref.dtype)
        return carry

    lax.fori_loop(0, NA_ROWS_PER_STEP // NA_ROWS_PER_ITER, rows_body, 0)


def _na(nq, nk, nv, cnk, cnv, bias):
    B, L, n = nq.shape
    Lc = cnk.shape[1]
    tq = NA_ROWS_PER_STEP * GRID_W
    full = lambda rows: pl.BlockSpec((1, rows, n), lambda b, i: (b, 0, 0))
    est = 2 * (2 * L * n * 2 + 2 * Lc * n * 2 + bias.size * 4 + 2 * tq * n * 2) + (8 << 20)
    return pl.pallas_call(
        _na_kernel,
        grid=(B, L // tq),
        in_specs=[
            pl.BlockSpec((1, tq, n), lambda b, i: (b, i, 0)),
            full(L), full(L), full(Lc), full(Lc),
            pl.BlockSpec(bias.shape, lambda b, i: (0, 0, 0, 0)),
        ],
        out_specs=pl.BlockSpec((1, tq, n), lambda b, i: (b, i, 0)),
        out_shape=jax.ShapeDtypeStruct((B, L, n), BF16),
        compiler_params=_params(("parallel", "parallel"), est),
        name="neighbourhood_attention",
    )(nq, nk, nv, cnk, cnv, bias)


def _na_bias_table(rpb, rows):
    kh = min(NA_WIN_H, rows)
    start = _na_window_start()
    kc = np.arange(GRID_W)
    inside = (kc[None, :] >= start[:, None]) & (kc[None, :] < start[:, None] + NA_WIN_W)
    sel = np.zeros((2 * NA_WIN_W - 1, GRID_W, GRID_W), np.float32)
    qq, kk = np.nonzero(inside)
    sel[kk - qq + NA_WIN_W - 1, qq, kk] = 1.0
    by_row = jnp.stack([rpb[:, d:d + kh, :] for d in range(NA_WIN_H)], axis=1)
    t = jnp.einsum('hdic,cqk->hdikq', by_row, jnp.asarray(sel), precision=lax.Precision.HIGHEST)
    t = t + jnp.asarray(np.where(inside, 0.0, MASK_VALUE).T, F32)[None, None, None, :, :]
    t = t.reshape(NA_HEADS // 2, 2, NA_WIN_H, kh * GRID_W, GRID_W)
    return jnp.transpose(t, (0, 2, 3, 1, 4)).reshape(NA_HEADS // 2, NA_WIN_H, kh * GRID_W, 2 * GRID_W)


def _ctx_attn_kernel(q_ref, k_ref, v_ref, o_ref):
    pair = 2 * NA_DH
    lane = lax.broadcasted_iota(jnp.int32, (1, pair), 1)
    head0 = lane < NA_DH
    Lc = q_ref.shape[1]
    outs = []
    for p in range(NA_HEADS // 2):
        ls = slice(p * pair, (p + 1) * pair)
        q = q_ref[0, :, ls]
        zero = jnp.zeros_like(q)
        q2 = jnp.concatenate([jnp.where(head0, q, zero), jnp.where(head0, zero, q)], axis=0)
        s = _dot_nt(q2, k_ref[0, :, ls]) * (NA_DH ** -0.5)
        o2 = _softmax_pv([s], [v_ref[0, :, ls]])
        outs.append(jnp.where(head0, o2[:Lc], o2[Lc:]))
    o_ref[0] = jnp.concatenate(outs, axis=1).astype(o_ref.dtype)


def _ctx_attn(cnq, cnk, cnv):
    B, Lc, n = cnq.shape
    spec = pl.BlockSpec((1, Lc, n), lambda b: (b, 0, 0))
    return pl.pallas_call(
        _ctx_attn_kernel,
        grid=(B,),
        in_specs=[spec, spec, spec],
        out_specs=spec,
        out_shape=jax.ShapeDtypeStruct((B, Lc, n), BF16),
        compiler_params=_params(("parallel",), 8 * Lc * n * 2 + (8 << 20)),
        name="context_attention",
    )(cnq, cnk, cnv)


def _pack_bf16_pairs(x):
    n = x.shape[1] // 2
    lo = lax.bitcast_convert_type(x[:, :n].astype(BF16).astype(F32), jnp.int32)
    hi = lax.bitcast_convert_type(x[:, n:].astype(BF16).astype(F32), jnp.int32)
    return lax.shift_right_logical(lo, 16) | (hi & jnp.int32(-65536))


def _unpack_bf16_pairs(p):
    lo = lax.bitcast_convert_type(lax.shift_left(p, 16), F32)
    hi = lax.bitcast_convert_type(p & jnp.int32(-65536), F32)
    return jnp.concatenate([lo, hi], axis=1)


def _store_packed(refs, x):
    n = 2 * SC_ROW_WORDS
    for p, ref in enumerate(refs):
        ref[...] = _pack_bf16_pairs(x[:, p * n:(p + 1) * n]).reshape(ref.shape)


def _load_packed(refs):
    return jnp.concatenate([_unpack_bf16_pairs(ref[...].reshape(ref.shape[-2:])) for ref in refs], axis=1)


def _route(logits):
    lane = lax.broadcasted_iota(jnp.int32, logits.shape, 1)
    big = jnp.int32(logits.shape[1])
    t1 = logits.max(axis=-1, keepdims=True)
    i1 = jnp.where(logits == t1, lane, big).min(axis=-1, keepdims=True)
    rest = jnp.where(lane == i1, -jnp.inf, logits)
    t2 = rest.max(axis=-1, keepdims=True)
    i2 = jnp.where(rest == t2, lane, big).min(axis=-1, keepdims=True)
    e2 = jnp.exp(t2 - t1)
    return i1, i2, 1.0 / (1.0 + e2), e2 / (1.0 + e2)


def _outproj_kernel(*refs, dense_ffn):
    ga_ref, na_ref, x_ref, mod_ref, wa_ref, wb_ref, gp_ref, gf_ref = refs[:8]
    m = mod_ref[0]
    y = _dot(ga_ref[0], wa_ref[...]) + _dot(na_ref[0], wb_ref[...])
    x1 = x_ref[0] + m[2:3] * (_rms(y) * gp_ref[...])
    h2 = _rms(x1) * gf_ref[...] * (1.0 + m[4:5]) + m[3:4]
    if dense_ffn:
        wg_ref, wu_ref, wd_ref, g2_ref, o_ref = refs[8:]
        h = h2.astype(BF16)
        hid = (_silu(_dot(h, wg_ref[...])) * _dot(h, wu_ref[...])).astype(BF16)
        o_ref[0] = x1 + m[5:6] * (_rms(_dot(hid, wd_ref[...])) * g2_ref[...])
    else:
        wr_ref, x1_ref, gw_ref, eid_ref, cnt_ref, *h2_refs = refs[8:]
        x1_ref[0] = x1
        _store_packed(h2_refs, h2)
        tm = h2.shape[0]
        lanes = wr_ref.shape[1] // 2
        prod = _dot(jnp.concatenate(_split(h2), axis=0), wr_ref[...])
        logits = (prod[:tm, :lanes] + prod[tm:, :lanes]) + (prod[:tm, lanes:] + prod[tm:, lanes:])
        lane = lax.broadcasted_iota(jnp.int32, (tm, lanes), 1)
        logits = jnp.where(lane < N_EXPERTS, logits, -jnp.inf)
        i1, i2, w1, w2 = _route(logits)
        gw_ref[0] = jnp.where(lane == 0, w1, jnp.where(lane == 1, w2, 0.0))
        eid_ref[0] = jnp.where(lane == 0, i1, jnp.where(lane == 1, i2, 0))
        chosen = jnp.where(lane == i1, 1.0, jnp.where(lane == i2, 1.0, 0.0))
        cnt_ref[0] = jnp.broadcast_to(chosen.sum(axis=0, keepdims=True), cnt_ref.shape[1:])


def _outproj(ga, na, x, mod, mod_row0, w_out, g_post, g_ffn, *, ffn=None, w_router=None):
    nb, rows, d = x.shape
    tm = min(TOKEN_TILE, rows)
    dense_ffn = ffn is not None
    tok = lambda n: pl.BlockSpec((1, tm, n), lambda b, i: (b, i, 0))
    const = lambda shape: pl.BlockSpec(shape, lambda b, i: (0,) * len(shape))
    once = lambda shape: pl.BlockSpec(shape, lambda b, i: (0,) * len(shape), pipeline_mode=pl.Buffered(1))
    wa = w_out[:GLA_WIDTH].astype(BF16)
    wb = w_out[GLA_WIDTH:].astype(BF16)
    args = [ga, na, x, mod, wa, wb, g_post.reshape(1, d), g_ffn.reshape(1, d)]
    in_specs = [tok(GLA_WIDTH), tok(NA_WIDTH), tok(d),
                pl.BlockSpec((1, 6, d), lambda b, i: (b + mod_row0, 0, 0)),
                const(wa.shape), const(wb.shape), const((1, d)), const((1, d))]
    est = 2 * (tm * d * (4 + 4 + 2) + 2 * tm * GLA_WIDTH * 2 + d * d * 2) + 4 * tm * d * 4
    if dense_ffn:
        wg, wu, wd, g2 = ffn
        dff = wg.shape[1]
        args += [wg, wu, wd, g2.reshape(1, d)]
        in_specs += [once(wg.shape), once(wu.shape), once(wd.shape), const((1, d))]
        out_specs = tok(d)
        out_shape = jax.ShapeDtypeStruct((nb, rows, d), F32)
        est += 3 * d * dff * 2 + 3 * tm * dff * 4
    else:
        nt = rows // tm
        wr = jnp.zeros((d, V7X_LANES), F32).at[:, :N_EXPERTS].set(w_router)
        wr = jnp.concatenate(_split(wr), axis=1)
        args.append(wr)
        in_specs.append(const(wr.shape))
        n_parts = d // (2 * SC_ROW_WORDS)
        out_specs = [tok(d), tok(V7X_LANES), tok(V7X_LANES),
                     pl.BlockSpec((1, 8, V7X_LANES), lambda b, i: (b * nt + i, 0, 0))] + [tok(SC_ROW_WORDS)] * n_parts
        out_shape = [jax.ShapeDtypeStruct((nb, rows, d), F32),
                     jax.ShapeDtypeStruct((nb, rows, V7X_LANES), F32),
                     jax.ShapeDtypeStruct((nb, rows, V7X_LANES), jnp.int32),
                     jax.ShapeDtypeStruct((nb * nt, 8, V7X_LANES), F32)]
        out_shape += [jax.ShapeDtypeStruct((nb, rows, SC_ROW_WORDS), jnp.int32)] * n_parts
    return pl.pallas_call(
        functools.partial(_outproj_kernel, dense_ffn=dense_ffn),
        grid=(nb, rows // tm),
        in_specs=in_specs,
        out_specs=out_specs,
        out_shape=out_shape,
        compiler_params=_params(("parallel", "parallel"), est),
        name="outproj_ffn" if dense_ffn else "outproj_router",
    )(*args)


def _slot_kernel(eid_ref, base_ref, pos_ref):
    eid = eid_ref[...]
    tm, lanes = eid.shape
    lane = lax.broadcasted_iota(jnp.int32, (tm, lanes), 1)
    i1 = eid[:, 0:1]
    i2 = eid[:, 1:2]
    chosen = jnp.where(lane == i1, 1.0, jnp.where(lane == i2, 1.0, 0.0)).astype(BF16)
    row = lax.broadcasted_iota(jnp.int32, (tm, tm), 0)
    col = lax.broadcasted_iota(jnp.int32, (tm, tm), 1)
    incl = jnp.where(row >= col, 1.0, 0.0).astype(BF16)
    slot = base_ref[0][0:1] + _dot(incl, chosen) - 1.0
    p1 = jnp.where(lane == i1, slot, 0.0).sum(axis=-1, keepdims=True).astype(jnp.int32)
    p2 = jnp.where(lane == i2, slot, 0.0).sum(axis=-1, keepdims=True).astype(jnp.int32)
    pos_ref[...] = jnp.where(lane == 0, p1, jnp.where(lane == 1, p2, 0))


def _slots(eid, tile_base, tm):
    t, lanes = eid.shape
    return pl.pallas_call(
        _slot_kernel,
        grid=(t // tm,),
        in_specs=[pl.BlockSpec((tm, lanes), lambda i: (i, 0)),
                  pl.BlockSpec((1, 8, lanes), lambda i: (i, 0, 0))],
        out_specs=pl.BlockSpec((tm, lanes), lambda i: (i, 0)),
        out_shape=jax.ShapeDtypeStruct((t, lanes), jnp.int32),
        compiler_params=_params(("parallel",), 8 * tm * lanes * 4 + 4 * tm * tm),
        name="moe_slots",
    )(eid, tile_base)


def _route_plan(cnt, tm_tokens):
    counts = cnt[:, 0, :N_EXPERTS].astype(jnp.int32)
    total = counts.sum(axis=0)
    padded = -(-total // MOE_ROW_TILE) * MOE_ROW_TILE
    start = jnp.cumsum(padded) - padded
    before = jnp.cumsum(counts, axis=0) - counts
    tile_base = (start[None, :] + before).astype(F32)
    tile_base = jnp.zeros((cnt.shape[0], 8, V7X_LANES), F32).at[:, :, :N_EXPERTS].set(tile_base[:, None, :])
    n_slots = tm_tokens * 2 + N_EXPERTS * MOE_ROW_TILE
    first_row = jnp.arange(n_slots // MOE_ROW_TILE, dtype=jnp.int32) * MOE_ROW_TILE
    tile_expert = jnp.minimum((first_row[:, None] >= (start + padded)[None, :]).sum(axis=1), N_EXPERTS - 1)
    n_valid = jnp.clip(start[tile_expert] + total[tile_expert] - first_row, 0, MOE_ROW_TILE)
    return tile_base, tile_expert.astype(jnp.int32), n_valid.astype(jnp.int32), n_slots


def _sc_mesh():
    return plsc.VectorSubcoreMesh(core_axis_name="core", subcore_axis_name="subcore")


def _scatter_rows(x, idx, n_out):
    t, w = x.shape
    n = idx.shape[0]
    win = SC_ROW_WINDOW
    n_blk = t // win

    @functools.partial(pl.kernel, out_type=jax.ShapeDtypeStruct((n_out, w), x.dtype), mesh=_sc_mesh(),
                       scratch_types=[], name="moe_dispatch")
    def scatter(x_hbm, i_hbm, o_hbm):
        def body(x_vmem, i_vmem):
            pltpu.sync_copy(x_vmem, o_hbm.at[i_vmem.at[0]])

        pltpu.emit_pipeline(
            body,
            grid=(n // win,),
            in_specs=[pl.BlockSpec((win, w), lambda i: (i % n_blk, 0)),
                      pl.BlockSpec((1, win), lambda i: (0, i))],
            out_specs=[],
            core_axis_name=("core", "subcore"),
            dimension_semantics=(pltpu.PARALLEL,),
        )(x_hbm, i_hbm)

    return scatter(x, idx.reshape(1, n))


def _gather_rows(x, idx):
    n = idx.shape[0]
    w = x.shape[1]
    win = SC_ROW_WINDOW

    @functools.partial(pl.kernel, out_type=jax.ShapeDtypeStruct((n, w), x.dtype), mesh=_sc_mesh(),
                       scratch_types=[], name="moe_combine_gather")
    def gather(x_hbm, i_hbm, o_hbm):
        def body(i_vmem, o_vmem):
            pltpu.sync_copy(x_hbm.at[i_vmem.at[0]], o_vmem)

        pltpu.emit_pipeline(
            body,
            grid=(n // win,),
            in_specs=[pl.BlockSpec((1, win), lambda i: (0, i))],
            out_specs=[pl.BlockSpec((win, w), lambda i: (i, 0))],
            core_axis_name=("core", "subcore"),
            dimension_semantics=(pltpu.PARALLEL,),
        )(i_hbm, o_hbm)

    return gather(x, idx.reshape(1, n))


def _experts_kernel(te_ref, nv_ref, *refs, n_parts):
    x_refs = refs[:n_parts]
    wg_ref, wu_ref, wd_ref = refs[n_parts:n_parts + 3]
    y_refs = refs[n_parts + 3:2 * n_parts + 3]
    i = pl.program_id(0)
    n_valid = nv_ref[i]
    tm = x_refs[0].shape[0]
    dff = wg_ref.shape[2]

    @pl.when(n_valid > 0)
    def _():
        row = lax.broadcasted_iota(jnp.int32, (tm, 1), 0)
        x = jnp.where(row < n_valid, _load_packed(x_refs), 0.0).astype(BF16)
        acc = None
        for lo in range(0, dff, MOE_FF_CHUNK):
            sl = slice(lo, lo + MOE_FF_CHUNK)
            hid = (_silu(_dot(x, wg_ref[0, :, sl])) * _dot(x, wu_ref[0, :, sl])).astype(BF16)
            part = _dot(hid, wd_ref[0, sl, :])
            acc = part if acc is None else acc + part
        _store_packed(y_refs, acc)

    @pl.when(n_valid == 0)
    def _():
        for ref in y_refs:
            ref[...] = jnp.zeros_like(ref)


def _experts(xs_parts, tile_expert, n_valid, wg, wu, wd):
    n_parts = len(xs_parts)
    n_slots, words = xs_parts[0].shape
    n_e, d, dff = wg.shape
    tm = MOE_ROW_TILE
    rows_spec = pl.BlockSpec((tm, words), lambda i, te, nv: (i, 0))
    weights = lambda shape: pl.BlockSpec((1,) + shape, lambda i, te, nv: (te[i], 0, 0), pipeline_mode=pl.Buffered(1))
    est = 3 * d * dff * 2 + 4 * tm * d * 2 + 3 * tm * MOE_FF_CHUNK * 4 + 3 * tm * d * 4
    grid_spec = pltpu.PrefetchScalarGridSpec(
        num_scalar_prefetch=2,
        grid=(n_slots // tm,),
        in_specs=[rows_spec] * n_parts + [weights((d, dff)), weights((d, dff)), weights((dff, d))],
        out_specs=[rows_spec] * n_parts,
    )
    return pl.pallas_call(
        functools.partial(_experts_kernel, n_parts=n_parts),
        grid_spec=grid_spec,
        out_shape=[jax.ShapeDtypeStruct((n_slots, words), jnp.int32)] * n_parts,
        compiler_params=_params(("arbitrary",), est),
        name="moe_experts",
    )(tile_expert, n_valid, *xs_parts, wg, wu, wd)


def _combine_kernel(*refs, n_parts):
    y1_refs = refs[:n_parts]
    y2_refs = refs[n_parts:2 * n_parts]
    gw_ref, x_ref, mod_ref, g_ref, o_ref = refs[2 * n_parts:]
    gw = gw_ref[0]
    y = gw[:, 0:1] * _load_packed(y1_refs) + gw[:, 1:2] * _load_packed(y2_refs)
    o_ref[0] = x_ref[0] + mod_ref[0][5:6] * (_rms(y) * g_ref[...])


def _combine(ys2_parts, gw, x1, mod, mod_row0, g_post):
    n_parts = len(ys2_parts)
    nb, rows, d = x1.shape
    words = ys2_parts[0].shape[-1]
    tm = min(TOKEN_TILE, rows)
    tok = lambda n: pl.BlockSpec((1, tm, n), lambda b, i: (b, i, 0))
    ysp = lambda k: pl.BlockSpec((1, tm, words), lambda b, i: (k * nb + b, i, 0))
    est = 2 * tm * (d * 2 * 4 + d * 4 + V7X_LANES * 4) + 3 * tm * d * 4
    return pl.pallas_call(
        functools.partial(_combine_kernel, n_parts=n_parts),
        grid=(nb, rows // tm),
        in_specs=[ysp(0)] * n_parts + [ysp(1)] * n_parts + [
            tok(V7X_LANES), tok(d),
            pl.BlockSpec((1, 6, d), lambda b, i: (b + mod_row0, 0, 0)),
            pl.BlockSpec((1, d), lambda b, i: (0, 0))],
        out_specs=tok(d),
        out_shape=jax.ShapeDtypeStruct((nb, rows, d), F32),
        compiler_params=_params(("parallel", "parallel"), est),
        name="moe_combine",
    )(*ys2_parts, *ys2_parts, gw, x1, mod, g_post.reshape(1, d))


def _moe_routed(gw, eid, cnt, *h2_parts, x1, mod, mod_row0, experts, g_post):
    nb, rows, d = x1.shape
    t = nb * rows
    tm = min(TOKEN_TILE, rows)
    tile_base, tile_expert, n_valid, n_slots = _route_plan(cnt, t)
    pos = _slots(eid.reshape(t, V7X_LANES), tile_base, tm)
    idx = jnp.concatenate([pos[:, 0], pos[:, 1]])
    xs = [_scatter_rows(h.reshape(t, h.shape[-1]), idx, n_slots) for h in h2_parts]
    ys = _experts(xs, tile_expert, n_valid, *experts)
    ys2 = [_gather_rows(y, idx).reshape(2 * nb, rows, y.shape[-1]) for y in ys]
    return _combine(ys2, gw, x1, mod, mod_row0, g_post)


def _cast_kernel(w_ref, o_ref):
    o_ref[...] = w_ref[...].astype(o_ref.dtype)


def _to_bf16(w):
    shape = w.shape
    w3 = w.reshape((-1,) + shape[-2:])
    n, r, c = w3.shape
    rb = r
    while rb * c * 4 > CAST_BLOCK_BYTES and rb % 16 == 0:
        rb //= 2
    spec = pl.BlockSpec((1, rb, c), lambda e, i: (e, i, 0))
    out = pl.pallas_call(
        _cast_kernel,
        grid=(n, r // rb),
        in_specs=[spec],
        out_specs=spec,
        out_shape=jax.ShapeDtypeStruct(w3.shape, BF16),
        compiler_params=_params(("parallel", "parallel"), 2 * rb * c * 6),
        name="cast_bf16",
    )(w3)
    return out.reshape(shape)
def _pack_w_in(w):
    a = 2 * GLA_KDIM + 2 * GLA_WIDTH
    lr = 2 * GLA_GATE_RANK
    pad = jnp.zeros((w.shape[0], LR_PAD - lr), w.dtype)
    return jnp.concatenate([w[:, :a], w[:, a + lr:], w[:, a:a + lr], pad], axis=1).astype(BF16)


def _pack_gate(w_gate, b_gate):
    wgp = jnp.zeros((2, LR_PAD, GLA_KDIM), F32)
    for d in range(2):
        wgp = wgp.at[d, d * GLA_GATE_RANK:(d + 1) * GLA_GATE_RANK].set(w_gate[d])
    return wgp, b_gate.reshape(2, 1, GLA_KDIM)


def kernel(x, c, ctx, c_ctx, w_ada, b_ada, g_pre_mix, g_post_mix, g_pre_ffn, g_post_ffn, w_in,
           gla_w_gate, gla_b_gate, gla_g_norm, na_rpb, w_out, ffn_w_gate, ffn_w_up, ffn_w_down,
           moe_w_router, moe_w_gate, moe_w_up, moe_w_down):
    B, L, D = x.shape
    Lc = ctx.shape[1]
    depth = w_ada.shape[0]
    rows = L // GRID_W
    ctx_row = B
    n_cond = -(-(B + 1) // 8) * 8
    cond = jnp.zeros((n_cond, D), F32).at[:B].set(c).at[B].set(c_ctx)
    mod = _modulation(cond, w_ada, b_ada)
    tables = _rope_tables(L)
    ctx = ctx.reshape(1, B * Lc, D)

    for i in range(depth):
        last = i == depth - 1
        j = i // 2
        w = _pack_w_in(w_in[i])
        wgp, bgp = _pack_gate(gla_w_gate[i], gla_b_gate[i])
        qk, gv, gr, nq, nk, nv, bf, bb = _inproj(x, mod[i], 0, g_pre_mix[i], w, wgp, bgp, tables)
        cparts = _inproj(ctx, mod[i], ctx_row, g_pre_mix[i], w, wgp, bgp)
        cqk, cgv, cgr, cnq, cnk, cnv, cbf, cbb = [t.reshape(B, Lc, t.shape[-1]) for t in cparts]
        ga, gac = _gla(qk, gv, gr, bf, bb, cqk, cgv, cgr, cbf, cbb, gla_g_norm[i])
        na = _na(nq, nk, nv, cnk, cnv, _na_bias_table(na_rpb[i], rows))
        if i % 2 == 0:
            ffn = (_to_bf16(ffn_w_gate[j]), _to_bf16(ffn_w_up[j]), _to_bf16(ffn_w_down[j]), g_post_ffn[i])

            def mix_and_ffn(ga_, na_, x_, row0):
                return _outproj(ga_, na_, x_, mod[i], row0, w_out[i], g_post_mix[i], g_pre_ffn[i], ffn=ffn)
        else:
            experts = (_to_bf16(moe_w_gate[j]), _to_bf16(moe_w_up[j]), _to_bf16(moe_w_down[j]))

            def mix_and_ffn(ga_, na_, x_, row0):
                x1, *routed = _outproj(ga_, na_, x_, mod[i], row0, w_out[i], g_post_mix[i], g_pre_ffn[i],
                                       w_router=moe_w_router[j])
                return _moe_routed(*routed, x1=x1, mod=mod[i], mod_row0=row0, experts=experts, g_post=g_post_ffn[i])
        x = mix_and_ffn(ga, na, x, 0)
        if not last:
            nac = _ctx_attn(cnq, cnk, cnv)
            ctx = mix_and_ffn(gac.reshape(1, B * Lc, GLA_WIDTH), nac.reshape(1, B * Lc, NA_WIDTH), ctx, ctx_row)
    return x
```

```python
import functools

import numpy as np
import jax
import jax.numpy as jnp
from jax import lax
from jax.experimental import pallas as pl
from jax.experimental.pallas import tpu as pltpu
from jax.experimental.pallas import tpu_sc as plsc

F32 = jnp.float32
BF16 = jnp.bfloat16

GRID_W = 64
GLA_HEADS = 4
GLA_DV = 128
GLA_DK = 64
GLA_KDIM = GLA_HEADS * GLA_DK
GLA_WIDTH = GLA_HEADS * GLA_DV
GLA_GATE_RANK = 16
GLA_GATE_NORM = 16.0
NA_HEADS = 8
NA_DH = 64
NA_WIDTH = NA_HEADS * NA_DH
NA_WIN_H = 8
NA_WIN_W = 16
ROPE_BASE = 10000.0
N_EXPERTS = 8
EPS = 1e-6

V7X_LANES = 128
V7X_VMEM_BYTES = 64 * 1024 * 1024
V7X_VMEM_USABLE = V7X_VMEM_BYTES - 8 * 1024 * 1024

TOKEN_TILE = 512
GLA_CHUNK = 128
GLA_UNROLL = 4
NA_ROWS_PER_STEP = 8
NA_ROWS_PER_ITER = 2
MASK_VALUE = -1e30
MOE_FF_CHUNK = 1792
MOE_ROW_TILE = 512
SC_ROW_WINDOW = 128
CAST_BLOCK_BYTES = 4 << 20
SC_ROW_WORDS = 256
LR_PAD = V7X_LANES


def _vmem_limit(estimate_bytes):
    return int(min(V7X_VMEM_USABLE, estimate_bytes * 5 // 4 + (4 << 20)))


def _params(semantics, vmem_estimate):
    return pltpu.CompilerParams(dimension_semantics=semantics, vmem_limit_bytes=_vmem_limit(vmem_estimate))


def _dot(a, b):
    return jnp.dot(a, b, preferred_element_type=F32)


def _dot_nt(a, b):
    return lax.dot_general(a, b, (((1,), (1,)), ((), ())), preferred_element_type=F32)


def _dot_tn(a, b):
    return lax.dot_general(a, b, (((0,), (0,)), ((), ())), preferred_element_type=F32)


def _split(x):
    hi = x.astype(BF16)
    lo = (x - hi.astype(F32)).astype(BF16)
    return hi, lo


def _dot3(a, b):
    ah, al = _split(a)
    bh, bl = _split(b)
    return _dot(ah, bh) + (_dot(al, bh) + _dot(ah, bl))


def _sigmoid(x):
    return 1.0 / (1.0 + jnp.exp(-x))


def _silu(x):
    return x * _sigmoid(x)


def _rms(x):
    return x * lax.rsqrt(jnp.mean(x * x, axis=-1, keepdims=True) + EPS)


def _mod_kernel(c_ref, w_ref, b_ref, o_ref):
    o_ref[0] = _dot3(_silu(c_ref[...]), w_ref[0]) + b_ref[0]


def _modulation(cond, w_ada, b_ada):
    depth, d, n = w_ada.shape
    rows = cond.shape[0]
    tn = 512
    out = pl.pallas_call(
        _mod_kernel,
        grid=(depth, n // tn),
        in_specs=[
            pl.BlockSpec((rows, d), lambda l, j: (0, 0)),
            pl.BlockSpec((1, d, tn), lambda l, j: (l, 0, j)),
            pl.BlockSpec((1, 1, tn), lambda l, j: (l, 0, j)),
        ],
        out_specs=pl.BlockSpec((1, rows, tn), lambda l, j: (l, 0, j)),
        out_shape=jax.ShapeDtypeStruct((depth, rows, n), F32),
        compiler_params=_params(("parallel", "parallel"), 3 * d * tn * 4 * 2),
        name="modulation",
    )(cond, w_ada, b_ada.reshape(depth, 1, n))
    return out.reshape(depth, rows, 6, d)


def _log_decay(logit):
    return (jnp.minimum(logit, 0.0) - jnp.log(1.0 + jnp.exp(-jnp.abs(logit)))) * (1.0 / GLA_GATE_NORM)


def _inproj_kernel(*refs, rope):
    if rope:
        x_ref, mod_ref, g_ref, w_ref, wg_ref, bg_ref, cos_ref, sin_ref = refs[:8]
    else:
        x_ref, mod_ref, g_ref, w_ref, wg_ref, bg_ref = refs[:6]
    qk_ref, gv_ref, gr_ref, nq_ref, nk_ref, nv_ref, bf_ref, bb_ref = refs[-8:]
    m = mod_ref[0]
    h = (_rms(x_ref[0]) * g_ref[...] * (1.0 + m[1:2]) + m[0:1]).astype(BF16)
    tm = h.shape[0]
    plain_refs = (gv_ref, gr_ref, nq_ref, nk_ref, nv_ref)
    offs = np.cumsum([0, qk_ref.shape[-1]] + [r.shape[-1] for r in plain_refs])

    def project(k):
        ref = plain_refs[k]
        ref[0] = _dot(h, w_ref[:, offs[k + 1]:offs[k + 2]]).astype(ref.dtype)

    lr = _dot(h, w_ref[:, offs[-1]:offs[-1] + LR_PAD])
    logits = [_dot3(lr, wg_ref[d]) + bg_ref[d] for d in range(2)]
    project(0)
    C = GLA_CHUNK
    row = lax.broadcasted_iota(jnp.int32, (C, C), 0)
    colm = lax.broadcasted_iota(jnp.int32, (C, C), 1)
    for d, out_ref in enumerate((bf_ref, bb_ref)):
        gh, gl = _split(_log_decay(logits[d]))
        project(1 + 2 * d)
        tri = (row >= colm) if d == 0 else (row <= colm)
        tmat = jnp.where(tri, 1.0, 0.0).astype(BF16)
        for c in range(tm // C):
            sl = slice(c * C, (c + 1) * C)
            out_ref[0, sl, :] = _dot(tmat, gh[sl]) + _dot(tmat, gl[sl])
        project(2 + 2 * d)
    n = qk_ref.shape[-1]
    qk = _dot(h, w_ref[:, 0:n])
    lane = lax.broadcasted_iota(jnp.int32, (1, n), 1)
    if rope:
        reps = n // cos_ref.shape[-1]
        cos = jnp.concatenate([cos_ref[...]] * reps, axis=1)
        sin = jnp.concatenate([sin_ref[...]] * reps, axis=1)
        quarter = GLA_DK // 4
        first = (lane % (2 * quarter)) < quarter
        partner = jnp.where(first, pltpu.roll(qk, n - quarter, 1), pltpu.roll(qk, quarter, 1))
        qk = qk * cos + partner * sin
    qk_ref[0] = jnp.where(lane < GLA_KDIM, qk * (GLA_DK ** -0.5), qk).astype(qk_ref.dtype)


def _inproj(x, mod, mod_row0, g, w, wgp, bgp, rope_tables=None):
    nb, rows, d = x.shape
    tm = min(TOKEN_TILE, rows)
    rope = rope_tables is not None
    widths = (2 * GLA_KDIM, GLA_WIDTH, GLA_WIDTH, NA_WIDTH, NA_WIDTH, NA_WIDTH, GLA_KDIM, GLA_KDIM)
    dtypes = (BF16,) * 6 + (F32, F32)
    tok = lambda n: pl.BlockSpec((1, tm, n), lambda b, i: (b, i, 0))
    const = lambda shape: pl.BlockSpec(shape, lambda b, i: (0,) * len(shape))
    args = [x, mod, g.reshape(1, d), w, wgp, bgp]
    in_specs = [tok(d), pl.BlockSpec((1, 6, d), lambda b, i: (b + mod_row0, 0, 0)), const((1, d)),
                const(w.shape), const(wgp.shape), const(bgp.shape)]
    if rope:
        args += list(rope_tables)
        in_specs += [pl.BlockSpec((tm, t.shape[1]), lambda b, i: (i, 0)) for t in rope_tables]
    est = 2 * (tm * d * 4 + d * w.shape[1] * 2 + sum(widths) * tm * 4) + tm * d * 8
    return pl.pallas_call(
        functools.partial(_inproj_kernel, rope=rope),
        grid=(nb, rows // tm),
        in_specs=in_specs,
        out_specs=[tok(n) for n in widths],
        out_shape=[jax.ShapeDtypeStruct((nb, rows, n), dt) for n, dt in zip(widths, dtypes)],
        compiler_params=_params(("parallel", "parallel"), est),
        name="inproj",
    )(*args)


def _gla_kernel(q_ref, k_ref, v_ref, r_ref, bf_ref, bb_ref, cq_ref, ck_ref, cv_ref, cr_ref, cbf_ref, cbb_ref,
                gain_ref, o_ref, oc_ref, of_ref, ocf_ref, st_ref):
    C = GLA_CHUNK
    L = q_ref.shape[1]
    Lc = cq_ref.shape[1]
    row = lax.broadcasted_iota(jnp.int32, (C, C), 0)
    col = lax.broadcasted_iota(jnp.int32, (C, C), 1)
    lane = lax.broadcasted_iota(jnp.int32, (1, 2 * GLA_DK), 1)
    head0 = lane < GLA_DK
    st_row = lax.broadcasted_iota(jnp.int32, (2 * GLA_DV, 2 * GLA_DK), 0) // GLA_DV
    st_col = lax.broadcasted_iota(jnp.int32, (2 * GLA_DV, 2 * GLA_DK), 1) // GLA_DK
    blockdiag = st_row == st_col
    gain = gain_ref[...]

    def chunk_steps(refs, chunks, acc_ref):
        rq, rk, rv = refs[:3]
        dirs = (0, 1)
        sls = [pl.ds(pl.multiple_of(i * C, C), C) for i in chunks]
        tris = [row >= col, row <= col]
        vs = [rv[0, sl, :] for sl in sls]
        qfs = [rq[0, sl, :].astype(F32) for sl in sls]
        kfs = [rk[0, sl, :].astype(F32) for sl in sls]
        bs = [refs[3 + d][0, sls[d], :] for d in dirs]
        b_mids = [b[C // 2:C // 2 + 1] for b in bs]
        b_edges = [bs[0][C - 1:C], bs[1][0:1]]
        qes = [(qfs[d] * jnp.exp(bs[d] - b_mids[d])).astype(BF16) for d in dirs]
        kes = [(kfs[d] * jnp.exp(b_mids[d] - bs[d])).astype(BF16) for d in dirs]
        zero = jnp.zeros_like(qes[0])
        lhss = [jnp.concatenate([jnp.where(head0, qe, zero), jnp.where(head0, zero, qe)], axis=0) for qe in qes]
        scores = [_dot_nt(lhss[d], kes[d]) for d in dirs]
        qbs = [(qfs[d] * jnp.exp(bs[d])).astype(BF16) for d in dirs]
        kds = [(kfs[d] * jnp.exp(b_edges[d] - bs[d])).astype(BF16) for d in dirs]
        sts = [st_ref[d] for d in dirs]
        inter = [_dot_nt(qbs[d], sts[d].astype(BF16)) for d in dirs]
        dss = [_dot_tn(vs[d], kds[d]) for d in dirs]
        ams = [jnp.where(jnp.concatenate([tris[d], tris[d]], axis=0), scores[d], 0.0).astype(BF16) for d in dirs]
        intra = [jnp.concatenate([_dot(ams[d][:C], vs[d][:, :GLA_DV]), _dot(ams[d][C:], vs[d][:, GLA_DV:])], axis=1)
                 for d in dirs]
        for d in dirs:
            st_ref[d] = sts[d] * jnp.exp(b_edges[d]) + jnp.where(blockdiag, dss[d], 0.0)
            acc_ref[d, sls[d], :] = intra[d] + inter[d]

    def finish(total, r):
        outs = []
        for h in range(2):
            oh = _rms(total[:, h * GLA_DV:(h + 1) * GLA_DV]) * gain
            outs.append(oh * _silu(r[:, h * GLA_DV:(h + 1) * GLA_DV].astype(F32)))
        return jnp.concatenate(outs, axis=1)

    def scan(refs, n, acc_ref, r_ref_, out_ref):
        def body(j, carry):
            chunk_steps(refs, (j, n - 1 - j), acc_ref)
            return carry

        lax.fori_loop(0, n, body, 0, unroll=min(GLA_UNROLL, n))

        def fin(i, carry):
            sl = pl.ds(pl.multiple_of(i * C, C), C)
            out_ref[0, sl, :] = finish(acc_ref[0, sl, :] + acc_ref[1, sl, :], r_ref_[0, sl, :]).astype(out_ref.dtype)
            return carry

        lax.fori_loop(0, n, fin, 0, unroll=min(GLA_UNROLL, n))

    st_ref[...] = jnp.zeros_like(st_ref)
    scan((cq_ref, ck_ref, cv_ref, cbf_ref, cbb_ref), Lc // C, ocf_ref, cr_ref, oc_ref)
    scan((q_ref, k_ref, v_ref, bf_ref, bb_ref), L // C, of_ref, r_ref, o_ref)


def _gla(qk, gv, gr, bf, bb, cqk, cgv, cgr, cbf, cbb, gain):
    B, L, _ = qk.shape
    Lc = cqk.shape[1]
    pair = 2 * GLA_DK
    pv = 2 * GLA_DV
    n_pair = GLA_HEADS // 2
    seq = lambda rows, n, off: pl.BlockSpec((1, rows, n), lambda b, p: (b, 0, p + off))
    const = lambda shape: pl.BlockSpec(shape, lambda b, p: (0,) * len(shape))
    est = (2 * (2 * L * pair * 2 + 2 * L * pv * 2 + 2 * L * pair * 4 + L * pv * 2)
           + 2 * L * pv * 4 + (8 << 20))
    return pl.pallas_call(
        _gla_kernel,
        grid=(B, n_pair),
        in_specs=[
            seq(L, pair, 0), seq(L, pair, n_pair), seq(L, pv, 0), seq(L, pv, 0), seq(L, pair, 0), seq(L, pair, 0),
            seq(Lc, pair, 0), seq(Lc, pair, n_pair), seq(Lc, pv, 0), seq(Lc, pv, 0), seq(Lc, pair, 0),
            seq(Lc, pair, 0),
            const((1, GLA_DV)),
        ],
        out_specs=[seq(L, pv, 0), seq(Lc, pv, 0)],
        out_shape=[jax.ShapeDtypeStruct((B, L, GLA_WIDTH), BF16), jax.ShapeDtypeStruct((B, Lc, GLA_WIDTH), BF16)],
        scratch_shapes=[pltpu.VMEM((2, L, pv), F32), pltpu.VMEM((2, Lc, pv), F32), pltpu.VMEM((2, pv, pair), F32)],
        compiler_params=_params(("parallel", "parallel"), est),
        name="gla",
    )(qk, qk, gv, gr, bf, bb, cqk, cqk, cgv, cgr, cbf, cbb, gain.reshape(1, GLA_DV))


def _rope_tables(L):
    pos = np.arange(L)
    half = GLA_DK // 4
    inv = ROPE_BASE ** (-np.arange(half, dtype=np.float64) / half)
    lane = np.arange(2 * GLA_DK)
    jj = lane % (GLA_DK // 2)
    use_col = (lane % GLA_DK) >= GLA_DK // 2
    p = np.where(use_col[None, :], (pos % GRID_W)[:, None], (pos // GRID_W)[:, None]).astype(np.float64)
    ang = p * inv[jj % half][None, :]
    first = jj < half
    cos = np.cos(ang)
    sin = np.where(first[None, :], -np.sin(ang), np.sin(ang))
    return jnp.asarray(cos, F32), jnp.asarray(sin, F32)


def _softmax_pv(s_parts, v_parts):
    m = s_parts[0].max(axis=-1, keepdims=True)
    for s in s_parts[1:]:
        m = jnp.maximum(m, s.max(axis=-1, keepdims=True))
    den = None
    acc = None
    for s, v in zip(s_parts, v_parts):
        p = jnp.exp(s - m)
        den = p.sum(axis=-1, keepdims=True) if den is None else den + p.sum(axis=-1, keepdims=True)
        pv = _dot(p.astype(BF16), v)
        acc = pv if acc is None else acc + pv
    return acc / den


def _na_window_start():
    cols = np.arange(GRID_W)
    return np.clip(cols - NA_WIN_W // 2, 0, GRID_W - NA_WIN_W)


def _na_kernel(q_ref, k_ref, v_ref, ck_ref, cv_ref, bias_ref, o_ref):
    W = GRID_W
    rows = k_ref.shape[1] // W
    n_loc = bias_ref.shape[2]
    kh = n_loc // W
    r0 = pl.program_id(1) * NA_ROWS_PER_STEP
    pair = 2 * NA_DH
    lane = lax.broadcasted_iota(jnp.int32, (1, pair), 1)
    head0 = lane < NA_DH
    scale = jnp.asarray(NA_DH ** -0.5, q_ref.dtype)

    n_pairs = NA_HEADS // 2
    lanes = [slice(p * pair, (p + 1) * pair) for p in range(n_pairs)]

    def rows_body(it, carry):
        units = []
        for j in range(NA_ROWS_PER_ITER):
            rr = it * NA_ROWS_PER_ITER + j
            r = r0 + rr
            rs = jnp.clip(r - kh // 2, 0, rows - kh)
            qs = pl.ds(pl.multiple_of(rr * W, W), W)
            ks = pl.ds(pl.multiple_of(rs * W, W), n_loc)
            units += [(qs, ks, rs - r + NA_WIN_H - 1, p) for p in range(n_pairs)]
        sts = []
        for qs, ks, dr, p in units:
            q = q_ref[0, qs, lanes[p]] * scale
            zero = jnp.zeros_like(q)
            q2 = jnp.concatenate([jnp.where(head0, q, zero), jnp.where(head0, zero, q)], axis=0)
            keys = jnp.concatenate([k_ref[0, ks, lanes[p]], ck_ref[0, :, lanes[p]]], axis=0)
            sts.append(_dot_nt(keys, q2))
        pts = []
        for (qs, ks, dr, p), st in zip(units, sts):
            st = jnp.concatenate([st[:n_loc] + bias_ref[p, dr], st[n_loc:]], axis=0)
            e = jnp.exp((st - st.max(axis=0, keepdims=True)).astype(BF16))
            den = e.astype(F32).sum(axis=0, keepdims=True)
            pts.append(e * (1.0 / den).astype(BF16))
        outs = []
        for (qs, ks, dr, p), pt in zip(units, pts):
            vals = jnp.concatenate([v_ref[0, ks, lanes[p]], cv_ref[0, :, lanes[p]]], axis=0)
            o2 = _dot_tn(pt, vals)
            outs.append(jnp.where(head0, o2[:W], o2[W:]))
        for j in range(NA_ROWS_PER_ITER):
            qs = units[j * n_pairs][0]
            o_ref[0, qs, :] = jnp.concatenate(outs[j * n_pairs:(j + 1) * n_pairs], axis=1).astype(o_ref.dtype)
        return carry

    lax.fori_loop(0, NA_ROWS_PER_STEP // NA_ROWS_PER_ITER, rows_body, 0)


def _na(nq, nk, nv, cnk, cnv, bias):
    B, L, n = nq.shape
    Lc = cnk.shape[1]
    tq = NA_ROWS_PER_STEP * GRID_W
    full = lambda rows: pl.BlockSpec((1, rows, n), lambda b, i: (b, 0, 0))
    est = 2 * (2 * L * n * 2 + 2 * Lc * n * 2 + bias.size * 4 + 2 * tq * n * 2) + (8 << 20)
    return pl.pallas_call(
        _na_kernel,
        grid=(B, L // tq),
        in_specs=[
            pl.BlockSpec((1, tq, n), lambda b, i: (b, i, 0)),
            full(L), full(L), full(Lc), full(Lc),
            pl.BlockSpec(bias.shape, lambda b, i: (0, 0, 0, 0)),
        ],
        out_specs=pl.BlockSpec((1, tq, n), lambda b, i: (b, i, 0)),
        out_shape=jax.ShapeDtypeStruct((B, L, n), BF16),
        compiler_params=_params(("parallel", "parallel"), est),
        name="neighbourhood_attention",
    )(nq, nk, nv, cnk, cnv, bias)


def _na_bias_table(rpb, rows):
    kh = min(NA_WIN_H, rows)
    start = _na_window_start()
    kc = np.arange(GRID_W)
    inside = (kc[None, :] >= start[:, None]) & (kc[None, :] < start[:, None] + NA_WIN_W)
    sel = np.zeros((2 * NA_WIN_W - 1, GRID_W, GRID_W), np.float32)
    qq, kk = np.nonzero(inside)
    sel[kk - qq + NA_WIN_W - 1, qq, kk] = 1.0
    by_row = jnp.stack([rpb[:, d:d + kh, :] for d in range(NA_WIN_H)], axis=1)
    t = jnp.einsum('hdic,cqk->hdikq', by_row, jnp.asarray(sel), precision=lax.Precision.HIGHEST)
    t = t + jnp.asarray(np.where(inside, 0.0, MASK_VALUE).T, F32)[None, None, None, :, :]
    t = t.reshape(NA_HEADS // 2, 2, NA_WIN_H, kh * GRID_W, GRID_W)
    return jnp.transpose(t, (0, 2, 3, 1, 4)).reshape(NA_HEADS // 2, NA_WIN_H, kh * GRID_W, 2 * GRID_W)


def _ctx_attn_kernel(q_ref, k_ref, v_ref, o_ref):
    pair = 2 * NA_DH
    lane = lax.broadcasted_iota(jnp.int32, (1, pair), 1)
    head0 = lane < NA_DH
    Lc = q_ref.shape[1]
    outs = []
    for p in range(NA_HEADS // 2):
        ls = slice(p * pair, (p + 1) * pair)
        q = q_ref[0, :, ls]
        zero = jnp.zeros_like(q)
        q2 = jnp.concatenate([jnp.where(head0, q, zero), jnp.where(head0, zero, q)], axis=0)
        s = _dot_nt(q2, k_ref[0, :, ls]) * (NA_DH ** -0.5)
        o2 = _softmax_pv([s], [v_ref[0, :, ls]])
        outs.append(jnp.where(head0, o2[:Lc], o2[Lc:]))
    o_ref[0] = jnp.concatenate(outs, axis=1).astype(o_ref.dtype)


def _ctx_attn(cnq, cnk, cnv):
    B, Lc, n = cnq.shape
    spec = pl.BlockSpec((1, Lc, n), lambda b: (b, 0, 0))
    return pl.pallas_call(
        _ctx_attn_kernel,
        grid=(B,),
        in_specs=[spec, spec, spec],
        out_specs=spec,
        out_shape=jax.ShapeDtypeStruct((B, Lc, n), BF16),
        compiler_params=_params(("parallel",), 8 * Lc * n * 2 + (8 << 20)),
        name="context_attention",
    )(cnq, cnk, cnv)


def _pack_bf16_pairs(x):
    n = x.shape[1] // 2
    lo = lax.bitcast_convert_type(x[:, :n].astype(BF16).astype(F32), jnp.int32)
    hi = lax.bitcast_convert_type(x[:, n:].astype(BF16).astype(F32), jnp.int32)
    return lax.shift_right_logical(lo, 16) | (hi & jnp.int32(-65536))


def _unpack_bf16_pairs(p):
    lo = lax.bitcast_convert_type(lax.shift_left(p, 16), F32)
    hi = lax.bitcast_convert_type(p & jnp.int32(-65536), F32)
    return jnp.concatenate([lo, hi], axis=1)


def _store_packed(refs, x):
    n = 2 * SC_ROW_WORDS
    for p, ref in enumerate(refs):
        ref[...] = _pack_bf16_pairs(x[:, p * n:(p + 1) * n]).reshape(ref.shape)


def _load_packed(refs):
    return jnp.concatenate([_unpack_bf16_pairs(ref[...].reshape(ref.shape[-2:])) for ref in refs], axis=1)


def _route(logits):
    lane = lax.broadcasted_iota(jnp.int32, logits.shape, 1)
    big = jnp.int32(logits.shape[1])
    t1 = logits.max(axis=-1, keepdims=True)
    i1 = jnp.where(logits == t1, lane, big).min(axis=-1, keepdims=True)
    rest = jnp.where(lane == i1, -jnp.inf, logits)
    t2 = rest.max(axis=-1, keepdims=True)
    i2 = jnp.where(rest == t2, lane, big).min(axis=-1, keepdims=True)
    e2 = jnp.exp(t2 - t1)
    return i1, i2, 1.0 / (1.0 + e2), e2 / (1.0 + e2)


def _outproj_kernel(*refs, dense_ffn):
    ga_ref, na_ref, x_ref, mod_ref, wa_ref, wb_ref, gp_ref, gf_ref = refs[:8]
    m = mod_ref[0]
    y = _dot(ga_ref[0], wa_ref[...]) + _dot(na_ref[0], wb_ref[...])
    x1 = x_ref[0] + m[2:3] * (_rms(y) * gp_ref[...])
    h2 = _rms(x1) * gf_ref[...] * (1.0 + m[4:5]) + m[3:4]
    if dense_ffn:
        wg_ref, wu_ref, wd_ref, g2_ref, o_ref = refs[8:]
        h = h2.astype(BF16)
        hid = (_silu(_dot(h, wg_ref[...])) * _dot(h, wu_ref[...])).astype(BF16)
        o_ref[0] = x1 + m[5:6] * (_rms(_dot(hid, wd_ref[...])) * g2_ref[...])
    else:
        wr_ref, x1_ref, gw_ref, eid_ref, cnt_ref, *h2_refs = refs[8:]
        x1_ref[0] = x1
        _store_packed(h2_refs, h2)
        tm = h2.shape[0]
        lanes = wr_ref.shape[1] // 2
        prod = _dot(jnp.concatenate(_split(h2), axis=0), wr_ref[...])
        logits = (prod[:tm, :lanes] + prod[tm:, :lanes]) + (prod[:tm, lanes:] + prod[tm:, lanes:])
        lane = lax.broadcasted_iota(jnp.int32, (tm, lanes), 1)
        logits = jnp.where(lane < N_EXPERTS, logits, -jnp.inf)
        i1, i2, w1, w2 = _route(logits)
        gw_ref[0] = jnp.where(lane == 0, w1, jnp.where(lane == 1, w2, 0.0))
        eid_ref[0] = jnp.where(lane == 0, i1, jnp.where(lane == 1, i2, 0))
        chosen = jnp.where(lane == i1, 1.0, jnp.where(lane == i2, 1.0, 0.0))
        cnt_ref[0] = jnp.broadcast_to(chosen.sum(axis=0, keepdims=True), cnt_ref.shape[1:])


def _outproj(ga, na, x, mod, mod_row0, w_out, g_post, g_ffn, *, ffn=None, w_router=None):
    nb, rows, d = x.shape
    tm = min(TOKEN_TILE, rows)
    dense_ffn = ffn is not None
    tok = lambda n: pl.BlockSpec((1, tm, n), lambda b, i: (b, i, 0))
    const = lambda shape: pl.BlockSpec(shape, lambda b, i: (0,) * len(shape))
    once = lambda shape: pl.BlockSpec(shape, lambda b, i: (0,) * len(shape), pipeline_mode=pl.Buffered(1))
    wa = w_out[:GLA_WIDTH].astype(BF16)
    wb = w_out[GLA_WIDTH:].astype(BF16)
    args = [ga, na, x, mod, wa, wb, g_post.reshape(1, d), g_ffn.reshape(1, d)]
    in_specs = [tok(GLA_WIDTH), tok(NA_WIDTH), tok(d),
                pl.BlockSpec((1, 6, d), lambda b, i: (b + mod_row0, 0, 0)),
                const(wa.shape), const(wb.shape), const((1, d)), const((1, d))]
    est = 2 * (tm * d * (4 + 4 + 2) + 2 * tm * GLA_WIDTH * 2 + d * d * 2) + 4 * tm * d * 4
    if dense_ffn:
        wg, wu, wd, g2 = ffn
        dff = wg.shape[1]
        args += [wg, wu, wd, g2.reshape(1, d)]
        in_specs += [once(wg.shape), once(wu.shape), once(wd.shape), const((1, d))]
        out_specs = tok(d)
        out_shape = jax.ShapeDtypeStruct((nb, rows, d), F32)
        est += 3 * d * dff * 2 + 3 * tm * dff * 4
    else:
        nt = rows // tm
        wr = jnp.zeros((d, V7X_LANES), F32).at[:, :N_EXPERTS].set(w_router)
        wr = jnp.concatenate(_split(wr), axis=1)
        args.append(wr)
        in_specs.append(const(wr.shape))
        n_parts = d // (2 * SC_ROW_WORDS)
        out_specs = [tok(d), tok(V7X_LANES), tok(V7X_LANES),
                     pl.BlockSpec((1, 8, V7X_LANES), lambda b, i: (b * nt + i, 0, 0))] + [tok(SC_ROW_WORDS)] * n_parts
        out_shape = [jax.ShapeDtypeStruct((nb, rows, d), F32),
                     jax.ShapeDtypeStruct((nb, rows, V7X_LANES), F32),
                     jax.ShapeDtypeStruct((nb, rows, V7X_LANES), jnp.int32),
                     jax.ShapeDtypeStruct((nb * nt, 8, V7X_LANES), F32)]
        out_shape += [jax.ShapeDtypeStruct((nb, rows, SC_ROW_WORDS), jnp.int32)] * n_parts
    return pl.pallas_call(
        functools.partial(_outproj_kernel, dense_ffn=dense_ffn),
        grid=(nb, rows // tm),
        in_specs=in_specs,
        out_specs=out_specs,
        out_shape=out_shape,
        compiler_params=_params(("parallel", "parallel"), est),
        name="outproj_ffn" if dense_ffn else "outproj_router",
    )(*args)


def _slot_kernel(eid_ref, base_ref, pos_ref):
    eid = eid_ref[...]
    tm, lanes = eid.shape
    lane = lax.broadcasted_iota(jnp.int32, (tm, lanes), 1)
    i1 = eid[:, 0:1]
    i2 = eid[:, 1:2]
    chosen = jnp.where(lane == i1, 1.0, jnp.where(lane == i2, 1.0, 0.0)).astype(BF16)
    row = lax.broadcasted_iota(jnp.int32, (tm, tm), 0)
    col = lax.broadcasted_iota(jnp.int32, (tm, tm), 1)
    incl = jnp.where(row >= col, 1.0, 0.0).astype(BF16)
    slot = base_ref[0][0:1] + _dot(incl, chosen) - 1.0
    p1 = jnp.where(lane == i1, slot, 0.0).sum(axis=-1, keepdims=True).astype(jnp.int32)
    p2 = jnp.where(lane == i2, slot, 0.0).sum(axis=-1, keepdims=True).astype(jnp.int32)
    pos_ref[...] = jnp.where(lane == 0, p1, jnp.where(lane == 1, p2, 0))


def _slots(eid, tile_base, tm):
    t, lanes = eid.shape
    return pl.pallas_call(
        _slot_kernel,
        grid=(t // tm,),
        in_specs=[pl.BlockSpec((tm, lanes), lambda i: (i, 0)),
                  pl.BlockSpec((1, 8, lanes), lambda i: (i, 0, 0))],
        out_specs=pl.BlockSpec((tm, lanes), lambda i: (i, 0)),
        out_shape=jax.ShapeDtypeStruct((t, lanes), jnp.int32),
        compiler_params=_params(("parallel",), 8 * tm * lanes * 4 + 4 * tm * tm),
        name="moe_slots",
    )(eid, tile_base)


def _route_plan(cnt, tm_tokens):
    counts = cnt[:, 0, :N_EXPERTS].astype(jnp.int32)
    total = counts.sum(axis=0)
    padded = -(-total // MOE_ROW_TILE) * MOE_ROW_TILE
    start = jnp.cumsum(padded) - padded
    before = jnp.cumsum(counts, axis=0) - counts
    tile_base = (start[None, :] + before).astype(F32)
    tile_base = jnp.zeros((cnt.shape[0], 8, V7X_LANES), F32).at[:, :, :N_EXPERTS].set(tile_base[:, None, :])
    n_slots = tm_tokens * 2 + N_EXPERTS * MOE_ROW_TILE
    first_row = jnp.arange(n_slots // MOE_ROW_TILE, dtype=jnp.int32) * MOE_ROW_TILE
    tile_expert = jnp.minimum((first_row[:, None] >= (start + padded)[None, :]).sum(axis=1), N_EXPERTS - 1)
    n_valid = jnp.clip(start[tile_expert] + total[tile_expert] - first_row, 0, MOE_ROW_TILE)
    return tile_base, tile_expert.astype(jnp.int32), n_valid.astype(jnp.int32), n_slots


def _sc_mesh():
    return plsc.VectorSubcoreMesh(core_axis_name="core", subcore_axis_name="subcore")


def _scatter_rows(x, idx, n_out):
    t, w = x.shape
    n = idx.shape[0]
    win = SC_ROW_WINDOW
    n_blk = t // win

    @functools.partial(pl.kernel, out_type=jax.ShapeDtypeStruct((n_out, w), x.dtype), mesh=_sc_mesh(),
                       scratch_types=[], name="moe_dispatch")
    def scatter(x_hbm, i_hbm, o_hbm):
        def body(x_vmem, i_vmem):
            pltpu.sync_copy(x_vmem, o_hbm.at[i_vmem.at[0]])

        pltpu.emit_pipeline(
            body,
            grid=(n // win,),
            in_specs=[pl.BlockSpec((win, w), lambda i: (i % n_blk, 0)),
                      pl.BlockSpec((1, win), lambda i: (0, i))],
            out_specs=[],
            core_axis_name=("core", "subcore"),
            dimension_semantics=(pltpu.PARALLEL,),
        )(x_hbm, i_hbm)

    return scatter(x, idx.reshape(1, n))


def _gather_rows(x, idx):
    n = idx.shape[0]
    w = x.shape[1]
    win = SC_ROW_WINDOW

    @functools.partial(pl.kernel, out_type=jax.ShapeDtypeStruct((n, w), x.dtype), mesh=_sc_mesh(),
                       scratch_types=[], name="moe_combine_gather")
    def gather(x_hbm, i_hbm, o_hbm):
        def body(i_vmem, o_vmem):
            pltpu.sync_copy(x_hbm.at[i_vmem.at[0]], o_vmem)

        pltpu.emit_pipeline(
            body,
            grid=(n // win,),
            in_specs=[pl.BlockSpec((1, win), lambda i: (0, i))],
            out_specs=[pl.BlockSpec((win, w), lambda i: (i, 0))],
            core_axis_name=("core", "subcore"),
            dimension_semantics=(pltpu.PARALLEL,),
        )(i_hbm, o_hbm)

    return gather(x, idx.reshape(1, n))


def _experts_kernel(te_ref, nv_ref, *refs, n_parts):
    x_refs = refs[:n_parts]
    wg_ref, wu_ref, wd_ref = refs[n_parts:n_parts + 3]
    y_refs = refs[n_parts + 3:2 * n_parts + 3]
    i = pl.program_id(0)
    n_valid = nv_ref[i]
    tm = x_refs[0].shape[0]
    dff = wg_ref.shape[2]

    @pl.when(n_valid > 0)
    def _():
        row = lax.broadcasted_iota(jnp.int32, (tm, 1), 0)
        x = jnp.where(row < n_valid, _load_packed(x_refs), 0.0).astype(BF16)
        acc = None
        for lo in range(0, dff, MOE_FF_CHUNK):
            sl = slice(lo, lo + MOE_FF_CHUNK)
            hid = (_silu(_dot(x, wg_ref[0, :, sl])) * _dot(x, wu_ref[0, :, sl])).astype(BF16)
            part = _dot(hid, wd_ref[0, sl, :])
            acc = part if acc is None else acc + part
        _store_packed(y_refs, acc)

    @pl.when(n_valid == 0)
    def _():
        for ref in y_refs:
            ref[...] = jnp.zeros_like(ref)


def _experts(xs_parts, tile_expert, n_valid, wg, wu, wd):
    n_parts = len(xs_parts)
    n_slots, words = xs_parts[0].shape
    n_e, d, dff = wg.shape
    tm = MOE_ROW_TILE
    rows_spec = pl.BlockSpec((tm, words), lambda i, te, nv: (i, 0))
    weights = lambda shape: pl.BlockSpec((1,) + shape, lambda i, te, nv: (te[i], 0, 0))
    est = 2 * 3 * d * dff * 2 + 4 * tm * d * 2 + 3 * tm * MOE_FF_CHUNK * 4 + 3 * tm * d * 4
    grid_spec = pltpu.PrefetchScalarGridSpec(
        num_scalar_prefetch=2,
        grid=(n_slots // tm,),
        in_specs=[rows_spec] * n_parts + [weights((d, dff)), weights((d, dff)), weights((dff, d))],
        out_specs=[rows_spec] * n_parts,
    )
    return pl.pallas_call(
        functools.partial(_experts_kernel, n_parts=n_parts),
        grid_spec=grid_spec,
        out_shape=[jax.ShapeDtypeStruct((n_slots, words), jnp.int32)] * n_parts,
        compiler_params=_params(("arbitrary",), est),
        name="moe_experts",
    )(tile_expert, n_valid, *xs_parts, wg, wu, wd)


def _combine_kernel(*refs, n_parts):
    y1_refs = refs[:n_parts]
    y2_refs = refs[n_parts:2 * n_parts]
    gw_ref, x_ref, mod_ref, g_ref, o_ref = refs[2 * n_parts:]
    gw = gw_ref[0]
    y = gw[:, 0:1] * _load_packed(y1_refs) + gw[:, 1:2] * _load_packed(y2_refs)
    o_ref[0] = x_ref[0] + mod_ref[0][5:6] * (_rms(y) * g_ref[...])


def _combine(ys2_parts, gw, x1, mod, mod_row0, g_post):
    n_parts = len(ys2_parts)
    nb, rows, d = x1.shape
    words = ys2_parts[0].shape[-1]
    tm = min(TOKEN_TILE, rows)
    tok = lambda n: pl.BlockSpec((1, tm, n), lambda b, i: (b, i, 0))
    ysp = lambda k: pl.BlockSpec((1, tm, words), lambda b, i: (k * nb + b, i, 0))
    est = 2 * tm * (d * 2 * 4 + d * 4 + V7X_LANES * 4) + 3 * tm * d * 4
    return pl.pallas_call(
        functools.partial(_combine_kernel, n_parts=n_parts),
        grid=(nb, rows // tm),
        in_specs=[ysp(0)] * n_parts + [ysp(1)] * n_parts + [
            tok(V7X_LANES), tok(d),
            pl.BlockSpec((1, 6, d), lambda b, i: (b + mod_row0, 0, 0)),
            pl.BlockSpec((1, d), lambda b, i: (0, 0))],
        out_specs=tok(d),
        out_shape=jax.ShapeDtypeStruct((nb, rows, d), F32),
        compiler_params=_params(("parallel", "parallel"), est),
        name="moe_combine",
    )(*ys2_parts, *ys2_parts, gw, x1, mod, g_post.reshape(1, d))


def _moe_routed(gw, eid, cnt, *h2_parts, x1, mod, mod_row0, experts, g_post):
    nb, rows, d = x1.shape
    t = nb * rows
    tm = min(TOKEN_TILE, rows)
    tile_base, tile_expert, n_valid, n_slots = _route_plan(cnt, t)
    pos = _slots(eid.reshape(t, V7X_LANES), tile_base, tm)
    idx = jnp.concatenate([pos[:, 0], pos[:, 1]])
    xs = [_scatter_rows(h.reshape(t, h.shape[-1]), idx, n_slots) for h in h2_parts]
    ys = _experts(xs, tile_expert, n_valid, *experts)
    ys2 = [_gather_rows(y, idx).reshape(2 * nb, rows, y.shape[-1]) for y in ys]
    return _combine(ys2, gw, x1, mod, mod_row0, g_post)


def _cast_kernel(w_ref, o_ref):
    o_ref[...] = w_ref[...].astype(o_ref.dtype)


def _to_bf16(w):
    shape = w.shape
    w3 = w.reshape((-1,) + shape[-2:])
    n, r, c = w3.shape
    rb = r
    while rb * c * 4 > CAST_BLOCK_BYTES and rb % 16 == 0:
        rb //= 2
    spec = pl.BlockSpec((1, rb, c), lambda e, i: (e, i, 0))
    out = pl.pallas_call(
        _cast_kernel,
        grid=(n, r // rb),
        in_specs=[spec],
        out_specs=spec,
        out_shape=jax.ShapeDtypeStruct(w3.shape, BF16),
        compiler_params=_params(("parallel", "parallel"), 2 * rb * c * 6),
        name="cast_bf16",
    )(w3)
    return out.reshape(shape)
def _pack_w_in(w):
    a = 2 * GLA_KDIM + 2 * GLA_WIDTH
    lr = 2 * GLA_GATE_RANK
    pad = jnp.zeros((w.shape[0], LR_PAD - lr), w.dtype)
    return jnp.concatenate([w[:, :a], w[:, a + lr:], w[:, a:a + lr], pad], axis=1).astype(BF16)


def _pack_gate(w_gate, b_gate):
    wgp = jnp.zeros((2, LR_PAD, GLA_KDIM), F32)
    for d in range(2):
        wgp = wgp.at[d, d * GLA_GATE_RANK:(d + 1) * GLA_GATE_RANK].set(w_gate[d])
    return wgp, b_gate.reshape(2, 1, GLA_KDIM)


def kernel(x, c, ctx, c_ctx, w_ada, b_ada, g_pre_mix, g_post_mix, g_pre_ffn, g_post_ffn, w_in,
           gla_w_gate, gla_b_gate, gla_g_norm, na_rpb, w_out, ffn_w_gate, ffn_w_up, ffn_w_down,
           moe_w_router, moe_w_gate, moe_w_up, moe_w_down):
    B, L, D = x.shape
    Lc = ctx.shape[1]
    depth = w_ada.shape[0]
    rows = L // GRID_W
    ctx_row = B
    n_cond = -(-(B + 1) // 8) * 8
    cond = jnp.zeros((n_cond, D), F32).at[:B].set(c).at[B].set(c_ctx)
    mod = _modulation(cond, w_ada, b_ada)
    tables = _rope_tables(L)
    ctx = ctx.reshape(1, B * Lc, D)

    for i in range(depth):
        last = i == depth - 1
        j = i // 2
        w = _pack_w_in(w_in[i])
        wgp, bgp = _pack_gate(gla_w_gate[i], gla_b_gate[i])
        qk, gv, gr, nq, nk, nv, bf, bb = _inproj(x, mod[i], 0, g_pre_mix[i], w, wgp, bgp, tables)
        cparts = _inproj(ctx, mod[i], ctx_row, g_pre_mix[i], w, wgp, bgp)
        cqk, cgv, cgr, cnq, cnk, cnv, cbf, cbb = [t.reshape(B, Lc, t.shape[-1]) for t in cparts]
        ga, gac = _gla(qk, gv, gr, bf, bb, cqk, cgv, cgr, cbf, cbb, gla_g_norm[i])
        na = _na(nq, nk, nv, cnk, cnv, _na_bias_table(na_rpb[i], rows))
        if i % 2 == 0:
            ffn = (_to_bf16(ffn_w_gate[j]), _to_bf16(ffn_w_up[j]), _to_bf16(ffn_w_down[j]), g_post_ffn[i])

            def mix_and_ffn(ga_, na_, x_, row0):
                return _outproj(ga_, na_, x_, mod[i], row0, w_out[i], g_post_mix[i], g_pre_ffn[i], ffn=ffn)
        else:
            experts = (_to_bf16(moe_w_gate[j]), _to_bf16(moe_w_up[j]), _to_bf16(moe_w_down[j]))

            def mix_and_ffn(ga_, na_, x_, row0):
                x1, *routed = _outproj(ga_, na_, x_, mod[i], row0, w_out[i], g_post_mix[i], g_pre_ffn[i],
                                       w_router=moe_w_router[j])
                return _moe_routed(*routed, x1=x1, mod=mod[i], mod_row0=row0, experts=experts, g_post=g_post_ffn[i])
        x = mix_and_ffn(ga, na, x, 0)
        if not last:
            nac = _ctx_attn(cnq, cnk, cnv)
            ctx = mix_and_ffn(gac.reshape(1, B * Lc, GLA_WIDTH), nac.reshape(1, B * Lc, NA_WIDTH), ctx, ctx_row)
    return x
```

```python
import functools

import numpy as np
import jax
import jax.numpy as jnp
from jax import lax
from jax.experimental import pallas as pl
from jax.experimental.pallas import tpu as pltpu
from jax.experimental.pallas import tpu_sc as plsc

F32 = jnp.float32
BF16 = jnp.bfloat16

GRID_W = 64
GLA_HEADS = 4
GLA_DV = 128
GLA_DK = 64
GLA_KDIM = GLA_HEADS * GLA_DK
GLA_WIDTH = GLA_HEADS * GLA_DV
GLA_GATE_RANK = 16
GLA_GATE_NORM = 16.0
NA_HEADS = 8
NA_DH = 64
NA_WIDTH = NA_HEADS * NA_DH
NA_WIN_H = 8
NA_WIN_W = 16
ROPE_BASE = 10000.0
N_EXPERTS = 8
EPS = 1e-6

V7X_LANES = 128
V7X_VMEM_BYTES = 64 * 1024 * 1024
V7X_VMEM_USABLE = V7X_VMEM_BYTES - 8 * 1024 * 1024

TOKEN_TILE = 512
GLA_CHUNK = 128
GLA_UNROLL = 4
NA_ROWS_PER_STEP = 8
NA_ROWS_PER_ITER = 2
MASK_VALUE = -1e30
MOE_FF_CHUNK = 1792
MOE_ROW_TILE = 512
SC_ROW_WINDOW = 128
CAST_BLOCK_BYTES = 4 << 20
SC_ROW_WORDS = 256
LR_PAD = V7X_LANES


def _vmem_limit(estimate_bytes):
    return int(min(V7X_VMEM_USABLE, estimate_bytes * 5 // 4 + (4 << 20)))


def _params(semantics, vmem_estimate):
    return pltpu.CompilerParams(dimension_semantics=semantics, vmem_limit_bytes=_vmem_limit(vmem_estimate))


def _dot(a, b):
    return jnp.dot(a, b, preferred_element_type=F32)


def _dot_nt(a, b):
    return lax.dot_general(a, b, (((1,), (1,)), ((), ())), preferred_element_type=F32)


def _dot_tn(a, b):
    return lax.dot_general(a, b, (((0,), (0,)), ((), ())), preferred_element_type=F32)


def _split(x):
    hi = x.astype(BF16)
    lo = (x - hi.astype(F32)).astype(BF16)
    return hi, lo


def _dot3(a, b):
    ah, al = _split(a)
    bh, bl = _split(b)
    return _dot(ah, bh) + (_dot(al, bh) + _dot(ah, bl))


def _sigmoid(x):
    return 1.0 / (1.0 + jnp.exp(-x))


def _silu(x):
    return x * _sigmoid(x)


def _rms(x):
    return x * lax.rsqrt(jnp.mean(x * x, axis=-1, keepdims=True) + EPS)


def _mod_kernel(c_ref, w_ref, b_ref, o_ref):
    o_ref[0] = _dot3(_silu(c_ref[...]), w_ref[0]) + b_ref[0]


def _modulation(cond, w_ada, b_ada):
    depth, d, n = w_ada.shape
    rows = cond.shape[0]
    tn = 512
    out = pl.pallas_call(
        _mod_kernel,
        grid=(depth, n // tn),
        in_specs=[
            pl.BlockSpec((rows, d), lambda l, j: (0, 0)),
            pl.BlockSpec((1, d, tn), lambda l, j: (l, 0, j)),
            pl.BlockSpec((1, 1, tn), lambda l, j: (l, 0, j)),
        ],
        out_specs=pl.BlockSpec((1, rows, tn), lambda l, j: (l, 0, j)),
        out_shape=jax.ShapeDtypeStruct((depth, rows, n), F32),
        compiler_params=_params(("parallel", "parallel"), 3 * d * tn * 4 * 2),
        name="modulation",
    )(cond, w_ada, b_ada.reshape(depth, 1, n))
    return out.reshape(depth, rows, 6, d)


def _log_decay(logit):
    return (jnp.minimum(logit, 0.0) - jnp.log(1.0 + jnp.exp(-jnp.abs(logit)))) * (1.0 / GLA_GATE_NORM)


def _inproj_kernel(*refs, rope):
    if rope:
        x_ref, mod_ref, g_ref, w_ref, wg_ref, bg_ref, cos_ref, sin_ref = refs[:8]
    else:
        x_ref, mod_ref, g_ref, w_ref, wg_ref, bg_ref = refs[:6]
    qk_ref, gv_ref, gr_ref, nq_ref, nk_ref, nv_ref, bf_ref, bb_ref = refs[-8:]
    m = mod_ref[0]
    h = (_rms(x_ref[0]) * g_ref[...] * (1.0 + m[1:2]) + m[0:1]).astype(BF16)
    tm = h.shape[0]
    plain_refs = (gv_ref, gr_ref, nq_ref, nk_ref, nv_ref)
    offs = np.cumsum([0, qk_ref.shape[-1]] + [r.shape[-1] for r in plain_refs])

    def project(k):
        ref = plain_refs[k]
        ref[0] = _dot(h, w_ref[:, offs[k + 1]:offs[k + 2]]).astype(ref.dtype)

    lr = _dot(h, w_ref[:, offs[-1]:offs[-1] + LR_PAD])
    logits = [_dot3(lr, wg_ref[d]) + bg_ref[d] for d in range(2)]
    project(0)
    C = GLA_CHUNK
    row = lax.broadcasted_iota(jnp.int32, (C, C), 0)
    colm = lax.broadcasted_iota(jnp.int32, (C, C), 1)
    for d, out_ref in enumerate((bf_ref, bb_ref)):
        gh, gl = _split(_log_decay(logits[d]))
        project(1 + 2 * d)
        tri = (row >= colm) if d == 0 else (row <= colm)
        tmat = jnp.where(tri, 1.0, 0.0).astype(BF16)
        for c in range(tm // C):
            sl = slice(c * C, (c + 1) * C)
            out_ref[0, sl, :] = _dot(tmat, gh[sl]) + _dot(tmat, gl[sl])
        project(2 + 2 * d)
    n = qk_ref.shape[-1]
    qk = _dot(h, w_ref[:, 0:n])
    lane = lax.broadcasted_iota(jnp.int32, (1, n), 1)
    if rope:
        reps = n // cos_ref.shape[-1]
        cos = jnp.concatenate([cos_ref[...]] * reps, axis=1)
        sin = jnp.concatenate([sin_ref[...]] * reps, axis=1)
        quarter = GLA_DK // 4
        first = (lane % (2 * quarter)) < quarter
        partner = jnp.where(first, pltpu.roll(qk, n - quarter, 1), pltpu.roll(qk, quarter, 1))
        qk = qk * cos + partner * sin
    qk_ref[0] = jnp.where(lane < GLA_KDIM, qk * (GLA_DK ** -0.5), qk).astype(qk_ref.dtype)


def _inproj(x, mod, mod_row0, g, w, wgp, bgp, rope_tables=None):
    nb, rows, d = x.shape
    tm = min(TOKEN_TILE, rows)
    rope = rope_tables is not None
    widths = (2 * GLA_KDIM, GLA_WIDTH, GLA_WIDTH, NA_WIDTH, NA_WIDTH, NA_WIDTH, GLA_KDIM, GLA_KDIM)
    dtypes = (BF16,) * 6 + (F32, F32)
    tok = lambda n: pl.BlockSpec((1, tm, n), lambda b, i: (b, i, 0))
    const = lambda shape: pl.BlockSpec(shape, lambda b, i: (0,) * len(shape))
    args = [x, mod, g.reshape(1, d), w, wgp, bgp]
    in_specs = [tok(d), pl.BlockSpec((1, 6, d), lambda b, i: (b + mod_row0, 0, 0)), const((1, d)),
                const(w.shape), const(wgp.shape), const(bgp.shape)]
    if rope:
        args += list(rope_tables)
        in_specs += [pl.BlockSpec((tm, t.shape[1]), lambda b, i: (i, 0)) for t in rope_tables]
    est = 2 * (tm * d * 4 + d * w.shape[1] * 2 + sum(widths) * tm * 4) + tm * d * 8
    return pl.pallas_call(
        functools.partial(_inproj_kernel, rope=rope),
        grid=(nb, rows // tm),
        in_specs=in_specs,
        out_specs=[tok(n) for n in widths],
        out_shape=[jax.ShapeDtypeStruct((nb, rows, n), dt) for n, dt in zip(widths, dtypes)],
        compiler_params=_params(("parallel", "parallel"), est),
        name="inproj",
    )(*args)


def _gla_kernel(q_ref, k_ref, v_ref, r_ref, bf_ref, bb_ref, cq_ref, ck_ref, cv_ref, cr_ref, cbf_ref, cbb_ref,
                gain_ref, o_ref, oc_ref, of_ref, ocf_ref, st_ref):
    C = GLA_CHUNK
    L = q_ref.shape[1]
    Lc = cq_ref.shape[1]
    row = lax.broadcasted_iota(jnp.int32, (C, C), 0)
    col = lax.broadcasted_iota(jnp.int32, (C, C), 1)
    lane = lax.broadcasted_iota(jnp.int32, (1, 2 * GLA_DK), 1)
    head0 = lane < GLA_DK
    st_row = lax.broadcasted_iota(jnp.int32, (2 * GLA_DV, 2 * GLA_DK), 0) // GLA_DV
    st_col = lax.broadcasted_iota(jnp.int32, (2 * GLA_DV, 2 * GLA_DK), 1) // GLA_DK
    blockdiag = st_row == st_col
    gain = gain_ref[...]

    def chunk_steps(refs, chunks, acc_ref):
        rq, rk, rv = refs[:3]
        dirs = (0, 1)
        sls = [pl.ds(pl.multiple_of(i * C, C), C) for i in chunks]
        tris = [row >= col, row <= col]
        vs = [rv[0, sl, :] for sl in sls]
        qfs = [rq[0, sl, :].astype(F32) for sl in sls]
        kfs = [rk[0, sl, :].astype(F32) for sl in sls]
        bs = [refs[3 + d][0, sls[d], :] for d in dirs]
        b_mids = [b[C // 2:C // 2 + 1] for b in bs]
        b_edges = [bs[0][C - 1:C], bs[1][0:1]]
        qes = [(qfs[d] * jnp.exp(bs[d] - b_mids[d])).astype(BF16) for d in dirs]
        kes = [(kfs[d] * jnp.exp(b_mids[d] - bs[d])).astype(BF16) for d in dirs]
        zero = jnp.zeros_like(qes[0])
        lhss = [jnp.concatenate([jnp.where(head0, qe, zero), jnp.where(head0, zero, qe)], axis=0) for qe in qes]
        scores = [_dot_nt(lhss[d], kes[d]) for d in dirs]
        qbs = [(qfs[d] * jnp.exp(bs[d])).astype(BF16) for d in dirs]
        kds = [(kfs[d] * jnp.exp(b_edges[d] - bs[d])).astype(BF16) for d in dirs]
        sts = [st_ref[d] for d in dirs]
        inter = [_dot_nt(qbs[d], sts[d].astype(BF16)) for d in dirs]
        dss = [_dot_tn(vs[d], kds[d]) for d in dirs]
        ams = [jnp.where(jnp.concatenate([tris[d], tris[d]], axis=0), scores[d], 0.0).astype(BF16) for d in dirs]
        intra = [jnp.concatenate([_dot(ams[d][:C], vs[d][:, :GLA_DV]), _dot(ams[d][C:], vs[d][:, GLA_DV:])], axis=1)
                 for d in dirs]
        for d in dirs:
            st_ref[d] = sts[d] * jnp.exp(b_edges[d]) + jnp.where(blockdiag, dss[d], 0.0)
            acc_ref[d, sls[d], :] = intra[d] + inter[d]

    def finish(total, r):
        outs = []
        for h in range(2):
            oh = _rms(total[:, h * GLA_DV:(h + 1) * GLA_DV]) * gain
            outs.append(oh * _silu(r[:, h * GLA_DV:(h + 1) * GLA_DV].astype(F32)))
        return jnp.concatenate(outs, axis=1)

    def scan(refs, n, acc_ref, r_ref_, out_ref):
        def body(j, carry):
            chunk_steps(refs, (j, n - 1 - j), acc_ref)
            return carry

        lax.fori_loop(0, n, body, 0, unroll=min(GLA_UNROLL, n))

        def fin(i, carry):
            sl = pl.ds(pl.multiple_of(i * C, C), C)
            out_ref[0, sl, :] = finish(acc_ref[0, sl, :] + acc_ref[1, sl, :], r_ref_[0, sl, :]).astype(out_ref.dtype)
            return carry

        lax.fori_loop(0, n, fin, 0, unroll=min(GLA_UNROLL, n))

    st_ref[...] = jnp.zeros_like(st_ref)
    scan((cq_ref, ck_ref, cv_ref, cbf_ref, cbb_ref), Lc // C, ocf_ref, cr_ref, oc_ref)
    scan((q_ref, k_ref, v_ref, bf_ref, bb_ref), L // C, of_ref, r_ref, o_ref)


def _gla(qk, gv, gr, bf, bb, cqk, cgv, cgr, cbf, cbb, gain):
    B, L, _ = qk.shape
    Lc = cqk.shape[1]
    pair = 2 * GLA_DK
    pv = 2 * GLA_DV
    n_pair = GLA_HEADS // 2
    seq = lambda rows, n, off: pl.BlockSpec((1, rows, n), lambda b, p: (b, 0, p + off))
    const = lambda shape: pl.BlockSpec(shape, lambda b, p: (0,) * len(shape))
    est = (2 * (2 * L * pair * 2 + 2 * L * pv * 2 + 2 * L * pair * 4 + L * pv * 2)
           + 2 * L * pv * 4 + (8 << 20))
    return pl.pallas_call(
        _gla_kernel,
        grid=(B, n_pair),
        in_specs=[
            seq(L, pair, 0), seq(L, pair, n_pair), seq(L, pv, 0), seq(L, pv, 0), seq(L, pair, 0), seq(L, pair, 0),
            seq(Lc, pair, 0), seq(Lc, pair, n_pair), seq(Lc, pv, 0), seq(Lc, pv, 0), seq(Lc, pair, 0),
            seq(Lc, pair, 0),
            const((1, GLA_DV)),
        ],
        out_specs=[seq(L, pv, 0), seq(Lc, pv, 0)],
        out_shape=[jax.ShapeDtypeStruct((B, L, GLA_WIDTH), BF16), jax.ShapeDtypeStruct((B, Lc, GLA_WIDTH), BF16)],
        scratch_shapes=[pltpu.VMEM((2, L, pv), F32), pltpu.VMEM((2, Lc, pv), F32), pltpu.VMEM((2, pv, pair), F32)],
        compiler_params=_params(("parallel", "parallel"), est),
        name="gla",
    )(qk, qk, gv, gr, bf, bb, cqk, cqk, cgv, cgr, cbf, cbb, gain.reshape(1, GLA_DV))


def _rope_tables(L):
    pos = np.arange(L)
    half = GLA_DK // 4
    inv = ROPE_BASE ** (-np.arange(half, dtype=np.float64) / half)
    lane = np.arange(2 * GLA_DK)
    jj = lane % (GLA_DK // 2)
    use_col = (lane % GLA_DK) >= GLA_DK // 2
    p = np.where(use_col[None, :], (pos % GRID_W)[:, None], (pos // GRID_W)[:, None]).astype(np.float64)
    ang = p * inv[jj % half][None, :]
    first = jj < half
    cos = np.cos(ang)
    sin = np.where(first[None, :], -np.sin(ang), np.sin(ang))
    return jnp.asarray(cos, F32), jnp.asarray(sin, F32)


def _softmax_pv(s_parts, v_parts):
    m = s_parts[0].max(axis=-1, keepdims=True)
    for s in s_parts[1:]:
        m = jnp.maximum(m, s.max(axis=-1, keepdims=True))
    den = None
    acc = None
    for s, v in zip(s_parts, v_parts):
        p = jnp.exp(s - m)
        den = p.sum(axis=-1, keepdims=True) if den is None else den + p.sum(axis=-1, keepdims=True)
        pv = _dot(p.astype(BF16), v)
        acc = pv if acc is None else acc + pv
    return acc / den


def _na_window_start():
    cols = np.arange(GRID_W)
    return np.clip(cols - NA_WIN_W // 2, 0, GRID_W - NA_WIN_W)


def _na_kernel(q_ref, k_ref, v_ref, ck_ref, cv_ref, bias_ref, o_ref):
    W = GRID_W
    rows = k_ref.shape[1] // W
    n_loc = bias_ref.shape[2]
    kh = n_loc // W
    r0 = pl.program_id(1) * NA_ROWS_PER_STEP
    pair = 2 * NA_DH
    lane = lax.broadcasted_iota(jnp.int32, (1, pair), 1)
    head0 = lane < NA_DH
    scale = jnp.asarray(NA_DH ** -0.5, q_ref.dtype)

    n_pairs = NA_HEADS // 2
    lanes = [slice(p * pair, (p + 1) * pair) for p in range(n_pairs)]

    def rows_body(it, carry):
        units = []
        for j in range(NA_ROWS_PER_ITER):
            rr = it * NA_ROWS_PER_ITER + j
            r = r0 + rr
            rs = jnp.clip(r - kh // 2, 0, rows - kh)
            qs = pl.ds(pl.multiple_of(rr * W, W), W)
            ks = pl.ds(pl.multiple_of(rs * W, W), n_loc)
            units += [(qs, ks, rs - r + NA_WIN_H - 1, p) for p in range(n_pairs)]
        sts = []
        for qs, ks, dr, p in units:
            q = q_ref[0, qs, lanes[p]] * scale
            zero = jnp.zeros_like(q)
            q2 = jnp.concatenate([jnp.where(head0, q, zero), jnp.where(head0, zero, q)], axis=0)
            keys = jnp.concatenate([k_ref[0, ks, lanes[p]], ck_ref[0, :, lanes[p]]], axis=0)
            sts.append(_dot_nt(keys, q2))
        pts = []
        for (qs, ks, dr, p), st in zip(units, sts):
            st = jnp.concatenate([st[:n_loc] + bias_ref[p, dr], st[n_loc:]], axis=0)
            pts.append(jnp.exp((st - st.max(axis=0, keepdims=True)).astype(BF16)))
        outs = []
        ones = jnp.ones((n_loc + ck_ref.shape[1], pair), BF16)
        for (qs, ks, dr, p), pt in zip(units, pts):
            vals = jnp.concatenate([v_ref[0, ks, lanes[p]], cv_ref[0, :, lanes[p]]], axis=0)
            pv = _dot_tn(pt, jnp.concatenate([vals, ones], axis=1))
            o2 = pv[:, :pair] / pv[:, pair:]
            outs.append(jnp.where(head0, o2[:W], o2[W:]))
        for j in range(NA_ROWS_PER_ITER):
            qs = units[j * n_pairs][0]
            o_ref[0, qs, :] = jnp.concatenate(outs[j * n_pairs:(j + 1) * n_pairs], axis=1).astype(o_ref.dtype)
        return carry

    lax.fori_loop(0, NA_ROWS_PER_STEP // NA_ROWS_PER_ITER, rows_body, 0, unroll=4)


def _na(nq, nk, nv, cnk, cnv, bias):
    B, L, n = nq.shape
    Lc = cnk.shape[1]
    tq = NA_ROWS_PER_STEP * GRID_W
    full = lambda rows: pl.BlockSpec((1, rows, n), lambda b, i: (b, 0, 0))
    est = 2 * (2 * L * n * 2 + 2 * Lc * n * 2 + bias.size * 4 + 2 * tq * n * 2) + (8 << 20)
    return pl.pallas_call(
        _na_kernel,
        grid=(B, L // tq),
        in_specs=[
            pl.BlockSpec((1, tq, n), lambda b, i: (b, i, 0)),
            full(L), full(L), full(Lc), full(Lc),
            pl.BlockSpec(bias.shape, lambda b, i: (0, 0, 0, 0)),
        ],
        out_specs=pl.BlockSpec((1, tq, n), lambda b, i: (b, i, 0)),
        out_shape=jax.ShapeDtypeStruct((B, L, n), BF16),
        compiler_params=_params(("parallel", "parallel"), est),
        name="neighbourhood_attention",
    )(nq, nk, nv, cnk, cnv, bias)


def _na_bias_table(rpb, rows):
    kh = min(NA_WIN_H, rows)
    start = _na_window_start()
    kc = np.arange(GRID_W)
    inside = (kc[None, :] >= start[:, None]) & (kc[None, :] < start[:, None] + NA_WIN_W)
    sel = np.zeros((2 * NA_WIN_W - 1, GRID_W, GRID_W), np.float32)
    qq, kk = np.nonzero(inside)
    sel[kk - qq + NA_WIN_W - 1, qq, kk] = 1.0
    by_row = jnp.stack([rpb[:, d:d + kh, :] for d in range(NA_WIN_H)], axis=1)
    t = jnp.einsum('hdic,cqk->hdikq', by_row, jnp.asarray(sel), precision=lax.Precision.HIGHEST)
    t = t + jnp.asarray(np.where(inside, 0.0, MASK_VALUE).T, F32)[None, None, None, :, :]
    t = t.reshape(NA_HEADS // 2, 2, NA_WIN_H, kh * GRID_W, GRID_W)
    return jnp.transpose(t, (0, 2, 3, 1, 4)).reshape(NA_HEADS // 2, NA_WIN_H, kh * GRID_W, 2 * GRID_W)


def _ctx_attn_kernel(q_ref, k_ref, v_ref, o_ref):
    pair = 2 * NA_DH
    lane = lax.broadcasted_iota(jnp.int32, (1, pair), 1)
    head0 = lane < NA_DH
    Lc = q_ref.shape[1]
    outs = []
    for p in range(NA_HEADS // 2):
        ls = slice(p * pair, (p + 1) * pair)
        q = q_ref[0, :, ls]
        zero = jnp.zeros_like(q)
        q2 = jnp.concatenate([jnp.where(head0, q, zero), jnp.where(head0, zero, q)], axis=0)
        s = _dot_nt(q2, k_ref[0, :, ls]) * (NA_DH ** -0.5)
        o2 = _softmax_pv([s], [v_ref[0, :, ls]])
        outs.append(jnp.where(head0, o2[:Lc], o2[Lc:]))
    o_ref[0] = jnp.concatenate(outs, axis=1).astype(o_ref.dtype)


def _ctx_attn(cnq, cnk, cnv):
    B, Lc, n = cnq.shape
    spec = pl.BlockSpec((1, Lc, n), lambda b: (b, 0, 0))
    return pl.pallas_call(
        _ctx_attn_kernel,
        grid=(B,),
        in_specs=[spec, spec, spec],
        out_specs=spec,
        out_shape=jax.ShapeDtypeStruct((B, Lc, n), BF16),
        compiler_params=_params(("parallel",), 8 * Lc * n * 2 + (8 << 20)),
        name="context_attention",
    )(cnq, cnk, cnv)


def _pack_bf16_pairs(x):
    n = x.shape[1] // 2
    lo = lax.bitcast_convert_type(x[:, :n].astype(BF16).astype(F32), jnp.int32)
    hi = lax.bitcast_convert_type(x[:, n:].astype(BF16).astype(F32), jnp.int32)
    return lax.shift_right_logical(lo, 16) | (hi & jnp.int32(-65536))


def _unpack_bf16_pairs(p):
    lo = lax.bitcast_convert_type(lax.shift_left(p, 16), F32)
    hi = lax.bitcast_convert_type(p & jnp.int32(-65536), F32)
    return jnp.concatenate([lo, hi], axis=1)


def _store_packed(refs, x):
    n = 2 * SC_ROW_WORDS
    for p, ref in enumerate(refs):
        ref[...] = _pack_bf16_pairs(x[:, p * n:(p + 1) * n]).reshape(ref.shape)


def _load_packed(refs):
    return jnp.concatenate([_unpack_bf16_pairs(ref[...].reshape(ref.shape[-2:])) for ref in refs], axis=1)


def _route(logits):
    lane = lax.broadcasted_iota(jnp.int32, logits.shape, 1)
    big = jnp.int32(logits.shape[1])
    t1 = logits.max(axis=-1, keepdims=True)
    i1 = jnp.where(logits == t1, lane, big).min(axis=-1, keepdims=True)
    rest = jnp.where(lane == i1, -jnp.inf, logits)
    t2 = rest.max(axis=-1, keepdims=True)
    i2 = jnp.where(rest == t2, lane, big).min(axis=-1, keepdims=True)
    e2 = jnp.exp(t2 - t1)
    return i1, i2, 1.0 / (1.0 + e2), e2 / (1.0 + e2)


def _outproj_kernel(*refs, dense_ffn):
    ga_ref, na_ref, x_ref, mod_ref, wa_ref, wb_ref, gp_ref, gf_ref = refs[:8]
    m = mod_ref[0]
    y = _dot(ga_ref[0], wa_ref[...]) + _dot(na_ref[0], wb_ref[...])
    x1 = x_ref[0] + m[2:3] * (_rms(y) * gp_ref[...])
    h2 = _rms(x1) * gf_ref[...] * (1.0 + m[4:5]) + m[3:4]
    if dense_ffn:
        wg_ref, wu_ref, wd_ref, g2_ref, o_ref = refs[8:]
        h = h2.astype(BF16)
        hid = (_silu(_dot(h, wg_ref[...])) * _dot(h, wu_ref[...])).astype(BF16)
        o_ref[0] = x1 + m[5:6] * (_rms(_dot(hid, wd_ref[...])) * g2_ref[...])
    else:
        wr_ref, x1_ref, gw_ref, eid_ref, cnt_ref, *h2_refs = refs[8:]
        x1_ref[0] = x1
        _store_packed(h2_refs, h2)
        tm = h2.shape[0]
        lanes = wr_ref.shape[1] // 2
        prod = _dot(jnp.concatenate(_split(h2), axis=0), wr_ref[...])
        logits = (prod[:tm, :lanes] + prod[tm:, :lanes]) + (prod[:tm, lanes:] + prod[tm:, lanes:])
        lane = lax.broadcasted_iota(jnp.int32, (tm, lanes), 1)
        logits = jnp.where(lane < N_EXPERTS, logits, -jnp.inf)
        i1, i2, w1, w2 = _route(logits)
        gw_ref[0] = jnp.where(lane == 0, w1, jnp.where(lane == 1, w2, 0.0))
        eid_ref[0] = jnp.where(lane == 0, i1, jnp.where(lane == 1, i2, 0))
        chosen = jnp.where(lane == i1, 1.0, jnp.where(lane == i2, 1.0, 0.0))
        cnt_ref[0] = jnp.broadcast_to(chosen.sum(axis=0, keepdims=True), cnt_ref.shape[1:])


def _outproj(ga, na, x, mod, mod_row0, w_out, g_post, g_ffn, *, ffn=None, w_router=None):
    nb, rows, d = x.shape
    tm = min(TOKEN_TILE, rows)
    dense_ffn = ffn is not None
    tok = lambda n: pl.BlockSpec((1, tm, n), lambda b, i: (b, i, 0))
    const = lambda shape: pl.BlockSpec(shape, lambda b, i: (0,) * len(shape))
    once = lambda shape: pl.BlockSpec(shape, lambda b, i: (0,) * len(shape), pipeline_mode=pl.Buffered(1))
    wa = w_out[:GLA_WIDTH].astype(BF16)
    wb = w_out[GLA_WIDTH:].astype(BF16)
    args = [ga, na, x, mod, wa, wb, g_post.reshape(1, d), g_ffn.reshape(1, d)]
    in_specs = [tok(GLA_WIDTH), tok(NA_WIDTH), tok(d),
                pl.BlockSpec((1, 6, d), lambda b, i: (b + mod_row0, 0, 0)),
                const(wa.shape), const(wb.shape), const((1, d)), const((1, d))]
    est = 2 * (tm * d * (4 + 4 + 2) + 2 * tm * GLA_WIDTH * 2 + d * d * 2) + 4 * tm * d * 4
    if dense_ffn:
        wg, wu, wd, g2 = ffn
        dff = wg.shape[1]
        args += [wg, wu, wd, g2.reshape(1, d)]
        in_specs += [once(wg.shape), once(wu.shape), once(wd.shape), const((1, d))]
        out_specs = tok(d)
        out_shape = jax.ShapeDtypeStruct((nb, rows, d), F32)
        est += 3 * d * dff * 2 + 3 * tm * dff * 4
    else:
        nt = rows // tm
        wr = jnp.zeros((d, V7X_LANES), F32).at[:, :N_EXPERTS].set(w_router)
        wr = jnp.concatenate(_split(wr), axis=1)
        args.append(wr)
        in_specs.append(const(wr.shape))
        n_parts = d // (2 * SC_ROW_WORDS)
        out_specs = [tok(d), tok(V7X_LANES), tok(V7X_LANES),
                     pl.BlockSpec((1, 8, V7X_LANES), lambda b, i: (b * nt + i, 0, 0))] + [tok(SC_ROW_WORDS)] * n_parts
        out_shape = [jax.ShapeDtypeStruct((nb, rows, d), F32),
                     jax.ShapeDtypeStruct((nb, rows, V7X_LANES), F32),
                     jax.ShapeDtypeStruct((nb, rows, V7X_LANES), jnp.int32),
                     jax.ShapeDtypeStruct((nb * nt, 8, V7X_LANES), F32)]
        out_shape += [jax.ShapeDtypeStruct((nb, rows, SC_ROW_WORDS), jnp.int32)] * n_parts
    return pl.pallas_call(
        functools.partial(_outproj_kernel, dense_ffn=dense_ffn),
        grid=(nb, rows // tm),
        in_specs=in_specs,
        out_specs=out_specs,
        out_shape=out_shape,
        compiler_params=_params(("parallel", "parallel"), est),
        name="outproj_ffn" if dense_ffn else "outproj_router",
    )(*args)


def _slot_kernel(eid_ref, base_ref, pos_ref):
    eid = eid_ref[...]
    tm, lanes = eid.shape
    lane = lax.broadcasted_iota(jnp.int32, (tm, lanes), 1)
    i1 = eid[:, 0:1]
    i2 = eid[:, 1:2]
    chosen = jnp.where(lane == i1, 1.0, jnp.where(lane == i2, 1.0, 0.0)).astype(BF16)
    row = lax.broadcasted_iota(jnp.int32, (tm, tm), 0)
    col = lax.broadcasted_iota(jnp.int32, (tm, tm), 1)
    incl = jnp.where(row >= col, 1.0, 0.0).astype(BF16)
    slot = base_ref[0][0:1] + _dot(incl, chosen) - 1.0
    p1 = jnp.where(lane == i1, slot, 0.0).sum(axis=-1, keepdims=True).astype(jnp.int32)
    p2 = jnp.where(lane == i2, slot, 0.0).sum(axis=-1, keepdims=True).astype(jnp.int32)
    pos_ref[...] = jnp.where(lane == 0, p1, jnp.where(lane == 1, p2, 0))


def _slots(eid, tile_base, tm):
    t, lanes = eid.shape
    return pl.pallas_call(
        _slot_kernel,
        grid=(t // tm,),
        in_specs=[pl.BlockSpec((tm, lanes), lambda i: (i, 0)),
                  pl.BlockSpec((1, 8, lanes), lambda i: (i, 0, 0))],
        out_specs=pl.BlockSpec((tm, lanes), lambda i: (i, 0)),
        out_shape=jax.ShapeDtypeStruct((t, lanes), jnp.int32),
        compiler_params=_params(("parallel",), 8 * tm * lanes * 4 + 4 * tm * tm),
        name="moe_slots",
    )(eid, tile_base)


def _route_plan(cnt, tm_tokens):
    counts = cnt[:, 0, :N_EXPERTS].astype(jnp.int32)
    total = counts.sum(axis=0)
    padded = -(-total // MOE_ROW_TILE) * MOE_ROW_TILE
    start = jnp.cumsum(padded) - padded
    before = jnp.cumsum(counts, axis=0) - counts
    tile_base = (start[None, :] + before).astype(F32)
    tile_base = jnp.zeros((cnt.shape[0], 8, V7X_LANES), F32).at[:, :, :N_EXPERTS].set(tile_base[:, None, :])
    n_slots = tm_tokens * 2 + N_EXPERTS * MOE_ROW_TILE
    first_row = jnp.arange(n_slots // MOE_ROW_TILE, dtype=jnp.int32) * MOE_ROW_TILE
    tile_expert = jnp.minimum((first_row[:, None] >= (start + padded)[None, :]).sum(axis=1), N_EXPERTS - 1)
    n_valid = jnp.clip(start[tile_expert] + total[tile_expert] - first_row, 0, MOE_ROW_TILE)
    return tile_base, tile_expert.astype(jnp.int32), n_valid.astype(jnp.int32), n_slots


def _sc_mesh():
    return plsc.VectorSubcoreMesh(core_axis_name="core", subcore_axis_name="subcore")


def _scatter_rows(x, idx, n_out):
    t, w = x.shape
    n = idx.shape[0]
    win = SC_ROW_WINDOW
    n_blk = t // win

    @functools.partial(pl.kernel, out_type=jax.ShapeDtypeStruct((n_out, w), x.dtype), mesh=_sc_mesh(),
                       scratch_types=[], name="moe_dispatch")
    def scatter(x_hbm, i_hbm, o_hbm):
        def body(x_vmem, i_vmem):
            pltpu.sync_copy(x_vmem, o_hbm.at[i_vmem.at[0]])

        pltpu.emit_pipeline(
            body,
            grid=(n // win,),
            in_specs=[pl.BlockSpec((win, w), lambda i: (i % n_blk, 0)),
                      pl.BlockSpec((1, win), lambda i: (0, i))],
            out_specs=[],
            core_axis_name=("core", "subcore"),
            dimension_semantics=(pltpu.PARALLEL,),
        )(x_hbm, i_hbm)

    return scatter(x, idx.reshape(1, n))


def _gather_rows(x, idx):
    n = idx.shape[0]
    w = x.shape[1]
    win = SC_ROW_WINDOW

    @functools.partial(pl.kernel, out_type=jax.ShapeDtypeStruct((n, w), x.dtype), mesh=_sc_mesh(),
                       scratch_types=[], name="moe_combine_gather")
    def gather(x_hbm, i_hbm, o_hbm):
        def body(i_vmem, o_vmem):
            pltpu.sync_copy(x_hbm.at[i_vmem.at[0]], o_vmem)

        pltpu.emit_pipeline(
            body,
            grid=(n // win,),
            in_specs=[pl.BlockSpec((1, win), lambda i: (0, i))],
            out_specs=[pl.BlockSpec((win, w), lambda i: (i, 0))],
            core_axis_name=("core", "subcore"),
            dimension_semantics=(pltpu.PARALLEL,),
        )(i_hbm, o_hbm)

    return gather(x, idx.reshape(1, n))


def _experts_kernel(te_ref, nv_ref, *refs, n_parts):
    x_refs = refs[:n_parts]
    wg_ref, wu_ref, wd_ref = refs[n_parts:n_parts + 3]
    y_refs = refs[n_parts + 3:2 * n_parts + 3]
    i = pl.program_id(0)
    n_valid = nv_ref[i]
    tm = x_refs[0].shape[0]
    dff = wg_ref.shape[2]

    @pl.when(n_valid > 0)
    def _():
        row = lax.broadcasted_iota(jnp.int32, (tm, 1), 0)
        x = jnp.where(row < n_valid, _load_packed(x_refs), 0.0).astype(BF16)
        acc = None
        for lo in range(0, dff, MOE_FF_CHUNK):
            sl = slice(lo, lo + MOE_FF_CHUNK)
            hid = (_silu(_dot(x, wg_ref[0, :, sl])) * _dot(x, wu_ref[0, :, sl])).astype(BF16)
            part = _dot(hid, wd_ref[0, sl, :])
            acc = part if acc is None else acc + part
        _store_packed(y_refs, acc)

    @pl.when(n_valid == 0)
    def _():
        for ref in y_refs:
            ref[...] = jnp.zeros_like(ref)


def _experts(xs_parts, tile_expert, n_valid, wg, wu, wd):
    n_parts = len(xs_parts)
    n_slots, words = xs_parts[0].shape
    n_e, d, dff = wg.shape
    tm = MOE_ROW_TILE
    rows_spec = pl.BlockSpec((tm, words), lambda i, te, nv: (i, 0))
    weights = lambda shape: pl.BlockSpec((1,) + shape, lambda i, te, nv: (te[i], 0, 0))
    est = 2 * 3 * d * dff * 2 + 4 * tm * d * 2 + 3 * tm * MOE_FF_CHUNK * 4 + 3 * tm * d * 4
    grid_spec = pltpu.PrefetchScalarGridSpec(
        num_scalar_prefetch=2,
        grid=(n_slots // tm,),
        in_specs=[rows_spec] * n_parts + [weights((d, dff)), weights((d, dff)), weights((dff, d))],
        out_specs=[rows_spec] * n_parts,
    )
    return pl.pallas_call(
        functools.partial(_experts_kernel, n_parts=n_parts),
        grid_spec=grid_spec,
        out_shape=[jax.ShapeDtypeStruct((n_slots, words), jnp.int32)] * n_parts,
        compiler_params=_params(("arbitrary",), est),
        name="moe_experts",
    )(tile_expert, n_valid, *xs_parts, wg, wu, wd)


def _combine_kernel(*refs, n_parts):
    y1_refs = refs[:n_parts]
    y2_refs = refs[n_parts:2 * n_parts]
    gw_ref, x_ref, mod_ref, g_ref, o_ref = refs[2 * n_parts:]
    gw = gw_ref[0]
    y = gw[:, 0:1] * _load_packed(y1_refs) + gw[:, 1:2] * _load_packed(y2_refs)
    o_ref[0] = x_ref[0] + mod_ref[0][5:6] * (_rms(y) * g_ref[...])


def _combine(ys2_parts, gw, x1, mod, mod_row0, g_post):
    n_parts = len(ys2_parts)
    nb, rows, d = x1.shape
    words = ys2_parts[0].shape[-1]
    tm = min(TOKEN_TILE, rows)
    tok = lambda n: pl.BlockSpec((1, tm, n), lambda b, i: (b, i, 0))
    ysp = lambda k: pl.BlockSpec((1, tm, words), lambda b, i: (k * nb + b, i, 0))
    est = 2 * tm * (d * 2 * 4 + d * 4 + V7X_LANES * 4) + 3 * tm * d * 4
    return pl.pallas_call(
        functools.partial(_combine_kernel, n_parts=n_parts),
        grid=(nb, rows // tm),
        in_specs=[ysp(0)] * n_parts + [ysp(1)] * n_parts + [
            tok(V7X_LANES), tok(d),
            pl.BlockSpec((1, 6, d), lambda b, i: (b + mod_row0, 0, 0)),
            pl.BlockSpec((1, d), lambda b, i: (0, 0))],
        out_specs=tok(d),
        out_shape=jax.ShapeDtypeStruct((nb, rows, d), F32),
        compiler_params=_params(("parallel", "parallel"), est),
        name="moe_combine",
    )(*ys2_parts, *ys2_parts, gw, x1, mod, g_post.reshape(1, d))


def _moe_routed(gw, eid, cnt, *h2_parts, x1, mod, mod_row0, experts, g_post):
    nb, rows, d = x1.shape
    t = nb * rows
    tm = min(TOKEN_TILE, rows)
    tile_base, tile_expert, n_valid, n_slots = _route_plan(cnt, t)
    pos = _slots(eid.reshape(t, V7X_LANES), tile_base, tm)
    idx = jnp.concatenate([pos[:, 0], pos[:, 1]])
    xs = [_scatter_rows(h.reshape(t, h.shape[-1]), idx, n_slots) for h in h2_parts]
    ys = _experts(xs, tile_expert, n_valid, *experts)
    ys2 = [_gather_rows(y, idx).reshape(2 * nb, rows, y.shape[-1]) for y in ys]
    return _combine(ys2, gw, x1, mod, mod_row0, g_post)


def _cast_kernel(w_ref, o_ref):
    o_ref[...] = w_ref[...].astype(o_ref.dtype)


def _to_bf16(w):
    shape = w.shape
    w3 = w.reshape((-1,) + shape[-2:])
    n, r, c = w3.shape
    rb = r
    while rb * c * 4 > CAST_BLOCK_BYTES and rb % 16 == 0:
        rb //= 2
    spec = pl.BlockSpec((1, rb, c), lambda e, i: (e, i, 0))
    out = pl.pallas_call(
        _cast_kernel,
        grid=(n, r // rb),
        in_specs=[spec],
        out_specs=spec,
        out_shape=jax.ShapeDtypeStruct(w3.shape, BF16),
        compiler_params=_params(("parallel", "parallel"), 2 * rb * c * 6),
        name="cast_bf16",
    )(w3)
    return out.reshape(shape)
def _pack_w_in(w):
    a = 2 * GLA_KDIM + 2 * GLA_WIDTH
    lr = 2 * GLA_GATE_RANK
    pad = jnp.zeros((w.shape[0], LR_PAD - lr), w.dtype)
    return jnp.concatenate([w[:, :a], w[:, a + lr:], w[:, a:a + lr], pad], axis=1).astype(BF16)


def _pack_gate(w_gate, b_gate):
    wgp = jnp.zeros((2, LR_PAD, GLA_KDIM), F32)
    for d in range(2):
        wgp = wgp.at[d, d * GLA_GATE_RANK:(d + 1) * GLA_GATE_RANK].set(w_gate[d])
    return wgp, b_gate.reshape(2, 1, GLA_KDIM)


def kernel(x, c, ctx, c_ctx, w_ada, b_ada, g_pre_mix, g_post_mix, g_pre_ffn, g_post_ffn, w_in,
           gla_w_gate, gla_b_gate, gla_g_norm, na_rpb, w_out, ffn_w_gate, ffn_w_up, ffn_w_down,
           moe_w_router, moe_w_gate, moe_w_up, moe_w_down):
    B, L, D = x.shape
    Lc = ctx.shape[1]
    depth = w_ada.shape[0]
    rows = L // GRID_W
    ctx_row = B
    n_cond = -(-(B + 1) // 8) * 8
    cond = jnp.zeros((n_cond, D), F32).at[:B].set(c).at[B].set(c_ctx)
    mod = _modulation(cond, w_ada, b_ada)
    tables = _rope_tables(L)
    ctx = ctx.reshape(1, B * Lc, D)

    for i in range(depth):
        last = i == depth - 1
        j = i // 2
        w = _pack_w_in(w_in[i])
        wgp, bgp = _pack_gate(gla_w_gate[i], gla_b_gate[i])
        qk, gv, gr, nq, nk, nv, bf, bb = _inproj(x, mod[i], 0, g_pre_mix[i], w, wgp, bgp, tables)
        cparts = _inproj(ctx, mod[i], ctx_row, g_pre_mix[i], w, wgp, bgp)
        cqk, cgv, cgr, cnq, cnk, cnv, cbf, cbb = [t.reshape(B, Lc, t.shape[-1]) for t in cparts]
        ga, gac = _gla(qk, gv, gr, bf, bb, cqk, cgv, cgr, cbf, cbb, gla_g_norm[i])
        na = _na(nq, nk, nv, cnk, cnv, _na_bias_table(na_rpb[i], rows))
        if i % 2 == 0:
            ffn = (_to_bf16(ffn_w_gate[j]), _to_bf16(ffn_w_up[j]), _to_bf16(ffn_w_down[j]), g_post_ffn[i])

            def mix_and_ffn(ga_, na_, x_, row0):
                return _outproj(ga_, na_, x_, mod[i], row0, w_out[i], g_post_mix[i], g_pre_ffn[i], ffn=ffn)
        else:
            experts = (_to_bf16(moe_w_gate[j]), _to_bf16(moe_w_up[j]), _to_bf16(moe_w_down[j]))

            def mix_and_ffn(ga_, na_, x_, row0):
                x1, *routed = _outproj(ga_, na_, x_, mod[i], row0, w_out[i], g_post_mix[i], g_pre_ffn[i],
                                       w_router=moe_w_router[j])
                return _moe_routed(*routed, x1=x1, mod=mod[i], mod_row0=row0, experts=experts, g_post=g_post_ffn[i])
        x = mix_and_ffn(ga, na, x, 0)
        if not last:
            nac = _ctx_attn(cnq, cnk, cnv)
            ctx = mix_and_ffn(gac.reshape(1, B * Lc, GLA_WIDTH), nac.reshape(1, B * Lc, NA_WIDTH), ctx, ctx_row)
    return x
```

```python
import functools

import numpy as np
import jax
import jax.numpy as jnp
from jax import lax
from jax.experimental import pallas as pl
from jax.experimental.pallas import tpu as pltpu
from jax.experimental.pallas import tpu_sc as plsc

F32 = jnp.float32
BF16 = jnp.bfloat16

GRID_W = 64
GLA_HEADS = 4
GLA_DV = 128
GLA_DK = 64
GLA_KDIM = GLA_HEADS * GLA_DK
GLA_WIDTH = GLA_HEADS * GLA_DV
GLA_GATE_RANK = 16
GLA_GATE_NORM = 16.0
NA_HEADS = 8
NA_DH = 64
NA_WIDTH = NA_HEADS * NA_DH
NA_WIN_H = 8
NA_WIN_W = 16
ROPE_BASE = 10000.0
N_EXPERTS = 8
EPS = 1e-6

V7X_LANES = 128
V7X_VMEM_BYTES = 64 * 1024 * 1024
V7X_VMEM_USABLE = V7X_VMEM_BYTES - 8 * 1024 * 1024

TOKEN_TILE = 512
GLA_CHUNK = 128
GLA_UNROLL = 4
NA_ROWS_PER_STEP = 8
NA_ROWS_PER_ITER = 2
MASK_VALUE = -1e30
MOE_FF_CHUNK = 1792
MOE_ROW_TILE = 512
SC_ROW_WINDOW = 128
CAST_BLOCK_BYTES = 4 << 20
SC_ROW_WORDS = 256
LR_PAD = V7X_LANES


def _vmem_limit(estimate_bytes):
    return int(min(V7X_VMEM_USABLE, estimate_bytes * 5 // 4 + (4 << 20)))


def _params(semantics, vmem_estimate):
    return pltpu.CompilerParams(dimension_semantics=semantics, vmem_limit_bytes=_vmem_limit(vmem_estimate))


def _dot(a, b):
    return jnp.dot(a, b, preferred_element_type=F32)


def _dot_nt(a, b):
    return lax.dot_general(a, b, (((1,), (1,)), ((), ())), preferred_element_type=F32)


def _dot_tn(a, b):
    return lax.dot_general(a, b, (((0,), (0,)), ((), ())), preferred_element_type=F32)


def _split(x):
    hi = x.astype(BF16)
    lo = (x - hi.astype(F32)).astype(BF16)
    return hi, lo


def _dot3(a, b):
    ah, al = _split(a)
    bh, bl = _split(b)
    return _dot(ah, bh) + (_dot(al, bh) + _dot(ah, bl))


def _sigmoid(x):
    return 1.0 / (1.0 + jnp.exp(-x))


def _silu(x):
    return x * _sigmoid(x)


def _rms(x):
    return x * lax.rsqrt(jnp.mean(x * x, axis=-1, keepdims=True) + EPS)


def _mod_kernel(c_ref, w_ref, b_ref, o_ref):
    o_ref[0] = _dot3(_silu(c_ref[...]), w_ref[0]) + b_ref[0]


def _modulation(cond, w_ada, b_ada):
    depth, d, n = w_ada.shape
    rows = cond.shape[0]
    tn = 512
    out = pl.pallas_call(
        _mod_kernel,
        grid=(depth, n // tn),
        in_specs=[
            pl.BlockSpec((rows, d), lambda l, j: (0, 0)),
            pl.BlockSpec((1, d, tn), lambda l, j: (l, 0, j)),
            pl.BlockSpec((1, 1, tn), lambda l, j: (l, 0, j)),
        ],
        out_specs=pl.BlockSpec((1, rows, tn), lambda l, j: (l, 0, j)),
        out_shape=jax.ShapeDtypeStruct((depth, rows, n), F32),
        compiler_params=_params(("parallel", "parallel"), 3 * d * tn * 4 * 2),
        name="modulation",
    )(cond, w_ada, b_ada.reshape(depth, 1, n))
    return out.reshape(depth, rows, 6, d)


def _log_decay(logit):
    return (jnp.minimum(logit, 0.0) - jnp.log(1.0 + jnp.exp(-jnp.abs(logit)))) * (1.0 / GLA_GATE_NORM)


def _inproj_kernel(*refs, rope):
    if rope:
        x_ref, mod_ref, g_ref, w_ref, wg_ref, bg_ref, cos_ref, sin_ref = refs[:8]
    else:
        x_ref, mod_ref, g_ref, w_ref, wg_ref, bg_ref = refs[:6]
    qk_ref, gv_ref, gr_ref, nq_ref, nk_ref, nv_ref, bf_ref, bb_ref = refs[-8:]
    m = mod_ref[0]
    h = (_rms(x_ref[0]) * g_ref[...] * (1.0 + m[1:2]) + m[0:1]).astype(BF16)
    tm = h.shape[0]
    plain_refs = (gv_ref, gr_ref, nq_ref, nk_ref, nv_ref)
    offs = np.cumsum([0, qk_ref.shape[-1]] + [r.shape[-1] for r in plain_refs])

    def project(k):
        ref = plain_refs[k]
        ref[0] = _dot(h, w_ref[:, offs[k + 1]:offs[k + 2]]).astype(ref.dtype)

    project(0)
    lr = _dot(h, w_ref[:, offs[-1]:offs[-1] + LR_PAD])
    lr_hi, lr_lo = _split(lr)
    lr_lane = lax.broadcasted_iota(jnp.int32, (1, LR_PAD), 1)
    lr_mix = jnp.where((lr_lane // (2 * GLA_GATE_RANK)) % 2 == 0, lr_hi, lr_lo)
    logits = [_dot(lr_mix, wg_ref[d]) + bg_ref[d] for d in range(2)]
    n = qk_ref.shape[-1]
    qk = _dot(h, w_ref[:, 0:n])
    lane = lax.broadcasted_iota(jnp.int32, (1, n), 1)
    if rope:
        reps = n // cos_ref.shape[-1]
        cos = jnp.concatenate([cos_ref[...]] * reps, axis=1)
        sin = jnp.concatenate([sin_ref[...]] * reps, axis=1)
        quarter = GLA_DK // 4
        first = (lane % (2 * quarter)) < quarter
        partner = jnp.where(first, pltpu.roll(qk, n - quarter, 1), pltpu.roll(qk, quarter, 1))
        qk = qk * cos + partner * sin
    qk_ref[0] = jnp.where(lane < GLA_KDIM, qk * (GLA_DK ** -0.5), qk).astype(qk_ref.dtype)
    C = GLA_CHUNK
    row = lax.broadcasted_iota(jnp.int32, (C, 2 * C), 0)
    colm = lax.broadcasted_iota(jnp.int32, (C, 2 * C), 1) % C
    for d, out_ref in enumerate((bf_ref, bb_ref)):
        gh, gl = _split(_log_decay(logits[d]))
        project(1 + 2 * d)
        tri = (row >= colm) if d == 0 else (row <= colm)
        tmat = jnp.where(tri, 1.0, 0.0).astype(BF16)
        for c in range(tm // C):
            sl = slice(c * C, (c + 1) * C)
            out_ref[0, sl, :] = _dot(tmat, jnp.concatenate([gh[sl], gl[sl]], axis=0))
        project(2 + 2 * d)


def _inproj(x, mod, mod_row0, g, w, wgp, bgp, rope_tables=None):
    nb, rows, d = x.shape
    tm = min(TOKEN_TILE, rows)
    rope = rope_tables is not None
    widths = (2 * GLA_KDIM, GLA_WIDTH, GLA_WIDTH, NA_WIDTH, NA_WIDTH, NA_WIDTH, GLA_KDIM, GLA_KDIM)
    dtypes = (BF16,) * 6 + (F32, F32)
    tok = lambda n: pl.BlockSpec((1, tm, n), lambda b, i: (b, i, 0))
    const = lambda shape: pl.BlockSpec(shape, lambda b, i: (0,) * len(shape))
    args = [x, mod, g.reshape(1, d), w, wgp, bgp]
    in_specs = [tok(d), pl.BlockSpec((1, 6, d), lambda b, i: (b + mod_row0, 0, 0)), const((1, d)),
                const(w.shape), const(wgp.shape), const(bgp.shape)]
    if rope:
        args += list(rope_tables)
        in_specs += [pl.BlockSpec((tm, t.shape[1]), lambda b, i: (i, 0)) for t in rope_tables]
    est = 2 * (tm * d * 4 + d * w.shape[1] * 2 + sum(widths) * tm * 4) + tm * d * 8
    return pl.pallas_call(
        functools.partial(_inproj_kernel, rope=rope),
        grid=(nb, rows // tm),
        in_specs=in_specs,
        out_specs=[tok(n) for n in widths],
        out_shape=[jax.ShapeDtypeStruct((nb, rows, n), dt) for n, dt in zip(widths, dtypes)],
        compiler_params=_params(("parallel", "parallel"), est),
        name="inproj",
    )(*args)


def _gla_kernel(q_ref, k_ref, v_ref, r_ref, bf_ref, bb_ref, cq_ref, ck_ref, cv_ref, cr_ref, cbf_ref, cbb_ref,
                gain_ref, o_ref, oc_ref, of_ref, ocf_ref, st_ref):
    C = GLA_CHUNK
    L = q_ref.shape[1]
    Lc = cq_ref.shape[1]
    row = lax.broadcasted_iota(jnp.int32, (C, C), 0)
    col = lax.broadcasted_iota(jnp.int32, (C, C), 1)
    lane = lax.broadcasted_iota(jnp.int32, (1, 2 * GLA_DK), 1)
    head0 = lane < GLA_DK
    st_row = lax.broadcasted_iota(jnp.int32, (2 * GLA_DV, 2 * GLA_DK), 0) // GLA_DV
    st_col = lax.broadcasted_iota(jnp.int32, (2 * GLA_DV, 2 * GLA_DK), 1) // GLA_DK
    blockdiag = st_row == st_col
    gain = gain_ref[...]

    def chunk_steps(refs, chunks, acc_ref):
        rq, rk, rv = refs[:3]
        dirs = (0, 1)
        sls = [pl.ds(pl.multiple_of(i * C, C), C) for i in chunks]
        tris = [row >= col, row <= col]
        vs = [rv[0, sl, :] for sl in sls]
        qfs = [rq[0, sl, :].astype(F32) for sl in sls]
        kfs = [rk[0, sl, :].astype(F32) for sl in sls]
        bs = [refs[3 + d][0, sls[d], :] for d in dirs]
        b_mids = [b[C // 2:C // 2 + 1] for b in bs]
        b_edges = [bs[0][C - 1:C], bs[1][0:1]]
        qes = [(qfs[d] * jnp.exp(bs[d] - b_mids[d])).astype(BF16) for d in dirs]
        kes = [(kfs[d] * jnp.exp(b_mids[d] - bs[d])).astype(BF16) for d in dirs]
        zero = jnp.zeros_like(qes[0])
        lhss = [jnp.concatenate([jnp.where(head0, qe, zero), jnp.where(head0, zero, qe)], axis=0) for qe in qes]
        scores = [_dot_nt(lhss[d], kes[d]) for d in dirs]
        qbs = [(qfs[d] * jnp.exp(bs[d])).astype(BF16) for d in dirs]
        kds = [(kfs[d] * jnp.exp(b_edges[d] - bs[d])).astype(BF16) for d in dirs]
        sts = [st_ref[d] for d in dirs]
        inter = [_dot_nt(qbs[d], sts[d].astype(BF16)) for d in dirs]
        dss = [_dot_tn(vs[d], kds[d]) for d in dirs]
        ams = [jnp.where(jnp.concatenate([tris[d], tris[d]], axis=0), scores[d], 0.0).astype(BF16) for d in dirs]
        intra = [jnp.concatenate([_dot(ams[d][:C], vs[d][:, :GLA_DV]), _dot(ams[d][C:], vs[d][:, GLA_DV:])], axis=1)
                 for d in dirs]
        for d in dirs:
            st_ref[d] = sts[d] * jnp.exp(b_edges[d]) + jnp.where(blockdiag, dss[d], 0.0)
            acc_ref[d, sls[d], :] = intra[d] + inter[d]

    def finish(total, r):
        outs = []
        for h in range(2):
            oh = _rms(total[:, h * GLA_DV:(h + 1) * GLA_DV]) * gain
            outs.append(oh * _silu(r[:, h * GLA_DV:(h + 1) * GLA_DV].astype(F32)))
        return jnp.concatenate(outs, axis=1)

    def scan(refs, n, acc_ref, r_ref_, out_ref):
        def body(j, carry):
            chunk_steps(refs, (j, n - 1 - j), acc_ref)
            return carry

        lax.fori_loop(0, n, body, 0, unroll=min(GLA_UNROLL, n))

        def fin(i, carry):
            sl = pl.ds(pl.multiple_of(i * C, C), C)
            out_ref[0, sl, :] = finish(acc_ref[0, sl, :] + acc_ref[1, sl, :], r_ref_[0, sl, :]).astype(out_ref.dtype)
            return carry

        lax.fori_loop(0, n, fin, 0, unroll=min(GLA_UNROLL, n))

    st_ref[...] = jnp.zeros_like(st_ref)
    scan((cq_ref, ck_ref, cv_ref, cbf_ref, cbb_ref), Lc // C, ocf_ref, cr_ref, oc_ref)
    scan((q_ref, k_ref, v_ref, bf_ref, bb_ref), L // C, of_ref, r_ref, o_ref)


def _gla(qk, gv, gr, bf, bb, cqk, cgv, cgr, cbf, cbb, gain):
    B, L, _ = qk.shape
    Lc = cqk.shape[1]
    pair = 2 * GLA_DK
    pv = 2 * GLA_DV
    n_pair = GLA_HEADS // 2
    seq = lambda rows, n, off: pl.BlockSpec((1, rows, n), lambda b, p: (b, 0, p + off))
    const = lambda shape: pl.BlockSpec(shape, lambda b, p: (0,) * len(shape))
    est = (2 * (2 * L * pair * 2 + 2 * L * pv * 2 + 2 * L * pair * 4 + L * pv * 2)
           + 2 * L * pv * 4 + (8 << 20))
    return pl.pallas_call(
        _gla_kernel,
        grid=(B, n_pair),
        in_specs=[
            seq(L, pair, 0), seq(L, pair, n_pair), seq(L, pv, 0), seq(L, pv, 0), seq(L, pair, 0), seq(L, pair, 0),
            seq(Lc, pair, 0), seq(Lc, pair, n_pair), seq(Lc, pv, 0), seq(Lc, pv, 0), seq(Lc, pair, 0),
            seq(Lc, pair, 0),
            const((1, GLA_DV)),
        ],
        out_specs=[seq(L, pv, 0), seq(Lc, pv, 0)],
        out_shape=[jax.ShapeDtypeStruct((B, L, GLA_WIDTH), BF16), jax.ShapeDtypeStruct((B, Lc, GLA_WIDTH), BF16)],
        scratch_shapes=[pltpu.VMEM((2, L, pv), F32), pltpu.VMEM((2, Lc, pv), F32), pltpu.VMEM((2, pv, pair), F32)],
        compiler_params=_params(("parallel", "parallel"), est),
        name="gla",
    )(qk, qk, gv, gr, bf, bb, cqk, cqk, cgv, cgr, cbf, cbb, gain.reshape(1, GLA_DV))


def _rope_tables(L):
    pos = np.arange(L)
    half = GLA_DK // 4
    inv = ROPE_BASE ** (-np.arange(half, dtype=np.float64) / half)
    lane = np.arange(2 * GLA_DK)
    jj = lane % (GLA_DK // 2)
    use_col = (lane % GLA_DK) >= GLA_DK // 2
    p = np.where(use_col[None, :], (pos % GRID_W)[:, None], (pos // GRID_W)[:, None]).astype(np.float64)
    ang = p * inv[jj % half][None, :]
    first = jj < half
    cos = np.cos(ang)
    sin = np.where(first[None, :], -np.sin(ang), np.sin(ang))
    return jnp.asarray(cos, F32), jnp.asarray(sin, F32)


def _softmax_pv(s_parts, v_parts):
    m = s_parts[0].max(axis=-1, keepdims=True)
    for s in s_parts[1:]:
        m = jnp.maximum(m, s.max(axis=-1, keepdims=True))
    den = None
    acc = None
    for s, v in zip(s_parts, v_parts):
        p = jnp.exp(s - m)
        den = p.sum(axis=-1, keepdims=True) if den is None else den + p.sum(axis=-1, keepdims=True)
        pv = _dot(p.astype(BF16), v)
        acc = pv if acc is None else acc + pv
    return acc / den


def _na_window_start():
    cols = np.arange(GRID_W)
    return np.clip(cols - NA_WIN_W // 2, 0, GRID_W - NA_WIN_W)


def _na_kernel(q_ref, k_ref, v_ref, ck_ref, cv_ref, bias_ref, o_ref):
    W = GRID_W
    rows = k_ref.shape[1] // W
    n_loc = bias_ref.shape[2]
    kh = n_loc // W
    r0 = pl.program_id(1) * NA_ROWS_PER_STEP
    pair = 2 * NA_DH
    lane = lax.broadcasted_iota(jnp.int32, (1, pair), 1)
    head0 = lane < NA_DH
    scale = jnp.asarray(NA_DH ** -0.5, q_ref.dtype)

    n_pairs = NA_HEADS // 2
    lanes = [slice(p * pair, (p + 1) * pair) for p in range(n_pairs)]

    def rows_body(it, carry):
        units = []
        for j in range(NA_ROWS_PER_ITER):
            rr = it * NA_ROWS_PER_ITER + j
            r = r0 + rr
            rs = jnp.clip(r - kh // 2, 0, rows - kh)
            qs = pl.ds(pl.multiple_of(rr * W, W), W)
            ks = pl.ds(pl.multiple_of(rs * W, W), n_loc)
            units += [(qs, ks, rs - r + NA_WIN_H - 1, p) for p in range(n_pairs)]
        sts = []
        for qs, ks, dr, p in units:
            q = q_ref[0, qs, lanes[p]] * scale
            zero = jnp.zeros_like(q)
            q2 = jnp.concatenate([jnp.where(head0, q, zero), jnp.where(head0, zero, q)], axis=0)
            keys = jnp.concatenate([k_ref[0, ks, lanes[p]], ck_ref[0, :, lanes[p]]], axis=0)
            sts.append(_dot_nt(keys, q2))
        pts = []
        for (qs, ks, dr, p), st in zip(units, sts):
            st = jnp.concatenate([st[:n_loc] + bias_ref[p, dr], st[n_loc:]], axis=0)
            pts.append(jnp.exp((st - st.max(axis=0, keepdims=True)).astype(BF16)))
        outs = []
        ones = jnp.ones((n_loc + ck_ref.shape[1], pair), BF16)
        for (qs, ks, dr, p), pt in zip(units, pts):
            vals = jnp.concatenate([v_ref[0, ks, lanes[p]], cv_ref[0, :, lanes[p]]], axis=0)
            pv = _dot_tn(pt, jnp.concatenate([vals, ones], axis=1))
            o2 = pv[:, :pair] / pv[:, pair:]
            outs.append(jnp.where(head0, o2[:W], o2[W:]))
        for j in range(NA_ROWS_PER_ITER):
            qs = units[j * n_pairs][0]
            o_ref[0, qs, :] = jnp.concatenate(outs[j * n_pairs:(j + 1) * n_pairs], axis=1).astype(o_ref.dtype)
        return carry

    lax.fori_loop(0, NA_ROWS_PER_STEP // NA_ROWS_PER_ITER, rows_body, 0, unroll=4)


def _na(nq, nk, nv, cnk, cnv, bias):
    B, L, n = nq.shape
    Lc = cnk.shape[1]
    tq = NA_ROWS_PER_STEP * GRID_W
    full = lambda rows: pl.BlockSpec((1, rows, n), lambda b, i: (b, 0, 0))
    est = 2 * (2 * L * n * 2 + 2 * Lc * n * 2 + bias.size * 4 + 2 * tq * n * 2) + (8 << 20)
    return pl.pallas_call(
        _na_kernel,
        grid=(B, L // tq),
        in_specs=[
            pl.BlockSpec((1, tq, n), lambda b, i: (b, i, 0)),
            full(L), full(L), full(Lc), full(Lc),
            pl.BlockSpec(bias.shape, lambda b, i: (0, 0, 0, 0)),
        ],
        out_specs=pl.BlockSpec((1, tq, n), lambda b, i: (b, i, 0)),
        out_shape=jax.ShapeDtypeStruct((B, L, n), BF16),
        compiler_params=_params(("parallel", "parallel"), est),
        name="neighbourhood_attention",
    )(nq, nk, nv, cnk, cnv, bias)


def _na_bias_table(rpb, rows):
    kh = min(NA_WIN_H, rows)
    start = _na_window_start()
    kc = np.arange(GRID_W)
    inside = (kc[None, :] >= start[:, None]) & (kc[None, :] < start[:, None] + NA_WIN_W)
    sel = np.zeros((2 * NA_WIN_W - 1, GRID_W, GRID_W), np.float32)
    qq, kk = np.nonzero(inside)
    sel[kk - qq + NA_WIN_W - 1, qq, kk] = 1.0
    by_row = jnp.stack([rpb[:, d:d + kh, :] for d in range(NA_WIN_H)], axis=1)
    t = jnp.einsum('hdic,cqk->hdikq', by_row, jnp.asarray(sel), precision=lax.Precision.HIGHEST)
    t = t + jnp.asarray(np.where(inside, 0.0, MASK_VALUE).T, F32)[None, None, None, :, :]
    t = t.reshape(NA_HEADS // 2, 2, NA_WIN_H, kh * GRID_W, GRID_W)
    return jnp.transpose(t, (0, 2, 3, 1, 4)).reshape(NA_HEADS // 2, NA_WIN_H, kh * GRID_W, 2 * GRID_W)


def _ctx_attn_kernel(q_ref, k_ref, v_ref, o_ref):
    pair = 2 * NA_DH
    lane = lax.broadcasted_iota(jnp.int32, (1, pair), 1)
    head0 = lane < NA_DH
    Lc = q_ref.shape[1]
    outs = []
    for p in range(NA_HEADS // 2):
        ls = slice(p * pair, (p + 1) * pair)
        q = q_ref[0, :, ls]
        zero = jnp.zeros_like(q)
        q2 = jnp.concatenate([jnp.where(head0, q, zero), jnp.where(head0, zero, q)], axis=0)
        s = _dot_nt(q2, k_ref[0, :, ls]) * (NA_DH ** -0.5)
        o2 = _softmax_pv([s], [v_ref[0, :, ls]])
        outs.append(jnp.where(head0, o2[:Lc], o2[Lc:]))
    o_ref[0] = jnp.concatenate(outs, axis=1).astype(o_ref.dtype)


def _ctx_attn(cnq, cnk, cnv):
    B, Lc, n = cnq.shape
    spec = pl.BlockSpec((1, Lc, n), lambda b: (b, 0, 0))
    return pl.pallas_call(
        _ctx_attn_kernel,
        grid=(B,),
        in_specs=[spec, spec, spec],
        out_specs=spec,
        out_shape=jax.ShapeDtypeStruct((B, Lc, n), BF16),
        compiler_params=_params(("parallel",), 8 * Lc * n * 2 + (8 << 20)),
        name="context_attention",
    )(cnq, cnk, cnv)


def _pack_bf16_pairs(x):
    n = x.shape[1] // 2
    lo = lax.bitcast_convert_type(x[:, :n].astype(BF16).astype(F32), jnp.int32)
    hi = lax.bitcast_convert_type(x[:, n:].astype(BF16).astype(F32), jnp.int32)
    return lax.shift_right_logical(lo, 16) | (hi & jnp.int32(-65536))


def _unpack_bf16_pairs(p):
    lo = lax.bitcast_convert_type(lax.shift_left(p, 16), F32)
    hi = lax.bitcast_convert_type(p & jnp.int32(-65536), F32)
    return jnp.concatenate([lo, hi], axis=1)


def _store_packed(refs, x):
    n = 2 * SC_ROW_WORDS
    for p, ref in enumerate(refs):
        ref[...] = _pack_bf16_pairs(x[:, p * n:(p + 1) * n]).reshape(ref.shape)


def _load_packed(refs):
    return jnp.concatenate([_unpack_bf16_pairs(ref[...].reshape(ref.shape[-2:])) for ref in refs], axis=1)


def _route(logits):
    lane = lax.broadcasted_iota(jnp.int32, logits.shape, 1)
    big = jnp.int32(logits.shape[1])
    t1 = logits.max(axis=-1, keepdims=True)
    i1 = jnp.where(logits == t1, lane, big).min(axis=-1, keepdims=True)
    rest = jnp.where(lane == i1, -jnp.inf, logits)
    t2 = rest.max(axis=-1, keepdims=True)
    i2 = jnp.where(rest == t2, lane, big).min(axis=-1, keepdims=True)
    e2 = jnp.exp(t2 - t1)
    return i1, i2, 1.0 / (1.0 + e2), e2 / (1.0 + e2)


def _outproj_kernel(*refs, dense_ffn):
    ga_ref, na_ref, x_ref, mod_ref, wa_ref, wb_ref, gp_ref, gf_ref = refs[:8]
    m = mod_ref[0]
    y = _dot(ga_ref[0], wa_ref[...]) + _dot(na_ref[0], wb_ref[...])
    x1 = x_ref[0] + m[2:3] * (_rms(y) * gp_ref[...])
    h2 = _rms(x1) * gf_ref[...] * (1.0 + m[4:5]) + m[3:4]
    if dense_ffn:
        wg_ref, wu_ref, wd_ref, g2_ref, o_ref = refs[8:]
        h = h2.astype(BF16)
        hid = (_silu(_dot(h, wg_ref[...])) * _dot(h, wu_ref[...])).astype(BF16)
        o_ref[0] = x1 + m[5:6] * (_rms(_dot(hid, wd_ref[...])) * g2_ref[...])
    else:
        wr_ref, x1_ref, gw_ref, eid_ref, cnt_ref, *h2_refs = refs[8:]
        x1_ref[0] = x1
        _store_packed(h2_refs, h2)
        tm = h2.shape[0]
        lanes = wr_ref.shape[1] // 2
        prod = _dot(jnp.concatenate(_split(h2), axis=0), wr_ref[...])
        logits = (prod[:tm, :lanes] + prod[tm:, :lanes]) + (prod[:tm, lanes:] + prod[tm:, lanes:])
        lane = lax.broadcasted_iota(jnp.int32, (tm, lanes), 1)
        logits = jnp.where(lane < N_EXPERTS, logits, -jnp.inf)
        i1, i2, w1, w2 = _route(logits)
        gw_ref[0] = jnp.where(lane == 0, w1, jnp.where(lane == 1, w2, 0.0))
        eid_ref[0] = jnp.where(lane == 0, i1, jnp.where(lane == 1, i2, 0))
        chosen = jnp.where(lane == i1, 1.0, jnp.where(lane == i2, 1.0, 0.0))
        cnt_ref[0] = jnp.broadcast_to(chosen.sum(axis=0, keepdims=True), cnt_ref.shape[1:])


def _outproj(ga, na, x, mod, mod_row0, w_out, g_post, g_ffn, *, ffn=None, w_router=None):
    nb, rows, d = x.shape
    tm = min(TOKEN_TILE, rows)
    dense_ffn = ffn is not None
    tok = lambda n: pl.BlockSpec((1, tm, n), lambda b, i: (b, i, 0))
    const = lambda shape: pl.BlockSpec(shape, lambda b, i: (0,) * len(shape))
    once = lambda shape: pl.BlockSpec(shape, lambda b, i: (0,) * len(shape), pipeline_mode=pl.Buffered(1))
    wa = w_out[:GLA_WIDTH].astype(BF16)
    wb = w_out[GLA_WIDTH:].astype(BF16)
    args = [ga, na, x, mod, wa, wb, g_post.reshape(1, d), g_ffn.reshape(1, d)]
    in_specs = [tok(GLA_WIDTH), tok(NA_WIDTH), tok(d),
                pl.BlockSpec((1, 6, d), lambda b, i: (b + mod_row0, 0, 0)),
                const(wa.shape), const(wb.shape), const((1, d)), const((1, d))]
    est = 2 * (tm * d * (4 + 4 + 2) + 2 * tm * GLA_WIDTH * 2 + d * d * 2) + 4 * tm * d * 4
    if dense_ffn:
        wg, wu, wd, g2 = ffn
        dff = wg.shape[1]
        args += [wg, wu, wd, g2.reshape(1, d)]
        in_specs += [once(wg.shape), once(wu.shape), once(wd.shape), const((1, d))]
        out_specs = tok(d)
        out_shape = jax.ShapeDtypeStruct((nb, rows, d), F32)
        est += 3 * d * dff * 2 + 3 * tm * dff * 4
    else:
        nt = rows // tm
        wr = jnp.zeros((d, V7X_LANES), F32).at[:, :N_EXPERTS].set(w_router)
        wr = jnp.concatenate(_split(wr), axis=1)
        args.append(wr)
        in_specs.append(const(wr.shape))
        n_parts = d // (2 * SC_ROW_WORDS)
        out_specs = [tok(d), tok(V7X_LANES), tok(V7X_LANES),
                     pl.BlockSpec((1, 8, V7X_LANES), lambda b, i: (b * nt + i, 0, 0))] + [tok(SC_ROW_WORDS)] * n_parts
        out_shape = [jax.ShapeDtypeStruct((nb, rows, d), F32),
                     jax.ShapeDtypeStruct((nb, rows, V7X_LANES), F32),
                     jax.ShapeDtypeStruct((nb, rows, V7X_LANES), jnp.int32),
                     jax.ShapeDtypeStruct((nb * nt, 8, V7X_LANES), F32)]
        out_shape += [jax.ShapeDtypeStruct((nb, rows, SC_ROW_WORDS), jnp.int32)] * n_parts
    return pl.pallas_call(
        functools.partial(_outproj_kernel, dense_ffn=dense_ffn),
        grid=(nb, rows // tm),
        in_specs=in_specs,
        out_specs=out_specs,
        out_shape=out_shape,
        compiler_params=_params(("parallel", "parallel"), est),
        name="outproj_ffn" if dense_ffn else "outproj_router",
    )(*args)


def _slot_kernel(eid_ref, base_ref, pos_ref):
    eid = eid_ref[...]
    tm, lanes = eid.shape
    lane = lax.broadcasted_iota(jnp.int32, (tm, lanes), 1)
    i1 = eid[:, 0:1]
    i2 = eid[:, 1:2]
    chosen = jnp.where(lane == i1, 1.0, jnp.where(lane == i2, 1.0, 0.0)).astype(BF16)
    row = lax.broadcasted_iota(jnp.int32, (tm, tm), 0)
    col = lax.broadcasted_iota(jnp.int32, (tm, tm), 1)
    incl = jnp.where(row >= col, 1.0, 0.0).astype(BF16)
    slot = base_ref[0][0:1] + _dot(incl, chosen) - 1.0
    p1 = jnp.where(lane == i1, slot, 0.0).sum(axis=-1, keepdims=True).astype(jnp.int32)
    p2 = jnp.where(lane == i2, slot, 0.0).sum(axis=-1, keepdims=True).astype(jnp.int32)
    pos_ref[...] = jnp.where(lane == 0, p1, jnp.where(lane == 1, p2, 0))


def _slots(eid, tile_base, tm):
    t, lanes = eid.shape
    return pl.pallas_call(
        _slot_kernel,
        grid=(t // tm,),
        in_specs=[pl.BlockSpec((tm, lanes), lambda i: (i, 0)),
                  pl.BlockSpec((1, 8, lanes), lambda i: (i, 0, 0))],
        out_specs=pl.BlockSpec((tm, lanes), lambda i: (i, 0)),
        out_shape=jax.ShapeDtypeStruct((t, lanes), jnp.int32),
        compiler_params=_params(("parallel",), 8 * tm * lanes * 4 + 4 * tm * tm),
        name="moe_slots",
    )(eid, tile_base)


def _route_plan(cnt, tm_tokens):
    counts = cnt[:, 0, :N_EXPERTS].astype(jnp.int32)
    total = counts.sum(axis=0)
    padded = -(-total // MOE_ROW_TILE) * MOE_ROW_TILE
    start = jnp.cumsum(padded) - padded
    before = jnp.cumsum(counts, axis=0) - counts
    tile_base = (start[None, :] + before).astype(F32)
    tile_base = jnp.zeros((cnt.shape[0], 8, V7X_LANES), F32).at[:, :, :N_EXPERTS].set(tile_base[:, None, :])
    n_slots = tm_tokens * 2 + N_EXPERTS * MOE_ROW_TILE
    first_row = jnp.arange(n_slots // MOE_ROW_TILE, dtype=jnp.int32) * MOE_ROW_TILE
    tile_expert = jnp.minimum((first_row[:, None] >= (start + padded)[None, :]).sum(axis=1), N_EXPERTS - 1)
    n_valid = jnp.clip(start[tile_expert] + total[tile_expert] - first_row, 0, MOE_ROW_TILE)
    return tile_base, tile_expert.astype(jnp.int32), n_valid.astype(jnp.int32), n_slots


def _sc_mesh():
    return plsc.VectorSubcoreMesh(core_axis_name="core", subcore_axis_name="subcore")


def _scatter_rows(x, idx, n_out):
    t, w = x.shape
    n = idx.shape[0]
    win = SC_ROW_WINDOW
    n_blk = t // win

    @functools.partial(pl.kernel, out_type=jax.ShapeDtypeStruct((n_out, w), x.dtype), mesh=_sc_mesh(),
                       scratch_types=[], name="moe_dispatch")
    def scatter(x_hbm, i_hbm, o_hbm):
        def body(x_vmem, i_vmem):
            pltpu.sync_copy(x_vmem, o_hbm.at[i_vmem.at[0]])

        pltpu.emit_pipeline(
            body,
            grid=(n // win,),
            in_specs=[pl.BlockSpec((win, w), lambda i: (i % n_blk, 0)),
                      pl.BlockSpec((1, win), lambda i: (0, i))],
            out_specs=[],
            core_axis_name=("core", "subcore"),
            dimension_semantics=(pltpu.PARALLEL,),
        )(x_hbm, i_hbm)

    return scatter(x, idx.reshape(1, n))


def _gather_rows(x, idx):
    n = idx.shape[0]
    w = x.shape[1]
    win = SC_ROW_WINDOW

    @functools.partial(pl.kernel, out_type=jax.ShapeDtypeStruct((n, w), x.dtype), mesh=_sc_mesh(),
                       scratch_types=[], name="moe_combine_gather")
    def gather(x_hbm, i_hbm, o_hbm):
        def body(i_vmem, o_vmem):
            pltpu.sync_copy(x_hbm.at[i_vmem.at[0]], o_vmem)

        pltpu.emit_pipeline(
            body,
            grid=(n // win,),
            in_specs=[pl.BlockSpec((1, win), lambda i: (0, i))],
            out_specs=[pl.BlockSpec((win, w), lambda i: (i, 0))],
            core_axis_name=("core", "subcore"),
            dimension_semantics=(pltpu.PARALLEL,),
        )(i_hbm, o_hbm)

    return gather(x, idx.reshape(1, n))


def _experts_kernel(te_ref, nv_ref, *refs, n_parts):
    x_refs = refs[:n_parts]
    wg_ref, wu_ref, wd_ref = refs[n_parts:n_parts + 3]
    y_refs = refs[n_parts + 3:2 * n_parts + 3]
    i = pl.program_id(0)
    n_valid = nv_ref[i]
    tm = x_refs[0].shape[0]
    dff = wg_ref.shape[2]

    @pl.when(n_valid > 0)
    def _():
        row = lax.broadcasted_iota(jnp.int32, (tm, 1), 0)
        x = jnp.where(row < n_valid, _load_packed(x_refs), 0.0).astype(BF16)
        acc = None
        for lo in range(0, dff, MOE_FF_CHUNK):
            sl = slice(lo, lo + MOE_FF_CHUNK)
            hid = (_silu(_dot(x, wg_ref[0, :, sl])) * _dot(x, wu_ref[0, :, sl])).astype(BF16)
            part = _dot(hid, wd_ref[0, sl, :])
            acc = part if acc is None else acc + part
        _store_packed(y_refs, acc)

    @pl.when(n_valid == 0)
    def _():
        for ref in y_refs:
            ref[...] = jnp.zeros_like(ref)


def _experts(xs_parts, tile_expert, n_valid, wg, wu, wd):
    n_parts = len(xs_parts)
    n_slots, words = xs_parts[0].shape
    n_e, d, dff = wg.shape
    tm = MOE_ROW_TILE
    rows_spec = pl.BlockSpec((tm, words), lambda i, te, nv: (i, 0))
    weights = lambda shape: pl.BlockSpec((1,) + shape, lambda i, te, nv: (te[i], 0, 0))
    est = 2 * 3 * d * dff * 2 + 4 * tm * d * 2 + 3 * tm * MOE_FF_CHUNK * 4 + 3 * tm * d * 4
    grid_spec = pltpu.PrefetchScalarGridSpec(
        num_scalar_prefetch=2,
        grid=(n_slots // tm,),
        in_specs=[rows_spec] * n_parts + [weights((d, dff)), weights((d, dff)), weights((dff, d))],
        out_specs=[rows_spec] * n_parts,
    )
    return pl.pallas_call(
        functools.partial(_experts_kernel, n_parts=n_parts),
        grid_spec=grid_spec,
        out_shape=[jax.ShapeDtypeStruct((n_slots, words), jnp.int32)] * n_parts,
        compiler_params=_params(("arbitrary",), est),
        name="moe_experts",
    )(tile_expert, n_valid, *xs_parts, wg, wu, wd)


def _combine_kernel(*refs, n_parts):
    y1_refs = refs[:n_parts]
    y2_refs = refs[n_parts:2 * n_parts]
    gw_ref, x_ref, mod_ref, g_ref, o_ref = refs[2 * n_parts:]
    gw = gw_ref[0]
    y = gw[:, 0:1] * _load_packed(y1_refs) + gw[:, 1:2] * _load_packed(y2_refs)
    o_ref[0] = x_ref[0] + mod_ref[0][5:6] * (_rms(y) * g_ref[...])


def _combine(ys2_parts, gw, x1, mod, mod_row0, g_post):
    n_parts = len(ys2_parts)
    nb, rows, d = x1.shape
    words = ys2_parts[0].shape[-1]
    tm = min(TOKEN_TILE, rows)
    tok = lambda n: pl.BlockSpec((1, tm, n), lambda b, i: (b, i, 0))
    ysp = lambda k: pl.BlockSpec((1, tm, words), lambda b, i: (k * nb + b, i, 0))
    est = 2 * tm * (d * 2 * 4 + d * 4 + V7X_LANES * 4) + 3 * tm * d * 4
    return pl.pallas_call(
        functools.partial(_combine_kernel, n_parts=n_parts),
        grid=(nb, rows // tm),
        in_specs=[ysp(0)] * n_parts + [ysp(1)] * n_parts + [
            tok(V7X_LANES), tok(d),
            pl.BlockSpec((1, 6, d), lambda b, i: (b + mod_row0, 0, 0)),
            pl.BlockSpec((1, d), lambda b, i: (0, 0))],
        out_specs=tok(d),
        out_shape=jax.ShapeDtypeStruct((nb, rows, d), F32),
        compiler_params=_params(("parallel", "parallel"), est),
        name="moe_combine",
    )(*ys2_parts, *ys2_parts, gw, x1, mod, g_post.reshape(1, d))


def _moe_routed(gw, eid, cnt, *h2_parts, x1, mod, mod_row0, experts, g_post):
    nb, rows, d = x1.shape
    t = nb * rows
    tm = min(TOKEN_TILE, rows)
    tile_base, tile_expert, n_valid, n_slots = _route_plan(cnt, t)
    pos = _slots(eid.reshape(t, V7X_LANES), tile_base, tm)
    idx = jnp.concatenate([pos[:, 0], pos[:, 1]])
    xs = [_scatter_rows(h.reshape(t, h.shape[-1]), idx, n_slots) for h in h2_parts]
    ys = _experts(xs, tile_expert, n_valid, *experts)
    ys2 = [_gather_rows(y, idx).reshape(2 * nb, rows, y.shape[-1]) for y in ys]
    return _combine(ys2, gw, x1, mod, mod_row0, g_post)


def _cast_kernel(w_ref, o_ref):
    o_ref[...] = w_ref[...].astype(o_ref.dtype)


def _to_bf16(w):
    shape = w.shape
    w3 = w.reshape((-1,) + shape[-2:])
    n, r, c = w3.shape
    rb = r
    while rb * c * 4 > CAST_BLOCK_BYTES and rb % 16 == 0:
        rb //= 2
    spec = pl.BlockSpec((1, rb, c), lambda e, i: (e, i, 0))
    out = pl.pallas_call(
        _cast_kernel,
        grid=(n, r // rb),
        in_specs=[spec],
        out_specs=spec,
        out_shape=jax.ShapeDtypeStruct(w3.shape, BF16),
        compiler_params=_params(("parallel", "parallel"), 2 * rb * c * 6),
        name="cast_bf16",
    )(w3)
    return out.reshape(shape)
def _pack_w_in(w):
    a = 2 * GLA_KDIM + 2 * GLA_WIDTH
    lr = 2 * GLA_GATE_RANK
    reps = LR_PAD // lr
    return jnp.concatenate([w[:, :a], w[:, a + lr:]] + [w[:, a:a + lr]] * reps, axis=1).astype(BF16)


def _pack_gate(w_gate, b_gate):
    lr = 2 * GLA_GATE_RANK
    packed = []
    for d in range(2):
        wd = jnp.zeros((lr, GLA_KDIM), F32).at[d * GLA_GATE_RANK:(d + 1) * GLA_GATE_RANK].set(w_gate[d])
        hi, lo = _split(wd)
        packed.append(jnp.concatenate([hi, hi, lo, lo], axis=0))
    return jnp.stack(packed), b_gate.reshape(2, 1, GLA_KDIM)


def kernel(x, c, ctx, c_ctx, w_ada, b_ada, g_pre_mix, g_post_mix, g_pre_ffn, g_post_ffn, w_in,
           gla_w_gate, gla_b_gate, gla_g_norm, na_rpb, w_out, ffn_w_gate, ffn_w_up, ffn_w_down,
           moe_w_router, moe_w_gate, moe_w_up, moe_w_down):
    B, L, D = x.shape
    Lc = ctx.shape[1]
    depth = w_ada.shape[0]
    rows = L // GRID_W
    ctx_row = B
    n_cond = -(-(B + 1) // 8) * 8
    cond = jnp.zeros((n_cond, D), F32).at[:B].set(c).at[B].set(c_ctx)
    mod = _modulation(cond, w_ada, b_ada)
    tables = _rope_tables(L)
    ctx = ctx.reshape(1, B * Lc, D)

    for i in range(depth):
        last = i == depth - 1
        j = i // 2
        w = _pack_w_in(w_in[i])
        wgp, bgp = _pack_gate(gla_w_gate[i], gla_b_gate[i])
        qk, gv, gr, nq, nk, nv, bf, bb = _inproj(x, mod[i], 0, g_pre_mix[i], w, wgp, bgp, tables)
        cparts = _inproj(ctx, mod[i], ctx_row, g_pre_mix[i], w, wgp, bgp)
        cqk, cgv, cgr, cnq, cnk, cnv, cbf, cbb = [t.reshape(B, Lc, t.shape[-1]) for t in cparts]
        ga, gac = _gla(qk, gv, gr, bf, bb, cqk, cgv, cgr, cbf, cbb, gla_g_norm[i])
        na = _na(nq, nk, nv, cnk, cnv, _na_bias_table(na_rpb[i], rows))
        if i % 2 == 0:
            ffn = (_to_bf16(ffn_w_gate[j]), _to_bf16(ffn_w_up[j]), _to_bf16(ffn_w_down[j]), g_post_ffn[i])

            def mix_and_ffn(ga_, na_, x_, row0):
                return _outproj(ga_, na_, x_, mod[i], row0, w_out[i], g_post_mix[i], g_pre_ffn[i], ffn=ffn)
        else:
            experts = (_to_bf16(moe_w_gate[j]), _to_bf16(moe_w_up[j]), _to_bf16(moe_w_down[j]))

            def mix_and_ffn(ga_, na_, x_, row0):
                x1, *routed = _outproj(ga_, na_, x_, mod[i], row0, w_out[i], g_post_mix[i], g_pre_ffn[i],
                                       w_router=moe_w_router[j])
                return _moe_routed(*routed, x1=x1, mod=mod[i], mod_row0=row0, experts=experts, g_post=g_post_ffn[i])
        x = mix_and_ffn(ga, na, x, 0)
        if not last:
            nac = _ctx_attn(cnq, cnk, cnv)
            ctx = mix_and_ffn(gac.reshape(1, B * Lc, GLA_WIDTH), nac.reshape(1, B * Lc, NA_WIDTH), ctx, ctx_row)
    return x
```

```python
import functools

import numpy as np
import jax
import jax.numpy as jnp
from jax import lax
from jax.experimental import pallas as pl
from jax.experimental.pallas import tpu as pltpu
from jax.experimental.pallas import tpu_sc as plsc

F32 = jnp.float32
BF16 = jnp.bfloat16

GRID_W = 64
GLA_HEADS = 4
GLA_DV = 128
GLA_DK = 64
GLA_KDIM = GLA_HEADS * GLA_DK
GLA_WIDTH = GLA_HEADS * GLA_DV
GLA_GATE_RANK = 16
GLA_GATE_NORM = 16.0
NA_HEADS = 8
NA_DH = 64
NA_WIDTH = NA_HEADS * NA_DH
NA_WIN_H = 8
NA_WIN_W = 16
ROPE_BASE = 10000.0
N_EXPERTS = 8
EPS = 1e-6

V7X_LANES = 128
V7X_VMEM_BYTES = 64 * 1024 * 1024
V7X_VMEM_USABLE = V7X_VMEM_BYTES - 8 * 1024 * 1024

TOKEN_TILE = 512
GLA_CHUNK = 128
GLA_UNROLL = 4
NA_ROWS_PER_STEP = 8
NA_ROWS_PER_ITER = 2
MASK_VALUE = -1e30
MOE_FF_CHUNK = 1792
MOE_ROW_TILE = 512
SC_ROW_WINDOW = 128
CAST_BLOCK_BYTES = 4 << 20
SC_ROW_WORDS = 256
LR_PAD = V7X_LANES


def _vmem_limit(estimate_bytes):
    return int(min(V7X_VMEM_USABLE, estimate_bytes * 5 // 4 + (4 << 20)))


def _params(semantics, vmem_estimate):
    return pltpu.CompilerParams(dimension_semantics=semantics, vmem_limit_bytes=_vmem_limit(vmem_estimate))


def _dot(a, b):
    return jnp.dot(a, b, preferred_element_type=F32)


def _dot_nt(a, b):
    return lax.dot_general(a, b, (((1,), (1,)), ((), ())), preferred_element_type=F32)


def _dot_tn(a, b):
    return lax.dot_general(a, b, (((0,), (0,)), ((), ())), preferred_element_type=F32)


def _split(x):
    hi = x.astype(BF16)
    lo = (x - hi.astype(F32)).astype(BF16)
    return hi, lo


def _dot3(a, b):
    ah, al = _split(a)
    bh, bl = _split(b)
    return _dot(ah, bh) + (_dot(al, bh) + _dot(ah, bl))


def _sigmoid(x):
    return 1.0 / (1.0 + jnp.exp(-x))


def _silu(x):
    return x * _sigmoid(x)


def _rms(x):
    return x * lax.rsqrt(jnp.mean(x * x, axis=-1, keepdims=True) + EPS)


def _mod_kernel(c_ref, w_ref, b_ref, o_ref):
    o_ref[0] = _dot3(_silu(c_ref[...]), w_ref[0]) + b_ref[0]


def _modulation(cond, w_ada, b_ada):
    depth, d, n = w_ada.shape
    rows = cond.shape[0]
    tn = 512
    out = pl.pallas_call(
        _mod_kernel,
        grid=(depth, n // tn),
        in_specs=[
            pl.BlockSpec((rows, d), lambda l, j: (0, 0)),
            pl.BlockSpec((1, d, tn), lambda l, j: (l, 0, j)),
            pl.BlockSpec((1, 1, tn), lambda l, j: (l, 0, j)),
        ],
        out_specs=pl.BlockSpec((1, rows, tn), lambda l, j: (l, 0, j)),
        out_shape=jax.ShapeDtypeStruct((depth, rows, n), F32),
        compiler_params=_params(("parallel", "parallel"), 3 * d * tn * 4 * 2),
        name="modulation",
    )(cond, w_ada, b_ada.reshape(depth, 1, n))
    return out.reshape(depth, rows, 6, d)


def _log_decay(logit):
    return (jnp.minimum(logit, 0.0) - jnp.log(1.0 + jnp.exp(-jnp.abs(logit)))) * (1.0 / GLA_GATE_NORM)


def _inproj_kernel(*refs, rope):
    if rope:
        x_ref, mod_ref, g_ref, w_ref, wg_ref, bg_ref, cos_ref, sin_ref = refs[:8]
    else:
        x_ref, mod_ref, g_ref, w_ref, wg_ref, bg_ref = refs[:6]
    qk_ref, gv_ref, gr_ref, nq_ref, nk_ref, nv_ref, bf_ref, bb_ref = refs[-8:]
    m = mod_ref[0]
    h = (_rms(x_ref[0]) * g_ref[...] * (1.0 + m[1:2]) + m[0:1]).astype(BF16)
    tm = h.shape[0]
    plain_refs = (gv_ref, gr_ref, nq_ref, nk_ref, nv_ref)
    offs = np.cumsum([0, qk_ref.shape[-1]] + [r.shape[-1] for r in plain_refs])

    def project(k):
        ref = plain_refs[k]
        ref[0] = _dot(h, w_ref[:, offs[k + 1]:offs[k + 2]]).astype(ref.dtype)

    project(0)
    lr = _dot(h, w_ref[:, offs[-1]:offs[-1] + LR_PAD])
    lr_hi, lr_lo = _split(lr)
    lr_lane = lax.broadcasted_iota(jnp.int32, (1, LR_PAD), 1)
    lr_mix = jnp.where((lr_lane // (2 * GLA_GATE_RANK)) % 2 == 0, lr_hi, lr_lo)
    logits = [_dot(lr_mix, wg_ref[d]) + bg_ref[d] for d in range(2)]
    n = qk_ref.shape[-1]
    qk = _dot(h, w_ref[:, 0:n])
    lane = lax.broadcasted_iota(jnp.int32, (1, n), 1)
    if rope:
        reps = n // cos_ref.shape[-1]
        cos = jnp.concatenate([cos_ref[...]] * reps, axis=1)
        sin = jnp.concatenate([sin_ref[...]] * reps, axis=1)
        quarter = GLA_DK // 4
        first = (lane % (2 * quarter)) < quarter
        partner = jnp.where(first, pltpu.roll(qk, n - quarter, 1), pltpu.roll(qk, quarter, 1))
        qk = qk * cos + partner * sin
    qk_ref[0] = jnp.where(lane < GLA_KDIM, qk * (GLA_DK ** -0.5), qk).astype(qk_ref.dtype)
    C = GLA_CHUNK
    row = lax.broadcasted_iota(jnp.int32, (C, 2 * C), 0)
    colm = lax.broadcasted_iota(jnp.int32, (C, 2 * C), 1) % C
    for d, out_ref in enumerate((bf_ref, bb_ref)):
        gh, gl = _split(_log_decay(logits[d]))
        project(1 + 2 * d)
        tri = (row >= colm) if d == 0 else (row <= colm)
        tmat = jnp.where(tri, 1.0, 0.0).astype(BF16)
        for c in range(tm // C):
            sl = slice(c * C, (c + 1) * C)
            out_ref[0, sl, :] = _dot(tmat, jnp.concatenate([gh[sl], gl[sl]], axis=0))
        project(2 + 2 * d)


def _inproj(x, mod, mod_row0, g, w, wgp, bgp, rope_tables=None):
    nb, rows, d = x.shape
    tm = min(TOKEN_TILE, rows)
    rope = rope_tables is not None
    widths = (2 * GLA_KDIM, GLA_WIDTH, GLA_WIDTH, NA_WIDTH, NA_WIDTH, NA_WIDTH, GLA_KDIM, GLA_KDIM)
    dtypes = (BF16,) * 6 + (F32, F32)
    tok = lambda n: pl.BlockSpec((1, tm, n), lambda b, i: (b, i, 0))
    const = lambda shape: pl.BlockSpec(shape, lambda b, i: (0,) * len(shape))
    args = [x, mod, g.reshape(1, d), w, wgp, bgp]
    in_specs = [tok(d), pl.BlockSpec((1, 6, d), lambda b, i: (b + mod_row0, 0, 0)), const((1, d)),
                const(w.shape), const(wgp.shape), const(bgp.shape)]
    if rope:
        args += list(rope_tables)
        in_specs += [pl.BlockSpec((tm, t.shape[1]), lambda b, i: (i, 0)) for t in rope_tables]
    est = 2 * (tm * d * 4 + d * w.shape[1] * 2 + sum(widths) * tm * 4) + tm * d * 8
    return pl.pallas_call(
        functools.partial(_inproj_kernel, rope=rope),
        grid=(nb, rows // tm),
        in_specs=in_specs,
        out_specs=[tok(n) for n in widths],
        out_shape=[jax.ShapeDtypeStruct((nb, rows, n), dt) for n, dt in zip(widths, dtypes)],
        compiler_params=_params(("parallel", "parallel"), est),
        name="inproj",
    )(*args)


def _gla_kernel(q_ref, k_ref, v_ref, r_ref, bf_ref, bb_ref, cq_ref, ck_ref, cv_ref, cr_ref, cbf_ref, cbb_ref,
                gain_ref, o_ref, oc_ref, of_ref, ocf_ref, st_ref):
    C = GLA_CHUNK
    L = q_ref.shape[1]
    Lc = cq_ref.shape[1]
    row = lax.broadcasted_iota(jnp.int32, (C, C), 0)
    col = lax.broadcasted_iota(jnp.int32, (C, C), 1)
    lane = lax.broadcasted_iota(jnp.int32, (1, 2 * GLA_DK), 1)
    head0 = lane < GLA_DK
    st_row = lax.broadcasted_iota(jnp.int32, (2 * GLA_DV, 2 * GLA_DK), 0) // GLA_DV
    st_col = lax.broadcasted_iota(jnp.int32, (2 * GLA_DV, 2 * GLA_DK), 1) // GLA_DK
    blockdiag = st_row == st_col
    gain = gain_ref[...]

    def chunk_steps(refs, chunks, acc_ref):
        rq, rk, rv = refs[:3]
        dirs = (0, 1)
        sls = [pl.ds(pl.multiple_of(i * C, C), C) for i in chunks]
        tris = [row >= col, row <= col]
        vs = [rv[0, sl, :] for sl in sls]
        qfs = [rq[0, sl, :].astype(F32) for sl in sls]
        kfs = [rk[0, sl, :].astype(F32) for sl in sls]
        bs = [refs[3 + d][0, sls[d], :] for d in dirs]
        b_mids = [b[C // 2:C // 2 + 1] for b in bs]
        b_edges = [bs[0][C - 1:C], bs[1][0:1]]
        qes = [(qfs[d] * jnp.exp(bs[d] - b_mids[d])).astype(BF16) for d in dirs]
        kes = [(kfs[d] * jnp.exp(b_mids[d] - bs[d])).astype(BF16) for d in dirs]
        zero = jnp.zeros_like(qes[0])
        lhss = [jnp.concatenate([jnp.where(head0, qe, zero), jnp.where(head0, zero, qe)], axis=0) for qe in qes]
        scores = [_dot_nt(lhss[d], kes[d]) for d in dirs]
        qbs = [(qfs[d] * jnp.exp(bs[d])).astype(BF16) for d in dirs]
        kds = [(kfs[d] * jnp.exp(b_edges[d] - bs[d])).astype(BF16) for d in dirs]
        sts = [st_ref[d] for d in dirs]
        inter = [_dot_nt(qbs[d], sts[d].astype(BF16)) for d in dirs]
        dss = [_dot_tn(vs[d], kds[d]) for d in dirs]
        ams = [jnp.where(jnp.concatenate([tris[d], tris[d]], axis=0), scores[d], 0.0).astype(BF16) for d in dirs]
        intra = [jnp.concatenate([_dot(ams[d][:C], vs[d][:, :GLA_DV]), _dot(ams[d][C:], vs[d][:, GLA_DV:])], axis=1)
                 for d in dirs]
        for d in dirs:
            st_ref[d] = sts[d] * jnp.exp(b_edges[d]) + jnp.where(blockdiag, dss[d], 0.0)
            acc_ref[d, sls[d], :] = intra[d] + inter[d]

    def finish(total, r):
        outs = []
        for h in range(2):
            oh = _rms(total[:, h * GLA_DV:(h + 1) * GLA_DV]) * gain
            outs.append(oh * _silu(r[:, h * GLA_DV:(h + 1) * GLA_DV].astype(F32)))
        return jnp.concatenate(outs, axis=1)

    def scan(refs, n, acc_ref, r_ref_, out_ref):
        def body(j, carry):
            chunk_steps(refs, (j, n - 1 - j), acc_ref)
            return carry

        lax.fori_loop(0, n, body, 0, unroll=min(GLA_UNROLL, n))

        def fin(i, carry):
            sl = pl.ds(pl.multiple_of(i * C, C), C)
            out_ref[0, sl, :] = finish(acc_ref[0, sl, :] + acc_ref[1, sl, :], r_ref_[0, sl, :]).astype(out_ref.dtype)
            return carry

        lax.fori_loop(0, n, fin, 0, unroll=min(GLA_UNROLL, n))

    st_ref[...] = jnp.zeros_like(st_ref)
    scan((cq_ref, ck_ref, cv_ref, cbf_ref, cbb_ref), Lc // C, ocf_ref, cr_ref, oc_ref)
    scan((q_ref, k_ref, v_ref, bf_ref, bb_ref), L // C, of_ref, r_ref, o_ref)


def _gla(qk, gv, gr, bf, bb, cqk, cgv, cgr, cbf, cbb, gain):
    B, L, _ = qk.shape
    Lc = cqk.shape[1]
    pair = 2 * GLA_DK
    pv = 2 * GLA_DV
    n_pair = GLA_HEADS // 2
    seq = lambda rows, n, off: pl.BlockSpec((1, rows, n), lambda b, p: (b, 0, p + off))
    const = lambda shape: pl.BlockSpec(shape, lambda b, p: (0,) * len(shape))
    est = (2 * (2 * L * pair * 2 + 2 * L * pv * 2 + 2 * L * pair * 4 + L * pv * 2)
           + 2 * L * pv * 4 + (8 << 20))
    return pl.pallas_call(
        _gla_kernel,
        grid=(B, n_pair),
        in_specs=[
            seq(L, pair, 0), seq(L, pair, n_pair), seq(L, pv, 0), seq(L, pv, 0), seq(L, pair, 0), seq(L, pair, 0),
            seq(Lc, pair, 0), seq(Lc, pair, n_pair), seq(Lc, pv, 0), seq(Lc, pv, 0), seq(Lc, pair, 0),
            seq(Lc, pair, 0),
            const((1, GLA_DV)),
        ],
        out_specs=[seq(L, pv, 0), seq(Lc, pv, 0)],
        out_shape=[jax.ShapeDtypeStruct((B, L, GLA_WIDTH), BF16), jax.ShapeDtypeStruct((B, Lc, GLA_WIDTH), BF16)],
        scratch_shapes=[pltpu.VMEM((2, L, pv), F32), pltpu.VMEM((2, Lc, pv), F32), pltpu.VMEM((2, pv, pair), F32)],
        compiler_params=_params(("parallel", "parallel"), est),
        name="gla",
    )(qk, qk, gv, gr, bf, bb, cqk, cqk, cgv, cgr, cbf, cbb, gain.reshape(1, GLA_DV))


def _rope_tables(L):
    pos = np.arange(L)
    half = GLA_DK // 4
    inv = ROPE_BASE ** (-np.arange(half, dtype=np.float64) / half)
    lane = np.arange(2 * GLA_DK)
    jj = lane % (GLA_DK // 2)
    use_col = (lane % GLA_DK) >= GLA_DK // 2
    p = np.where(use_col[None, :], (pos % GRID_W)[:, None], (pos // GRID_W)[:, None]).astype(np.float64)
    ang = p * inv[jj % half][None, :]
    first = jj < half
    cos = np.cos(ang)
    sin = np.where(first[None, :], -np.sin(ang), np.sin(ang))
    return jnp.asarray(cos, F32), jnp.asarray(sin, F32)


def _softmax_pv(s_parts, v_parts):
    m = s_parts[0].max(axis=-1, keepdims=True)
    for s in s_parts[1:]:
        m = jnp.maximum(m, s.max(axis=-1, keepdims=True))
    den = None
    acc = None
    for s, v in zip(s_parts, v_parts):
        p = jnp.exp(s - m)
        den = p.sum(axis=-1, keepdims=True) if den is None else den + p.sum(axis=-1, keepdims=True)
        pv = _dot(p.astype(BF16), v)
        acc = pv if acc is None else acc + pv
    return acc / den


def _na_window_start():
    cols = np.arange(GRID_W)
    return np.clip(cols - NA_WIN_W // 2, 0, GRID_W - NA_WIN_W)


def _na_kernel(q_ref, k_ref, v_ref, ck_ref, cv_ref, bias_ref, o_ref):
    W = GRID_W
    rows = k_ref.shape[1] // W
    n_loc = bias_ref.shape[2]
    kh = n_loc // W
    r0 = pl.program_id(1) * NA_ROWS_PER_STEP
    pair = 2 * NA_DH
    lane = lax.broadcasted_iota(jnp.int32, (1, pair), 1)
    head0 = lane < NA_DH
    scale = jnp.asarray(NA_DH ** -0.5, q_ref.dtype)

    n_pairs = NA_HEADS // 2
    lanes = [slice(p * pair, (p + 1) * pair) for p in range(n_pairs)]

    def rows_body(it, carry):
        units = []
        for j in range(NA_ROWS_PER_ITER):
            rr = it * NA_ROWS_PER_ITER + j
            r = r0 + rr
            rs = jnp.clip(r - kh // 2, 0, rows - kh)
            qs = pl.ds(pl.multiple_of(rr * W, W), W)
            ks = pl.ds(pl.multiple_of(rs * W, W), n_loc)
            units += [(qs, ks, rs - r + NA_WIN_H - 1, p) for p in range(n_pairs)]
        sts = []
        for qs, ks, dr, p in units:
            q = q_ref[0, qs, lanes[p]] * scale
            zero = jnp.zeros_like(q)
            q2 = jnp.concatenate([jnp.where(head0, q, zero), jnp.where(head0, zero, q)], axis=0)
            keys = jnp.concatenate([k_ref[0, ks, lanes[p]], ck_ref[0, :, lanes[p]]], axis=0)
            sts.append(_dot_nt(keys, q2))
        pts = []
        for (qs, ks, dr, p), st in zip(units, sts):
            st = jnp.concatenate([st[:n_loc] + bias_ref[p, dr], st[n_loc:]], axis=0)
            pts.append(jnp.exp((st - st.max(axis=0, keepdims=True)).astype(BF16)))
        outs = []
        ones = jnp.ones((n_loc + ck_ref.shape[1], pair), BF16)
        for (qs, ks, dr, p), pt in zip(units, pts):
            vals = jnp.concatenate([v_ref[0, ks, lanes[p]], cv_ref[0, :, lanes[p]]], axis=0)
            pv = _dot_tn(pt, jnp.concatenate([vals, ones], axis=1))
            o2 = pv[:, :pair] / pv[:, pair:]
            outs.append(jnp.where(head0, o2[:W], o2[W:]))
        for j in range(NA_ROWS_PER_ITER):
            qs = units[j * n_pairs][0]
            o_ref[0, qs, :] = jnp.concatenate(outs[j * n_pairs:(j + 1) * n_pairs], axis=1).astype(o_ref.dtype)
        return carry

    lax.fori_loop(0, NA_ROWS_PER_STEP // NA_ROWS_PER_ITER, rows_body, 0, unroll=4)


def _na(nq, nk, nv, cnk, cnv, bias):
    B, L, n = nq.shape
    Lc = cnk.shape[1]
    tq = NA_ROWS_PER_STEP * GRID_W
    full = lambda rows: pl.BlockSpec((1, rows, n), lambda b, i: (b, 0, 0))
    est = 2 * (2 * L * n * 2 + 2 * Lc * n * 2 + bias.size * 4 + 2 * tq * n * 2) + (8 << 20)
    return pl.pallas_call(
        _na_kernel,
        grid=(B, L // tq),
        in_specs=[
            pl.BlockSpec((1, tq, n), lambda b, i: (b, i, 0)),
            full(L), full(L), full(Lc), full(Lc),
            pl.BlockSpec(bias.shape, lambda b, i: (0, 0, 0, 0)),
        ],
        out_specs=pl.BlockSpec((1, tq, n), lambda b, i: (b, i, 0)),
        out_shape=jax.ShapeDtypeStruct((B, L, n), BF16),
        compiler_params=_params(("parallel", "parallel"), est),
        name="neighbourhood_attention",
    )(nq, nk, nv, cnk, cnv, bias)


def _na_bias_table(rpb, rows):
    kh = min(NA_WIN_H, rows)
    start = _na_window_start()
    kc = np.arange(GRID_W)
    inside = (kc[None, :] >= start[:, None]) & (kc[None, :] < start[:, None] + NA_WIN_W)
    sel = np.zeros((2 * NA_WIN_W - 1, GRID_W, GRID_W), np.float32)
    qq, kk = np.nonzero(inside)
    sel[kk - qq + NA_WIN_W - 1, qq, kk] = 1.0
    by_row = jnp.stack([rpb[:, d:d + kh, :] for d in range(NA_WIN_H)], axis=1)
    t = jnp.einsum('hdic,cqk->hdikq', by_row, jnp.asarray(sel), precision=lax.Precision.HIGHEST)
    t = t + jnp.asarray(np.where(inside, 0.0, MASK_VALUE).T, F32)[None, None, None, :, :]
    t = t.reshape(NA_HEADS // 2, 2, NA_WIN_H, kh * GRID_W, GRID_W)
    return jnp.transpose(t, (0, 2, 3, 1, 4)).reshape(NA_HEADS // 2, NA_WIN_H, kh * GRID_W, 2 * GRID_W)


def _ctx_attn_kernel(q_ref, k_ref, v_ref, o_ref):
    pair = 2 * NA_DH
    lane = lax.broadcasted_iota(jnp.int32, (1, pair), 1)
    head0 = lane < NA_DH
    Lc = q_ref.shape[1]
    outs = []
    for p in range(NA_HEADS // 2):
        ls = slice(p * pair, (p + 1) * pair)
        q = q_ref[0, :, ls]
        zero = jnp.zeros_like(q)
        q2 = jnp.concatenate([jnp.where(head0, q, zero), jnp.where(head0, zero, q)], axis=0)
        s = _dot_nt(q2, k_ref[0, :, ls]) * (NA_DH ** -0.5)
        o2 = _softmax_pv([s], [v_ref[0, :, ls]])
        outs.append(jnp.where(head0, o2[:Lc], o2[Lc:]))
    o_ref[0] = jnp.concatenate(outs, axis=1).astype(o_ref.dtype)


def _ctx_attn(cnq, cnk, cnv):
    B, Lc, n = cnq.shape
    spec = pl.BlockSpec((1, Lc, n), lambda b: (b, 0, 0))
    return pl.pallas_call(
        _ctx_attn_kernel,
        grid=(B,),
        in_specs=[spec, spec, spec],
        out_specs=spec,
        out_shape=jax.ShapeDtypeStruct((B, Lc, n), BF16),
        compiler_params=_params(("parallel",), 8 * Lc * n * 2 + (8 << 20)),
        name="context_attention",
    )(cnq, cnk, cnv)


def _pack_bf16_pairs(x):
    n = x.shape[1] // 2
    lo = lax.bitcast_convert_type(x[:, :n].astype(BF16).astype(F32), jnp.int32)
    hi = lax.bitcast_convert_type(x[:, n:].astype(BF16).astype(F32), jnp.int32)
    return lax.shift_right_logical(lo, 16) | (hi & jnp.int32(-65536))


def _unpack_bf16_pairs(p):
    lo = lax.bitcast_convert_type(lax.shift_left(p, 16), F32)
    hi = lax.bitcast_convert_type(p & jnp.int32(-65536), F32)
    return jnp.concatenate([lo, hi], axis=1)


def _store_packed(refs, x):
    n = 2 * SC_ROW_WORDS
    for p, ref in enumerate(refs):
        ref[...] = _pack_bf16_pairs(x[:, p * n:(p + 1) * n]).reshape(ref.shape)


def _load_packed(refs):
    return jnp.concatenate([_unpack_bf16_pairs(ref[...].reshape(ref.shape[-2:])) for ref in refs], axis=1)


def _route(logits):
    lane = lax.broadcasted_iota(jnp.int32, logits.shape, 1)
    big = jnp.int32(logits.shape[1])
    t1 = logits.max(axis=-1, keepdims=True)
    i1 = jnp.where(logits == t1, lane, big).min(axis=-1, keepdims=True)
    rest = jnp.where(lane == i1, -jnp.inf, logits)
    t2 = rest.max(axis=-1, keepdims=True)
    i2 = jnp.where(rest == t2, lane, big).min(axis=-1, keepdims=True)
    e2 = jnp.exp(t2 - t1)
    return i1, i2, 1.0 / (1.0 + e2), e2 / (1.0 + e2)


def _outproj_kernel(*refs, dense_ffn):
    ga_ref, na_ref, x_ref, mod_ref, wa_ref, wb_ref, gp_ref, gf_ref = refs[:8]
    m = mod_ref[0]
    tm = x_ref.shape[1]
    th = tm // 2 if dense_ffn else tm
    halves = [slice(lo, lo + th) for lo in range(0, tm, th)]
    ys = [_dot(ga_ref[0, hs, :], wa_ref[...]) + _dot(na_ref[0, hs, :], wb_ref[...]) for hs in halves]
    x1s = [x_ref[0, hs, :] + m[2:3] * (_rms(y) * gp_ref[...]) for hs, y in zip(halves, ys)]
    h2s = [_rms(x1) * gf_ref[...] * (1.0 + m[4:5]) + m[3:4] for x1 in x1s]
    if dense_ffn:
        wg_ref, wu_ref, wd_ref, g2_ref, o_ref = refs[8:]
        ups = [(_dot(h, wg_ref[...]), _dot(h, wu_ref[...])) for h in [h2.astype(BF16) for h2 in h2s]]
        hids = [(_silu(g) * u).astype(BF16) for g, u in ups]
        downs = [_dot(hid, wd_ref[...]) for hid in hids]
        for hs, x1, dn in zip(halves, x1s, downs):
            o_ref[0, hs, :] = x1 + m[5:6] * (_rms(dn) * g2_ref[...])
    else:
        wr_ref, x1_ref, gw_ref, eid_ref, cnt_ref, *h2_refs = refs[8:]
        lanes = wr_ref.shape[1] // 2
        prods = [_dot(jnp.concatenate(_split(h2), axis=0), wr_ref[...]) for h2 in h2s]
        lane = lax.broadcasted_iota(jnp.int32, (th, lanes), 1)
        counts = None
        for hs, x1, h2, prod in zip(halves, x1s, h2s, prods):
            x1_ref[0, hs, :] = x1
            for p, ref in enumerate(h2_refs):
                n = 2 * SC_ROW_WORDS
                ref[0, hs, :] = _pack_bf16_pairs(h2[:, p * n:(p + 1) * n])
            logits = (prod[:th, :lanes] + prod[th:, :lanes]) + (prod[:th, lanes:] + prod[th:, lanes:])
            i1, i2, w1, w2 = _route(jnp.where(lane < N_EXPERTS, logits, -jnp.inf))
            gw_ref[0, hs, :] = jnp.where(lane == 0, w1, jnp.where(lane == 1, w2, 0.0))
            eid_ref[0, hs, :] = jnp.where(lane == 0, i1, jnp.where(lane == 1, i2, 0))
            chosen = jnp.where(lane == i1, 1.0, jnp.where(lane == i2, 1.0, 0.0)).sum(axis=0, keepdims=True)
            counts = chosen if counts is None else counts + chosen
        cnt_ref[0] = jnp.broadcast_to(counts, cnt_ref.shape[1:])


def _outproj(ga, na, x, mod, mod_row0, w_out, g_post, g_ffn, *, ffn=None, w_router=None):
    nb, rows, d = x.shape
    tm = min(TOKEN_TILE, rows)
    dense_ffn = ffn is not None
    tok = lambda n: pl.BlockSpec((1, tm, n), lambda b, i: (b, i, 0))
    const = lambda shape: pl.BlockSpec(shape, lambda b, i: (0,) * len(shape))
    once = lambda shape: pl.BlockSpec(shape, lambda b, i: (0,) * len(shape), pipeline_mode=pl.Buffered(1))
    wa = w_out[:GLA_WIDTH].astype(BF16)
    wb = w_out[GLA_WIDTH:].astype(BF16)
    args = [ga, na, x, mod, wa, wb, g_post.reshape(1, d), g_ffn.reshape(1, d)]
    in_specs = [tok(GLA_WIDTH), tok(NA_WIDTH), tok(d),
                pl.BlockSpec((1, 6, d), lambda b, i: (b + mod_row0, 0, 0)),
                const(wa.shape), const(wb.shape), const((1, d)), const((1, d))]
    est = 2 * (tm * d * (4 + 4 + 2) + 2 * tm * GLA_WIDTH * 2 + d * d * 2) + 4 * tm * d * 4
    if dense_ffn:
        wg, wu, wd, g2 = ffn
        dff = wg.shape[1]
        args += [wg, wu, wd, g2.reshape(1, d)]
        in_specs += [once(wg.shape), once(wu.shape), once(wd.shape), const((1, d))]
        out_specs = tok(d)
        out_shape = jax.ShapeDtypeStruct((nb, rows, d), F32)
        est += 3 * d * dff * 2 + 3 * tm * dff * 4
    else:
        nt = rows // tm
        wr = jnp.zeros((d, V7X_LANES), F32).at[:, :N_EXPERTS].set(w_router)
        wr = jnp.concatenate(_split(wr), axis=1)
        args.append(wr)
        in_specs.append(const(wr.shape))
        n_parts = d // (2 * SC_ROW_WORDS)
        out_specs = [tok(d), tok(V7X_LANES), tok(V7X_LANES),
                     pl.BlockSpec((1, 8, V7X_LANES), lambda b, i: (b * nt + i, 0, 0))] + [tok(SC_ROW_WORDS)] * n_parts
        out_shape = [jax.ShapeDtypeStruct((nb, rows, d), F32),
                     jax.ShapeDtypeStruct((nb, rows, V7X_LANES), F32),
                     jax.ShapeDtypeStruct((nb, rows, V7X_LANES), jnp.int32),
                     jax.ShapeDtypeStruct((nb * nt, 8, V7X_LANES), F32)]
        out_shape += [jax.ShapeDtypeStruct((nb, rows, SC_ROW_WORDS), jnp.int32)] * n_parts
    return pl.pallas_call(
        functools.partial(_outproj_kernel, dense_ffn=dense_ffn),
        grid=(nb, rows // tm),
        in_specs=in_specs,
        out_specs=out_specs,
        out_shape=out_shape,
        compiler_params=_params(("parallel", "parallel"), est),
        name="outproj_ffn" if dense_ffn else "outproj_router",
    )(*args)


def _slot_kernel(eid_ref, base_ref, pos_ref):
    eid = eid_ref[...]
    tm, lanes = eid.shape
    lane = lax.broadcasted_iota(jnp.int32, (tm, lanes), 1)
    i1 = eid[:, 0:1]
    i2 = eid[:, 1:2]
    chosen = jnp.where(lane == i1, 1.0, jnp.where(lane == i2, 1.0, 0.0)).astype(BF16)
    row = lax.broadcasted_iota(jnp.int32, (tm, tm), 0)
    col = lax.broadcasted_iota(jnp.int32, (tm, tm), 1)
    incl = jnp.where(row >= col, 1.0, 0.0).astype(BF16)
    slot = base_ref[0][0:1] + _dot(incl, chosen) - 1.0
    p1 = jnp.where(lane == i1, slot, 0.0).sum(axis=-1, keepdims=True).astype(jnp.int32)
    p2 = jnp.where(lane == i2, slot, 0.0).sum(axis=-1, keepdims=True).astype(jnp.int32)
    pos_ref[...] = jnp.where(lane == 0, p1, jnp.where(lane == 1, p2, 0))


def _slots(eid, tile_base, tm):
    t, lanes = eid.shape
    return pl.pallas_call(
        _slot_kernel,
        grid=(t // tm,),
        in_specs=[pl.BlockSpec((tm, lanes), lambda i: (i, 0)),
                  pl.BlockSpec((1, 8, lanes), lambda i: (i, 0, 0))],
        out_specs=pl.BlockSpec((tm, lanes), lambda i: (i, 0)),
        out_shape=jax.ShapeDtypeStruct((t, lanes), jnp.int32),
        compiler_params=_params(("parallel",), 8 * tm * lanes * 4 + 4 * tm * tm),
        name="moe_slots",
    )(eid, tile_base)


def _route_plan(cnt, tm_tokens):
    counts = cnt[:, 0, :N_EXPERTS].astype(jnp.int32)
    total = counts.sum(axis=0)
    padded = -(-total // MOE_ROW_TILE) * MOE_ROW_TILE
    start = jnp.cumsum(padded) - padded
    before = jnp.cumsum(counts, axis=0) - counts
    tile_base = (start[None, :] + before).astype(F32)
    tile_base = jnp.zeros((cnt.shape[0], 8, V7X_LANES), F32).at[:, :, :N_EXPERTS].set(tile_base[:, None, :])
    n_slots = tm_tokens * 2 + N_EXPERTS * MOE_ROW_TILE
    first_row = jnp.arange(n_slots // MOE_ROW_TILE, dtype=jnp.int32) * MOE_ROW_TILE
    tile_expert = jnp.minimum((first_row[:, None] >= (start + padded)[None, :]).sum(axis=1), N_EXPERTS - 1)
    n_valid = jnp.clip(start[tile_expert] + total[tile_expert] - first_row, 0, MOE_ROW_TILE)
    return tile_base, tile_expert.astype(jnp.int32), n_valid.astype(jnp.int32), n_slots


def _sc_mesh():
    return plsc.VectorSubcoreMesh(core_axis_name="core", subcore_axis_name="subcore")


def _scatter_rows(x, idx, n_out):
    t, w = x.shape
    n = idx.shape[0]
    win = SC_ROW_WINDOW
    n_blk = t // win

    @functools.partial(pl.kernel, out_type=jax.ShapeDtypeStruct((n_out, w), x.dtype), mesh=_sc_mesh(),
                       scratch_types=[], name="moe_dispatch")
    def scatter(x_hbm, i_hbm, o_hbm):
        def body(x_vmem, i_vmem):
            pltpu.sync_copy(x_vmem, o_hbm.at[i_vmem.at[0]])

        pltpu.emit_pipeline(
            body,
            grid=(n // win,),
            in_specs=[pl.BlockSpec((win, w), lambda i: (i % n_blk, 0)),
                      pl.BlockSpec((1, win), lambda i: (0, i))],
            out_specs=[],
            core_axis_name=("core", "subcore"),
            dimension_semantics=(pltpu.PARALLEL,),
        )(x_hbm, i_hbm)

    return scatter(x, idx.reshape(1, n))


def _gather_rows(x, idx):
    n = idx.shape[0]
    w = x.shape[1]
    win = SC_ROW_WINDOW

    @functools.partial(pl.kernel, out_type=jax.ShapeDtypeStruct((n, w), x.dtype), mesh=_sc_mesh(),
                       scratch_types=[], name="moe_combine_gather")
    def gather(x_hbm, i_hbm, o_hbm):
        def body(i_vmem, o_vmem):
            pltpu.sync_copy(x_hbm.at[i_vmem.at[0]], o_vmem)

        pltpu.emit_pipeline(
            body,
            grid=(n // win,),
            in_specs=[pl.BlockSpec((1, win), lambda i: (0, i))],
            out_specs=[pl.BlockSpec((win, w), lambda i: (i, 0))],
            core_axis_name=("core", "subcore"),
            dimension_semantics=(pltpu.PARALLEL,),
        )(i_hbm, o_hbm)

    return gather(x, idx.reshape(1, n))


def _experts_kernel(te_ref, nv_ref, *refs, n_parts):
    x_refs = refs[:n_parts]
    wg_ref, wu_ref, wd_ref = refs[n_parts:n_parts + 3]
    y_refs = refs[n_parts + 3:2 * n_parts + 3]
    i = pl.program_id(0)
    n_valid = nv_ref[i]
    tm = x_refs[0].shape[0]
    dff = wg_ref.shape[2]

    @pl.when(n_valid > 0)
    def _():
        row = lax.broadcasted_iota(jnp.int32, (tm, 1), 0)
        x = jnp.where(row < n_valid, _load_packed(x_refs), 0.0).astype(BF16)
        acc = None
        for lo in range(0, dff, MOE_FF_CHUNK):
            sl = slice(lo, lo + MOE_FF_CHUNK)
            hid = (_silu(_dot(x, wg_ref[0, :, sl])) * _dot(x, wu_ref[0, :, sl])).astype(BF16)
            part = _dot(hid, wd_ref[0, sl, :])
            acc = part if acc is None else acc + part
        _store_packed(y_refs, acc)

    @pl.when(n_valid == 0)
    def _():
        for ref in y_refs:
            ref[...] = jnp.zeros_like(ref)


def _experts(xs_parts, tile_expert, n_valid, wg, wu, wd):
    n_parts = len(xs_parts)
    n_slots, words = xs_parts[0].shape
    n_e, d, dff = wg.shape
    tm = MOE_ROW_TILE
    rows_spec = pl.BlockSpec((tm, words), lambda i, te, nv: (i, 0))
    weights = lambda shape: pl.BlockSpec((1,) + shape, lambda i, te, nv: (te[i], 0, 0))
    est = 2 * 3 * d * dff * 2 + 4 * tm * d * 2 + 3 * tm * MOE_FF_CHUNK * 4 + 3 * tm * d * 4
    grid_spec = pltpu.PrefetchScalarGridSpec(
        num_scalar_prefetch=2,
        grid=(n_slots // tm,),
        in_specs=[rows_spec] * n_parts + [weights((d, dff)), weights((d, dff)), weights((dff, d))],
        out_specs=[rows_spec] * n_parts,
    )
    return pl.pallas_call(
        functools.partial(_experts_kernel, n_parts=n_parts),
        grid_spec=grid_spec,
        out_shape=[jax.ShapeDtypeStruct((n_slots, words), jnp.int32)] * n_parts,
        compiler_params=_params(("arbitrary",), est),
        name="moe_experts",
    )(tile_expert, n_valid, *xs_parts, wg, wu, wd)


def _combine_kernel(*refs, n_parts):
    y1_refs = refs[:n_parts]
    y2_refs = refs[n_parts:2 * n_parts]
    gw_ref, x_ref, mod_ref, g_ref, o_ref = refs[2 * n_parts:]
    gw = gw_ref[0]
    y = gw[:, 0:1] * _load_packed(y1_refs) + gw[:, 1:2] * _load_packed(y2_refs)
    o_ref[0] = x_ref[0] + mod_ref[0][5:6] * (_rms(y) * g_ref[...])


def _combine(ys2_parts, gw, x1, mod, mod_row0, g_post):
    n_parts = len(ys2_parts)
    nb, rows, d = x1.shape
    words = ys2_parts[0].shape[-1]
    tm = min(TOKEN_TILE, rows)
    tok = lambda n: pl.BlockSpec((1, tm, n), lambda b, i: (b, i, 0))
    ysp = lambda k: pl.BlockSpec((1, tm, words), lambda b, i: (k * nb + b, i, 0))
    est = 2 * tm * (d * 2 * 4 + d * 4 + V7X_LANES * 4) + 3 * tm * d * 4
    return pl.pallas_call(
        functools.partial(_combine_kernel, n_parts=n_parts),
        grid=(nb, rows // tm),
        in_specs=[ysp(0)] * n_parts + [ysp(1)] * n_parts + [
            tok(V7X_LANES), tok(d),
            pl.BlockSpec((1, 6, d), lambda b, i: (b + mod_row0, 0, 0)),
            pl.BlockSpec((1, d), lambda b, i: (0, 0))],
        out_specs=tok(d),
        out_shape=jax.ShapeDtypeStruct((nb, rows, d), F32),
        compiler_params=_params(("parallel", "parallel"), est),
        name="moe_combine",
    )(*ys2_parts, *ys2_parts, gw, x1, mod, g_post.reshape(1, d))


def _moe_routed(gw, eid, cnt, *h2_parts, x1, mod, mod_row0, experts, g_post):
    nb, rows, d = x1.shape
    t = nb * rows
    tm = min(TOKEN_TILE, rows)
    tile_base, tile_expert, n_valid, n_slots = _route_plan(cnt, t)
    pos = _slots(eid.reshape(t, V7X_LANES), tile_base, tm)
    idx = jnp.concatenate([pos[:, 0], pos[:, 1]])
    xs = [_scatter_rows(h.reshape(t, h.shape[-1]), idx, n_slots) for h in h2_parts]
    ys = _experts(xs, tile_expert, n_valid, *experts)
    ys2 = [_gather_rows(y, idx).reshape(2 * nb, rows, y.shape[-1]) for y in ys]
    return _combine(ys2, gw, x1, mod, mod_row0, g_post)


def _cast_kernel(w_ref, o_ref):
    o_ref[...] = w_ref[...].astype(o_ref.dtype)


def _to_bf16(w):
    shape = w.shape
    w3 = w.reshape((-1,) + shape[-2:])
    n, r, c = w3.shape
    rb = r
    while rb * c * 4 > CAST_BLOCK_BYTES and rb % 16 == 0:
        rb //= 2
    spec = pl.BlockSpec((1, rb, c), lambda e, i: (e, i, 0))
    out = pl.pallas_call(
        _cast_kernel,
        grid=(n, r // rb),
        in_specs=[spec],
        out_specs=spec,
        out_shape=jax.ShapeDtypeStruct(w3.shape, BF16),
        compiler_params=_params(("parallel", "parallel"), 2 * rb * c * 6),
        name="cast_bf16",
    )(w3)
    return out.reshape(shape)
def _pack_w_in(w):
    a = 2 * GLA_KDIM + 2 * GLA_WIDTH
    lr = 2 * GLA_GATE_RANK
    reps = LR_PAD // lr
    return jnp.concatenate([w[:, :a], w[:, a + lr:]] + [w[:, a:a + lr]] * reps, axis=1).astype(BF16)


def _pack_gate(w_gate, b_gate):
    lr = 2 * GLA_GATE_RANK
    packed = []
    for d in range(2):
        wd = jnp.zeros((lr, GLA_KDIM), F32).at[d * GLA_GATE_RANK:(d + 1) * GLA_GATE_RANK].set(w_gate[d])
        hi, lo = _split(wd)
        packed.append(jnp.concatenate([hi, hi, lo, lo], axis=0))
    return jnp.stack(packed), b_gate.reshape(2, 1, GLA_KDIM)


def kernel(x, c, ctx, c_ctx, w_ada, b_ada, g_pre_mix, g_post_mix, g_pre_ffn, g_post_ffn, w_in,
           gla_w_gate, gla_b_gate, gla_g_norm, na_rpb, w_out, ffn_w_gate, ffn_w_up, ffn_w_down,
           moe_w_router, moe_w_gate, moe_w_up, moe_w_down):
    B, L, D = x.shape
    Lc = ctx.shape[1]
    depth = w_ada.shape[0]
    rows = L // GRID_W
    ctx_row = B
    n_cond = -(-(B + 1) // 8) * 8
    cond = jnp.zeros((n_cond, D), F32).at[:B].set(c).at[B].set(c_ctx)
    mod = _modulation(cond, w_ada, b_ada)
    tables = _rope_tables(L)
    ctx = ctx.reshape(1, B * Lc, D)

    for i in range(depth):
        last = i == depth - 1
        j = i // 2
        w = _pack_w_in(w_in[i])
        wgp, bgp = _pack_gate(gla_w_gate[i], gla_b_gate[i])
        qk, gv, gr, nq, nk, nv, bf, bb = _inproj(x, mod[i], 0, g_pre_mix[i], w, wgp, bgp, tables)
        cparts = _inproj(ctx, mod[i], ctx_row, g_pre_mix[i], w, wgp, bgp)
        cqk, cgv, cgr, cnq, cnk, cnv, cbf, cbb = [t.reshape(B, Lc, t.shape[-1]) for t in cparts]
        ga, gac = _gla(qk, gv, gr, bf, bb, cqk, cgv, cgr, cbf, cbb, gla_g_norm[i])
        na = _na(nq, nk, nv, cnk, cnv, _na_bias_table(na_rpb[i], rows))
        if i % 2 == 0:
            ffn = (_to_bf16(ffn_w_gate[j]), _to_bf16(ffn_w_up[j]), _to_bf16(ffn_w_down[j]), g_post_ffn[i])

            def mix_and_ffn(ga_, na_, x_, row0):
                return _outproj(ga_, na_, x_, mod[i], row0, w_out[i], g_post_mix[i], g_pre_ffn[i], ffn=ffn)
        else:
            experts = (_to_bf16(moe_w_gate[j]), _to_bf16(moe_w_up[j]), _to_bf16(moe_w_down[j]))

            def mix_and_ffn(ga_, na_, x_, row0):
                x1, *routed = _outproj(ga_, na_, x_, mod[i], row0, w_out[i], g_post_mix[i], g_pre_ffn[i],
                                       w_router=moe_w_router[j])
                return _moe_routed(*routed, x1=x1, mod=mod[i], mod_row0=row0, experts=experts, g_post=g_post_ffn[i])
        x = mix_and_ffn(ga, na, x, 0)
        if not last:
            nac = _ctx_attn(cnq, cnk, cnv)
            ctx = mix_and_ffn(gac.reshape(1, B * Lc, GLA_WIDTH), nac.reshape(1, B * Lc, NA_WIDTH), ctx, ctx_row)
    return x
```

```python
import functools

import numpy as np
import jax
import jax.numpy as jnp
from jax import lax
from jax.experimental import pallas as pl
from jax.experimental.pallas import tpu as pltpu
from jax.experimental.pallas import tpu_sc as plsc

F32 = jnp.float32
BF16 = jnp.bfloat16

GRID_W = 64
GLA_HEADS = 4
GLA_DV = 128
GLA_DK = 64
GLA_KDIM = GLA_HEADS * GLA_DK
GLA_WIDTH = GLA_HEADS * GLA_DV
GLA_GATE_RANK = 16
GLA_GATE_NORM = 16.0
NA_HEADS = 8
NA_DH = 64
NA_WIDTH = NA_HEADS * NA_DH
NA_WIN_H = 8
NA_WIN_W = 16
ROPE_BASE = 10000.0
N_EXPERTS = 8
EPS = 1e-6

V7X_LANES = 128
V7X_VMEM_BYTES = 64 * 1024 * 1024
V7X_VMEM_USABLE = V7X_VMEM_BYTES - 8 * 1024 * 1024

TOKEN_TILE = 512
GLA_CHUNK = 128
INPROJ_ROW_BLOCKS = 2
GLA_UNROLL = 4
NA_ROWS_PER_STEP = 8
NA_ROWS_PER_ITER = 2
MASK_VALUE = -1e30
MOE_FF_CHUNK = 1792
MOE_ROW_TILE = 512
SC_ROW_WINDOW = 128
CAST_BLOCK_BYTES = 4 << 20
SC_ROW_WORDS = 256
LR_PAD = V7X_LANES


def _vmem_limit(estimate_bytes):
    return int(min(V7X_VMEM_USABLE, estimate_bytes * 5 // 4 + (4 << 20)))


def _params(semantics, vmem_estimate):
    return pltpu.CompilerParams(dimension_semantics=semantics, vmem_limit_bytes=_vmem_limit(vmem_estimate))


def _dot(a, b):
    return jnp.dot(a, b, preferred_element_type=F32)


def _dot_nt(a, b):
    return lax.dot_general(a, b, (((1,), (1,)), ((), ())), preferred_element_type=F32)


def _dot_tn(a, b):
    return lax.dot_general(a, b, (((0,), (0,)), ((), ())), preferred_element_type=F32)


def _split(x):
    hi = x.astype(BF16)
    lo = (x - hi.astype(F32)).astype(BF16)
    return hi, lo


def _dot3(a, b):
    ah, al = _split(a)
    bh, bl = _split(b)
    return _dot(ah, bh) + (_dot(al, bh) + _dot(ah, bl))


def _sigmoid(x):
    return 1.0 / (1.0 + jnp.exp(-x))


def _silu(x):
    return x * _sigmoid(x)


def _rms(x):
    return x * lax.rsqrt(jnp.mean(x * x, axis=-1, keepdims=True) + EPS)


def _mod_kernel(c_ref, w_ref, b_ref, o_ref):
    o_ref[0] = _dot3(_silu(c_ref[...]), w_ref[0]) + b_ref[0]


def _modulation(cond, w_ada, b_ada):
    depth, d, n = w_ada.shape
    rows = cond.shape[0]
    tn = 512
    out = pl.pallas_call(
        _mod_kernel,
        grid=(depth, n // tn),
        in_specs=[
            pl.BlockSpec((rows, d), lambda l, j: (0, 0)),
            pl.BlockSpec((1, d, tn), lambda l, j: (l, 0, j)),
            pl.BlockSpec((1, 1, tn), lambda l, j: (l, 0, j)),
        ],
        out_specs=pl.BlockSpec((1, rows, tn), lambda l, j: (l, 0, j)),
        out_shape=jax.ShapeDtypeStruct((depth, rows, n), F32),
        compiler_params=_params(("parallel", "parallel"), 3 * d * tn * 4 * 2),
        name="modulation",
    )(cond, w_ada, b_ada.reshape(depth, 1, n))
    return out.reshape(depth, rows, 6, d)


def _log_decay(logit):
    return (jnp.minimum(logit, 0.0) - jnp.log(1.0 + jnp.exp(-jnp.abs(logit)))) * (1.0 / GLA_GATE_NORM)


def _inproj_kernel(*refs, rope):
    if rope:
        x_ref, mod_ref, g_ref, w_ref, wg_ref, bg_ref, cos_ref, sin_ref = refs[:8]
    else:
        x_ref, mod_ref, g_ref, w_ref, wg_ref, bg_ref = refs[:6]
    qk_ref, gv_ref, gr_ref, nq_ref, nk_ref, nv_ref, bf_ref, bb_ref = refs[-8:]
    m = mod_ref[0]
    tm = x_ref.shape[1]
    plain_refs = (gv_ref, gr_ref, nq_ref, nk_ref, nv_ref)
    offs = np.cumsum([0, qk_ref.shape[-1]] + [r.shape[-1] for r in plain_refs])
    C = GLA_CHUNK
    row = lax.broadcasted_iota(jnp.int32, (C, 2 * C), 0)
    colm = lax.broadcasted_iota(jnp.int32, (C, 2 * C), 1) % C

    def rows_block(lo, n_rows):
        rs = slice(lo, lo + n_rows)
        h = (_rms(x_ref[0, rs, :]) * g_ref[...] * (1.0 + m[1:2]) + m[0:1]).astype(BF16)

        def project(k):
            ref = plain_refs[k]
            ref[0, rs, :] = _dot(h, w_ref[:, offs[k + 1]:offs[k + 2]]).astype(ref.dtype)

        project(0)
        lr = _dot(h, w_ref[:, offs[-1]:offs[-1] + LR_PAD])
        lr_hi, lr_lo = _split(lr)
        lr_lane = lax.broadcasted_iota(jnp.int32, (1, LR_PAD), 1)
        lr_mix = jnp.where((lr_lane // (2 * GLA_GATE_RANK)) % 2 == 0, lr_hi, lr_lo)
        logits = [_dot(lr_mix, wg_ref[d]) + bg_ref[d] for d in range(2)]
        n = qk_ref.shape[-1]
        qk = _dot(h, w_ref[:, 0:n])
        lane = lax.broadcasted_iota(jnp.int32, (1, n), 1)
        if rope:
            reps = n // cos_ref.shape[-1]
            cos = jnp.concatenate([cos_ref[rs, :]] * reps, axis=1)
            sin = jnp.concatenate([sin_ref[rs, :]] * reps, axis=1)
            quarter = GLA_DK // 4
            first = (lane % (2 * quarter)) < quarter
            partner = jnp.where(first, pltpu.roll(qk, n - quarter, 1), pltpu.roll(qk, quarter, 1))
            qk = qk * cos + partner * sin
        qk_ref[0, rs, :] = jnp.where(lane < GLA_KDIM, qk * (GLA_DK ** -0.5), qk).astype(qk_ref.dtype)
        for d, out_ref in enumerate((bf_ref, bb_ref)):
            gh, gl = _split(_log_decay(logits[d]))
            project(1 + 2 * d)
            tri = (row >= colm) if d == 0 else (row <= colm)
            tmat = jnp.where(tri, 1.0, 0.0).astype(BF16)
            for c in range(n_rows // C):
                sl = slice(c * C, (c + 1) * C)
                out_ref[0, lo + c * C:lo + (c + 1) * C, :] = _dot(tmat, jnp.concatenate([gh[sl], gl[sl]], axis=0))
            project(2 + 2 * d)

    n_blocks = INPROJ_ROW_BLOCKS if tm % (INPROJ_ROW_BLOCKS * C) == 0 else 1
    for blk in range(n_blocks):
        rows_block(blk * (tm // n_blocks), tm // n_blocks)


def _inproj(x, mod, mod_row0, g, w, wgp, bgp, rope_tables=None):
    nb, rows, d = x.shape
    tm = min(TOKEN_TILE, rows)
    rope = rope_tables is not None
    widths = (2 * GLA_KDIM, GLA_WIDTH, GLA_WIDTH, NA_WIDTH, NA_WIDTH, NA_WIDTH, GLA_KDIM, GLA_KDIM)
    dtypes = (BF16,) * 6 + (F32, F32)
    tok = lambda n: pl.BlockSpec((1, tm, n), lambda b, i: (b, i, 0))
    const = lambda shape: pl.BlockSpec(shape, lambda b, i: (0,) * len(shape))
    args = [x, mod, g.reshape(1, d), w, wgp, bgp]
    in_specs = [tok(d), pl.BlockSpec((1, 6, d), lambda b, i: (b + mod_row0, 0, 0)), const((1, d)),
                const(w.shape), const(wgp.shape), const(bgp.shape)]
    if rope:
        args += list(rope_tables)
        in_specs += [pl.BlockSpec((tm, t.shape[1]), lambda b, i: (i, 0)) for t in rope_tables]
    est = 2 * (tm * d * 4 + d * w.shape[1] * 2 + sum(widths) * tm * 4) + tm * d * 8
    return pl.pallas_call(
        functools.partial(_inproj_kernel, rope=rope),
        grid=(nb, rows // tm),
        in_specs=in_specs,
        out_specs=[tok(n) for n in widths],
        out_shape=[jax.ShapeDtypeStruct((nb, rows, n), dt) for n, dt in zip(widths, dtypes)],
        compiler_params=_params(("parallel", "parallel"), est),
        name="inproj",
    )(*args)


def _gla_kernel(q_ref, k_ref, v_ref, r_ref, bf_ref, bb_ref, cq_ref, ck_ref, cv_ref, cr_ref, cbf_ref, cbb_ref,
                gain_ref, o_ref, oc_ref, of_ref, ocf_ref, st_ref):
    C = GLA_CHUNK
    L = q_ref.shape[1]
    Lc = cq_ref.shape[1]
    row = lax.broadcasted_iota(jnp.int32, (C, C), 0)
    col = lax.broadcasted_iota(jnp.int32, (C, C), 1)
    lane = lax.broadcasted_iota(jnp.int32, (1, 2 * GLA_DK), 1)
    head0 = lane < GLA_DK
    st_row = lax.broadcasted_iota(jnp.int32, (2 * GLA_DV, 2 * GLA_DK), 0) // GLA_DV
    st_col = lax.broadcasted_iota(jnp.int32, (2 * GLA_DV, 2 * GLA_DK), 1) // GLA_DK
    blockdiag = st_row == st_col
    gain = gain_ref[...]

    def chunk_steps(refs, chunks, acc_ref):
        rq, rk, rv = refs[:3]
        dirs = (0, 1)
        sls = [pl.ds(pl.multiple_of(i * C, C), C) for i in chunks]
        tris = [row >= col, row <= col]
        vs = [rv[0, sl, :] for sl in sls]
        qfs = [rq[0, sl, :].astype(F32) for sl in sls]
        kfs = [rk[0, sl, :].astype(F32) for sl in sls]
        bs = [refs[3 + d][0, sls[d], :] for d in dirs]
        b_mids = [b[C // 2:C // 2 + 1] for b in bs]
        b_edges = [bs[0][C - 1:C], bs[1][0:1]]
        qes = [(qfs[d] * jnp.exp(bs[d] - b_mids[d])).astype(BF16) for d in dirs]
        kes = [(kfs[d] * jnp.exp(b_mids[d] - bs[d])).astype(BF16) for d in dirs]
        zero = jnp.zeros_like(qes[0])
        lhss = [jnp.concatenate([jnp.where(head0, qe, zero), jnp.where(head0, zero, qe)], axis=0) for qe in qes]
        scores = [_dot_nt(lhss[d], kes[d]) for d in dirs]
        qbs = [(qfs[d] * jnp.exp(bs[d])).astype(BF16) for d in dirs]
        kds = [(kfs[d] * jnp.exp(b_edges[d] - bs[d])).astype(BF16) for d in dirs]
        sts = [st_ref[d] for d in dirs]
        inter = [_dot_nt(qbs[d], sts[d].astype(BF16)) for d in dirs]
        dss = [_dot_tn(vs[d], kds[d]) for d in dirs]
        ams = [jnp.where(jnp.concatenate([tris[d], tris[d]], axis=0), scores[d], 0.0).astype(BF16) for d in dirs]
        intra = [jnp.concatenate([_dot(ams[d][:C], vs[d][:, :GLA_DV]), _dot(ams[d][C:], vs[d][:, GLA_DV:])], axis=1)
                 for d in dirs]
        for d in dirs:
            st_ref[d] = sts[d] * jnp.exp(b_edges[d]) + jnp.where(blockdiag, dss[d], 0.0)
            acc_ref[d, sls[d], :] = intra[d] + inter[d]

    def finish(total, r):
        outs = []
        for h in range(2):
            oh = _rms(total[:, h * GLA_DV:(h + 1) * GLA_DV]) * gain
            outs.append(oh * _silu(r[:, h * GLA_DV:(h + 1) * GLA_DV].astype(F32)))
        return jnp.concatenate(outs, axis=1)

    def scan(refs, n, acc_ref, r_ref_, out_ref):
        def body(j, carry):
            chunk_steps(refs, (j, n - 1 - j), acc_ref)
            return carry

        lax.fori_loop(0, n, body, 0, unroll=min(GLA_UNROLL, n))

        def fin(i, carry):
            sl = pl.ds(pl.multiple_of(i * C, C), C)
            out_ref[0, sl, :] = finish(acc_ref[0, sl, :] + acc_ref[1, sl, :], r_ref_[0, sl, :]).astype(out_ref.dtype)
            return carry

        lax.fori_loop(0, n, fin, 0, unroll=min(GLA_UNROLL, n))

    st_ref[...] = jnp.zeros_like(st_ref)
    scan((cq_ref, ck_ref, cv_ref, cbf_ref, cbb_ref), Lc // C, ocf_ref, cr_ref, oc_ref)
    scan((q_ref, k_ref, v_ref, bf_ref, bb_ref), L // C, of_ref, r_ref, o_ref)


def _gla(qk, gv, gr, bf, bb, cqk, cgv, cgr, cbf, cbb, gain):
    B, L, _ = qk.shape
    Lc = cqk.shape[1]
    pair = 2 * GLA_DK
    pv = 2 * GLA_DV
    n_pair = GLA_HEADS // 2
    seq = lambda rows, n, off: pl.BlockSpec((1, rows, n), lambda b, p: (b, 0, p + off))
    const = lambda shape: pl.BlockSpec(shape, lambda b, p: (0,) * len(shape))
    est = (2 * (2 * L * pair * 2 + 2 * L * pv * 2 + 2 * L * pair * 4 + L * pv * 2)
           + 2 * L * pv * 4 + (8 << 20))
    return pl.pallas_call(
        _gla_kernel,
        grid=(B, n_pair),
        in_specs=[
            seq(L, pair, 0), seq(L, pair, n_pair), seq(L, pv, 0), seq(L, pv, 0), seq(L, pair, 0), seq(L, pair, 0),
            seq(Lc, pair, 0), seq(Lc, pair, n_pair), seq(Lc, pv, 0), seq(Lc, pv, 0), seq(Lc, pair, 0),
            seq(Lc, pair, 0),
            const((1, GLA_DV)),
        ],
        out_specs=[seq(L, pv, 0), seq(Lc, pv, 0)],
        out_shape=[jax.ShapeDtypeStruct((B, L, GLA_WIDTH), BF16), jax.ShapeDtypeStruct((B, Lc, GLA_WIDTH), BF16)],
        scratch_shapes=[pltpu.VMEM((2, L, pv), F32), pltpu.VMEM((2, Lc, pv), F32), pltpu.VMEM((2, pv, pair), F32)],
        compiler_params=_params(("parallel", "parallel"), est),
        name="gla",
    )(qk, qk, gv, gr, bf, bb, cqk, cqk, cgv, cgr, cbf, cbb, gain.reshape(1, GLA_DV))


def _rope_tables(L):
    pos = np.arange(L)
    half = GLA_DK // 4
    inv = ROPE_BASE ** (-np.arange(half, dtype=np.float64) / half)
    lane = np.arange(2 * GLA_DK)
    jj = lane % (GLA_DK // 2)
    use_col = (lane % GLA_DK) >= GLA_DK // 2
    p = np.where(use_col[None, :], (pos % GRID_W)[:, None], (pos // GRID_W)[:, None]).astype(np.float64)
    ang = p * inv[jj % half][None, :]
    first = jj < half
    cos = np.cos(ang)
    sin = np.where(first[None, :], -np.sin(ang), np.sin(ang))
    return jnp.asarray(cos, F32), jnp.asarray(sin, F32)


def _softmax_pv(s_parts, v_parts):
    m = s_parts[0].max(axis=-1, keepdims=True)
    for s in s_parts[1:]:
        m = jnp.maximum(m, s.max(axis=-1, keepdims=True))
    den = None
    acc = None
    for s, v in zip(s_parts, v_parts):
        p = jnp.exp(s - m)
        den = p.sum(axis=-1, keepdims=True) if den is None else den + p.sum(axis=-1, keepdims=True)
        pv = _dot(p.astype(BF16), v)
        acc = pv if acc is None else acc + pv
    return acc / den


def _na_window_start():
    cols = np.arange(GRID_W)
    return np.clip(cols - NA_WIN_W // 2, 0, GRID_W - NA_WIN_W)


def _na_kernel(q_ref, k_ref, v_ref, ck_ref, cv_ref, bias_ref, o_ref):
    W = GRID_W
    rows = k_ref.shape[1] // W
    n_loc = bias_ref.shape[2]
    kh = n_loc // W
    r0 = pl.program_id(1) * NA_ROWS_PER_STEP
    pair = 2 * NA_DH
    lane = lax.broadcasted_iota(jnp.int32, (1, pair), 1)
    head0 = lane < NA_DH
    scale = jnp.asarray(NA_DH ** -0.5, q_ref.dtype)

    n_pairs = NA_HEADS // 2
    lanes = [slice(p * pair, (p + 1) * pair) for p in range(n_pairs)]

    def rows_body(it, carry):
        units = []
        for j in range(NA_ROWS_PER_ITER):
            rr = it * NA_ROWS_PER_ITER + j
            r = r0 + rr
            rs = jnp.clip(r - kh // 2, 0, rows - kh)
            qs = pl.ds(pl.multiple_of(rr * W, W), W)
            ks = pl.ds(pl.multiple_of(rs * W, W), n_loc)
            units += [(qs, ks, rs - r + NA_WIN_H - 1, p) for p in range(n_pairs)]
        sts = []
        for qs, ks, dr, p in units:
            q = q_ref[0, qs, lanes[p]] * scale
            zero = jnp.zeros_like(q)
            q2 = jnp.concatenate([jnp.where(head0, q, zero), jnp.where(head0, zero, q)], axis=0)
            keys = jnp.concatenate([k_ref[0, ks, lanes[p]], ck_ref[0, :, lanes[p]]], axis=0)
            sts.append(_dot_nt(keys, q2))
        pts = []
        for (qs, ks, dr, p), st in zip(units, sts):
            st = jnp.concatenate([st[:n_loc] + bias_ref[p, dr], st[n_loc:]], axis=0)
            pts.append(jnp.exp((st - st.max(axis=0, keepdims=True)).astype(BF16)))
        outs = []
        ones = jnp.ones((n_loc + ck_ref.shape[1], pair), BF16)
        for (qs, ks, dr, p), pt in zip(units, pts):
            vals = jnp.concatenate([v_ref[0, ks, lanes[p]], cv_ref[0, :, lanes[p]]], axis=0)
            pv = _dot_tn(pt, jnp.concatenate([vals, ones], axis=1))
            o2 = pv[:, :pair] / pv[:, pair:]
            outs.append(jnp.where(head0, o2[:W], o2[W:]))
        for j in range(NA_ROWS_PER_ITER):
            qs = units[j * n_pairs][0]
            o_ref[0, qs, :] = jnp.concatenate(outs[j * n_pairs:(j + 1) * n_pairs], axis=1).astype(o_ref.dtype)
        return carry

    lax.fori_loop(0, NA_ROWS_PER_STEP // NA_ROWS_PER_ITER, rows_body, 0, unroll=4)


def _na(nq, nk, nv, cnk, cnv, bias):
    B, L, n = nq.shape
    Lc = cnk.shape[1]
    tq = NA_ROWS_PER_STEP * GRID_W
    full = lambda rows: pl.BlockSpec((1, rows, n), lambda b, i: (b, 0, 0))
    est = 2 * (2 * L * n * 2 + 2 * Lc * n * 2 + bias.size * 4 + 2 * tq * n * 2) + (8 << 20)
    return pl.pallas_call(
        _na_kernel,
        grid=(B, L // tq),
        in_specs=[
            pl.BlockSpec((1, tq, n), lambda b, i: (b, i, 0)),
            full(L), full(L), full(Lc), full(Lc),
            pl.BlockSpec(bias.shape, lambda b, i: (0, 0, 0, 0)),
        ],
        out_specs=pl.BlockSpec((1, tq, n), lambda b, i: (b, i, 0)),
        out_shape=jax.ShapeDtypeStruct((B, L, n), BF16),
        compiler_params=_params(("parallel", "parallel"), est),
        name="neighbourhood_attention",
    )(nq, nk, nv, cnk, cnv, bias)


def _na_bias_table(rpb, rows):
    kh = min(NA_WIN_H, rows)
    start = _na_window_start()
    kc = np.arange(GRID_W)
    inside = (kc[None, :] >= start[:, None]) & (kc[None, :] < start[:, None] + NA_WIN_W)
    sel = np.zeros((2 * NA_WIN_W - 1, GRID_W, GRID_W), np.float32)
    qq, kk = np.nonzero(inside)
    sel[kk - qq + NA_WIN_W - 1, qq, kk] = 1.0
    by_row = jnp.stack([rpb[:, d:d + kh, :] for d in range(NA_WIN_H)], axis=1)
    t = jnp.einsum('hdic,cqk->hdikq', by_row, jnp.asarray(sel), precision=lax.Precision.HIGHEST)
    t = t + jnp.asarray(np.where(inside, 0.0, MASK_VALUE).T, F32)[None, None, None, :, :]
    t = t.reshape(NA_HEADS // 2, 2, NA_WIN_H, kh * GRID_W, GRID_W)
    return jnp.transpose(t, (0, 2, 3, 1, 4)).reshape(NA_HEADS // 2, NA_WIN_H, kh * GRID_W, 2 * GRID_W)


def _ctx_attn_kernel(q_ref, k_ref, v_ref, o_ref):
    pair = 2 * NA_DH
    lane = lax.broadcasted_iota(jnp.int32, (1, pair), 1)
    head0 = lane < NA_DH
    Lc = q_ref.shape[1]
    outs = []
    for p in range(NA_HEADS // 2):
        ls = slice(p * pair, (p + 1) * pair)
        q = q_ref[0, :, ls]
        zero = jnp.zeros_like(q)
        q2 = jnp.concatenate([jnp.where(head0, q, zero), jnp.where(head0, zero, q)], axis=0)
        s = _dot_nt(q2, k_ref[0, :, ls]) * (NA_DH ** -0.5)
        o2 = _softmax_pv([s], [v_ref[0, :, ls]])
        outs.append(jnp.where(head0, o2[:Lc], o2[Lc:]))
    o_ref[0] = jnp.concatenate(outs, axis=1).astype(o_ref.dtype)


def _ctx_attn(cnq, cnk, cnv):
    B, Lc, n = cnq.shape
    spec = pl.BlockSpec((1, Lc, n), lambda b: (b, 0, 0))
    return pl.pallas_call(
        _ctx_attn_kernel,
        grid=(B,),
        in_specs=[spec, spec, spec],
        out_specs=spec,
        out_shape=jax.ShapeDtypeStruct((B, Lc, n), BF16),
        compiler_params=_params(("parallel",), 8 * Lc * n * 2 + (8 << 20)),
        name="context_attention",
    )(cnq, cnk, cnv)


def _pack_bf16_pairs(x):
    n = x.shape[1] // 2
    lo = lax.bitcast_convert_type(x[:, :n].astype(BF16).astype(F32), jnp.int32)
    hi = lax.bitcast_convert_type(x[:, n:].astype(BF16).astype(F32), jnp.int32)
    return lax.shift_right_logical(lo, 16) | (hi & jnp.int32(-65536))


def _unpack_bf16_pairs(p):
    lo = lax.bitcast_convert_type(lax.shift_left(p, 16), F32)
    hi = lax.bitcast_convert_type(p & jnp.int32(-65536), F32)
    return jnp.concatenate([lo, hi], axis=1)


def _store_packed(refs, x):
    n = 2 * SC_ROW_WORDS
    for p, ref in enumerate(refs):
        ref[...] = _pack_bf16_pairs(x[:, p * n:(p + 1) * n]).reshape(ref.shape)


def _load_packed(refs):
    return jnp.concatenate([_unpack_bf16_pairs(ref[...].reshape(ref.shape[-2:])) for ref in refs], axis=1)


def _route(logits):
    lane = lax.broadcasted_iota(jnp.int32, logits.shape, 1)
    big = jnp.int32(logits.shape[1])
    t1 = logits.max(axis=-1, keepdims=True)
    i1 = jnp.where(logits == t1, lane, big).min(axis=-1, keepdims=True)
    rest = jnp.where(lane == i1, -jnp.inf, logits)
    t2 = rest.max(axis=-1, keepdims=True)
    i2 = jnp.where(rest == t2, lane, big).min(axis=-1, keepdims=True)
    e2 = jnp.exp(t2 - t1)
    return i1, i2, 1.0 / (1.0 + e2), e2 / (1.0 + e2)


def _outproj_kernel(*refs, dense_ffn):
    ga_ref, na_ref, x_ref, mod_ref, wa_ref, wb_ref, gp_ref, gf_ref = refs[:8]
    m = mod_ref[0]
    tm = x_ref.shape[1]
    th = tm // 2 if dense_ffn else tm
    halves = [slice(lo, lo + th) for lo in range(0, tm, th)]
    ys = [_dot(ga_ref[0, hs, :], wa_ref[...]) + _dot(na_ref[0, hs, :], wb_ref[...]) for hs in halves]
    x1s = [x_ref[0, hs, :] + m[2:3] * (_rms(y) * gp_ref[...]) for hs, y in zip(halves, ys)]
    h2s = [_rms(x1) * gf_ref[...] * (1.0 + m[4:5]) + m[3:4] for x1 in x1s]
    if dense_ffn:
        wg_ref, wu_ref, wd_ref, g2_ref, o_ref = refs[8:]
        ups = [(_dot(h, wg_ref[...]), _dot(h, wu_ref[...])) for h in [h2.astype(BF16) for h2 in h2s]]
        hids = [(_silu(g) * u).astype(BF16) for g, u in ups]
        downs = [_dot(hid, wd_ref[...]) for hid in hids]
        for hs, x1, dn in zip(halves, x1s, downs):
            o_ref[0, hs, :] = x1 + m[5:6] * (_rms(dn) * g2_ref[...])
    else:
        wr_ref, x1_ref, gw_ref, eid_ref, cnt_ref, *h2_refs = refs[8:]
        lanes = wr_ref.shape[1] // 2
        prods = [_dot(jnp.concatenate(_split(h2), axis=0), wr_ref[...]) for h2 in h2s]
        lane = lax.broadcasted_iota(jnp.int32, (th, lanes), 1)
        counts = None
        for hs, x1, h2, prod in zip(halves, x1s, h2s, prods):
            x1_ref[0, hs, :] = x1
            for p, ref in enumerate(h2_refs):
                n = 2 * SC_ROW_WORDS
                ref[0, hs, :] = _pack_bf16_pairs(h2[:, p * n:(p + 1) * n])
            logits = (prod[:th, :lanes] + prod[th:, :lanes]) + (prod[:th, lanes:] + prod[th:, lanes:])
            i1, i2, w1, w2 = _route(jnp.where(lane < N_EXPERTS, logits, -jnp.inf))
            gw_ref[0, hs, :] = jnp.where(lane == 0, w1, jnp.where(lane == 1, w2, 0.0))
            eid_ref[0, hs, :] = jnp.where(lane == 0, i1, jnp.where(lane == 1, i2, 0))
            chosen = jnp.where(lane == i1, 1.0, jnp.where(lane == i2, 1.0, 0.0)).sum(axis=0, keepdims=True)
            counts = chosen if counts is None else counts + chosen
        cnt_ref[0] = jnp.broadcast_to(counts, cnt_ref.shape[1:])


def _outproj(ga, na, x, mod, mod_row0, w_out, g_post, g_ffn, *, ffn=None, w_router=None):
    nb, rows, d = x.shape
    tm = min(TOKEN_TILE, rows)
    dense_ffn = ffn is not None
    tok = lambda n: pl.BlockSpec((1, tm, n), lambda b, i: (b, i, 0))
    const = lambda shape: pl.BlockSpec(shape, lambda b, i: (0,) * len(shape))
    once = lambda shape: pl.BlockSpec(shape, lambda b, i: (0,) * len(shape), pipeline_mode=pl.Buffered(1))
    wa = w_out[:GLA_WIDTH].astype(BF16)
    wb = w_out[GLA_WIDTH:].astype(BF16)
    args = [ga, na, x, mod, wa, wb, g_post.reshape(1, d), g_ffn.reshape(1, d)]
    in_specs = [tok(GLA_WIDTH), tok(NA_WIDTH), tok(d),
                pl.BlockSpec((1, 6, d), lambda b, i: (b + mod_row0, 0, 0)),
                const(wa.shape), const(wb.shape), const((1, d)), const((1, d))]
    est = 2 * (tm * d * (4 + 4 + 2) + 2 * tm * GLA_WIDTH * 2 + d * d * 2) + 4 * tm * d * 4
    if dense_ffn:
        wg, wu, wd, g2 = ffn
        dff = wg.shape[1]
        args += [wg, wu, wd, g2.reshape(1, d)]
        in_specs += [once(wg.shape), once(wu.shape), once(wd.shape), const((1, d))]
        out_specs = tok(d)
        out_shape = jax.ShapeDtypeStruct((nb, rows, d), F32)
        est += 3 * d * dff * 2 + 3 * tm * dff * 4
    else:
        nt = rows // tm
        wr = jnp.zeros((d, V7X_LANES), F32).at[:, :N_EXPERTS].set(w_router)
        wr = jnp.concatenate(_split(wr), axis=1)
        args.append(wr)
        in_specs.append(const(wr.shape))
        n_parts = d // (2 * SC_ROW_WORDS)
        out_specs = [tok(d), tok(V7X_LANES), tok(V7X_LANES),
                     pl.BlockSpec((1, 8, V7X_LANES), lambda b, i: (b * nt + i, 0, 0))] + [tok(SC_ROW_WORDS)] * n_parts
        out_shape = [jax.ShapeDtypeStruct((nb, rows, d), F32),
                     jax.ShapeDtypeStruct((nb, rows, V7X_LANES), F32),
                     jax.ShapeDtypeStruct((nb, rows, V7X_LANES), jnp.int32),
                     jax.ShapeDtypeStruct((nb * nt, 8, V7X_LANES), F32)]
        out_shape += [jax.ShapeDtypeStruct((nb, rows, SC_ROW_WORDS), jnp.int32)] * n_parts
    return pl.pallas_call(
        functools.partial(_outproj_kernel, dense_ffn=dense_ffn),
        grid=(nb, rows // tm),
        in_specs=in_specs,
        out_specs=out_specs,
        out_shape=out_shape,
        compiler_params=_params(("parallel", "parallel"), est),
        name="outproj_ffn" if dense_ffn else "outproj_router",
    )(*args)


def _slot_kernel(eid_ref, base_ref, pos_ref):
    eid = eid_ref[...]
    tm, lanes = eid.shape
    lane = lax.broadcasted_iota(jnp.int32, (tm, lanes), 1)
    i1 = eid[:, 0:1]
    i2 = eid[:, 1:2]
    chosen = jnp.where(lane == i1, 1.0, jnp.where(lane == i2, 1.0, 0.0)).astype(BF16)
    row = lax.broadcasted_iota(jnp.int32, (tm, tm), 0)
    col = lax.broadcasted_iota(jnp.int32, (tm, tm), 1)
    incl = jnp.where(row >= col, 1.0, 0.0).astype(BF16)
    slot = base_ref[0][0:1] + _dot(incl, chosen) - 1.0
    p1 = jnp.where(lane == i1, slot, 0.0).sum(axis=-1, keepdims=True).astype(jnp.int32)
    p2 = jnp.where(lane == i2, slot, 0.0).sum(axis=-1, keepdims=True).astype(jnp.int32)
    pos_ref[...] = jnp.where(lane == 0, p1, jnp.where(lane == 1, p2, 0))


def _slots(eid, tile_base, tm):
    t, lanes = eid.shape
    return pl.pallas_call(
        _slot_kernel,
        grid=(t // tm,),
        in_specs=[pl.BlockSpec((tm, lanes), lambda i: (i, 0)),
                  pl.BlockSpec((1, 8, lanes), lambda i: (i, 0, 0))],
        out_specs=pl.BlockSpec((tm, lanes), lambda i: (i, 0)),
        out_shape=jax.ShapeDtypeStruct((t, lanes), jnp.int32),
        compiler_params=_params(("parallel",), 8 * tm * lanes * 4 + 4 * tm * tm),
        name="moe_slots",
    )(eid, tile_base)


def _route_plan(cnt, tm_tokens):
    counts = cnt[:, 0, :N_EXPERTS].astype(jnp.int32)
    total = counts.sum(axis=0)
    padded = -(-total // MOE_ROW_TILE) * MOE_ROW_TILE
    start = jnp.cumsum(padded) - padded
    before = jnp.cumsum(counts, axis=0) - counts
    tile_base = (start[None, :] + before).astype(F32)
    tile_base = jnp.zeros((cnt.shape[0], 8, V7X_LANES), F32).at[:, :, :N_EXPERTS].set(tile_base[:, None, :])
    n_slots = tm_tokens * 2 + N_EXPERTS * MOE_ROW_TILE
    first_row = jnp.arange(n_slots // MOE_ROW_TILE, dtype=jnp.int32) * MOE_ROW_TILE
    tile_expert = jnp.minimum((first_row[:, None] >= (start + padded)[None, :]).sum(axis=1), N_EXPERTS - 1)
    n_valid = jnp.clip(start[tile_expert] + total[tile_expert] - first_row, 0, MOE_ROW_TILE)
    return tile_base, tile_expert.astype(jnp.int32), n_valid.astype(jnp.int32), n_slots


def _sc_mesh():
    return plsc.VectorSubcoreMesh(core_axis_name="core", subcore_axis_name="subcore")


def _scatter_rows(x, idx, n_out):
    t, w = x.shape
    n = idx.shape[0]
    win = SC_ROW_WINDOW
    n_blk = t // win

    @functools.partial(pl.kernel, out_type=jax.ShapeDtypeStruct((n_out, w), x.dtype), mesh=_sc_mesh(),
                       scratch_types=[], name="moe_dispatch")
    def scatter(x_hbm, i_hbm, o_hbm):
        def body(x_vmem, i_vmem):
            pltpu.sync_copy(x_vmem, o_hbm.at[i_vmem.at[0]])

        pltpu.emit_pipeline(
            body,
            grid=(n // win,),
            in_specs=[pl.BlockSpec((win, w), lambda i: (i % n_blk, 0)),
                      pl.BlockSpec((1, win), lambda i: (0, i))],
            out_specs=[],
            core_axis_name=("core", "subcore"),
            dimension_semantics=(pltpu.PARALLEL,),
        )(x_hbm, i_hbm)

    return scatter(x, idx.reshape(1, n))


def _gather_rows(x, idx):
    n = idx.shape[0]
    w = x.shape[1]
    win = SC_ROW_WINDOW

    @functools.partial(pl.kernel, out_type=jax.ShapeDtypeStruct((n, w), x.dtype), mesh=_sc_mesh(),
                       scratch_types=[], name="moe_combine_gather")
    def gather(x_hbm, i_hbm, o_hbm):
        def body(i_vmem, o_vmem):
            pltpu.sync_copy(x_hbm.at[i_vmem.at[0]], o_vmem)

        pltpu.emit_pipeline(
            body,
            grid=(n // win,),
            in_specs=[pl.BlockSpec((1, win), lambda i: (0, i))],
            out_specs=[pl.BlockSpec((win, w), lambda i: (i, 0))],
            core_axis_name=("core", "subcore"),
            dimension_semantics=(pltpu.PARALLEL,),
        )(i_hbm, o_hbm)

    return gather(x, idx.reshape(1, n))


def _experts_kernel(te_ref, nv_ref, *refs, n_parts):
    x_refs = refs[:n_parts]
    wg_ref, wu_ref, wd_ref = refs[n_parts:n_parts + 3]
    y_refs = refs[n_parts + 3:2 * n_parts + 3]
    i = pl.program_id(0)
    n_valid = nv_ref[i]
    tm = x_refs[0].shape[0]
    dff = wg_ref.shape[2]

    @pl.when(n_valid > 0)
    def _():
        row = lax.broadcasted_iota(jnp.int32, (tm, 1), 0)
        x = jnp.where(row < n_valid, _load_packed(x_refs), 0.0).astype(BF16)
        acc = None
        for lo in range(0, dff, MOE_FF_CHUNK):
            sl = slice(lo, lo + MOE_FF_CHUNK)
            hid = (_silu(_dot(x, wg_ref[0, :, sl])) * _dot(x, wu_ref[0, :, sl])).astype(BF16)
            part = _dot(hid, wd_ref[0, sl, :])
            acc = part if acc is None else acc + part
        _store_packed(y_refs, acc)

    @pl.when(n_valid == 0)
    def _():
        for ref in y_refs:
            ref[...] = jnp.zeros_like(ref)


def _experts(xs_parts, tile_expert, n_valid, wg, wu, wd):
    n_parts = len(xs_parts)
    n_slots, words = xs_parts[0].shape
    n_e, d, dff = wg.shape
    tm = MOE_ROW_TILE
    rows_spec = pl.BlockSpec((tm, words), lambda i, te, nv: (i, 0))
    weights = lambda shape: pl.BlockSpec((1,) + shape, lambda i, te, nv: (te[i], 0, 0))
    est = 2 * 3 * d * dff * 2 + 4 * tm * d * 2 + 3 * tm * MOE_FF_CHUNK * 4 + 3 * tm * d * 4
    grid_spec = pltpu.PrefetchScalarGridSpec(
        num_scalar_prefetch=2,
        grid=(n_slots // tm,),
        in_specs=[rows_spec] * n_parts + [weights((d, dff)), weights((d, dff)), weights((dff, d))],
        out_specs=[rows_spec] * n_parts,
    )
    return pl.pallas_call(
        functools.partial(_experts_kernel, n_parts=n_parts),
        grid_spec=grid_spec,
        out_shape=[jax.ShapeDtypeStruct((n_slots, words), jnp.int32)] * n_parts,
        compiler_params=_params(("arbitrary",), est),
        name="moe_experts",
    )(tile_expert, n_valid, *xs_parts, wg, wu, wd)


def _combine_kernel(*refs, n_parts):
    y1_refs = refs[:n_parts]
    y2_refs = refs[n_parts:2 * n_parts]
    gw_ref, x_ref, mod_ref, g_ref, o_ref = refs[2 * n_parts:]
    gw = gw_ref[0]
    y = gw[:, 0:1] * _load_packed(y1_refs) + gw[:, 1:2] * _load_packed(y2_refs)
    o_ref[0] = x_ref[0] + mod_ref[0][5:6] * (_rms(y) * g_ref[...])


def _combine(ys2_parts, gw, x1, mod, mod_row0, g_post):
    n_parts = len(ys2_parts)
    nb, rows, d = x1.shape
    words = ys2_parts[0].shape[-1]
    tm = min(TOKEN_TILE, rows)
    tok = lambda n: pl.BlockSpec((1, tm, n), lambda b, i: (b, i, 0))
    ysp = lambda k: pl.BlockSpec((1, tm, words), lambda b, i: (k * nb + b, i, 0))
    est = 2 * tm * (d * 2 * 4 + d * 4 + V7X_LANES * 4) + 3 * tm * d * 4
    return pl.pallas_call(
        functools.partial(_combine_kernel, n_parts=n_parts),
        grid=(nb, rows // tm),
        in_specs=[ysp(0)] * n_parts + [ysp(1)] * n_parts + [
            tok(V7X_LANES), tok(d),
            pl.BlockSpec((1, 6, d), lambda b, i: (b + mod_row0, 0, 0)),
            pl.BlockSpec((1, d), lambda b, i: (0, 0))],
        out_specs=tok(d),
        out_shape=jax.ShapeDtypeStruct((nb, rows, d), F32),
        compiler_params=_params(("parallel", "parallel"), est),
        name="moe_combine",
    )(*ys2_parts, *ys2_parts, gw, x1, mod, g_post.reshape(1, d))


def _moe_routed(gw, eid, cnt, *h2_parts, x1, mod, mod_row0, experts, g_post):
    nb, rows, d = x1.shape
    t = nb * rows
    tm = min(TOKEN_TILE, rows)
    tile_base, tile_expert, n_valid, n_slots = _route_plan(cnt, t)
    pos = _slots(eid.reshape(t, V7X_LANES), tile_base, tm)
    idx = jnp.concatenate([pos[:, 0], pos[:, 1]])
    xs = [_scatter_rows(h.reshape(t, h.shape[-1]), idx, n_slots) for h in h2_parts]
    ys = _experts(xs, tile_expert, n_valid, *experts)
    ys2 = [_gather_rows(y, idx).reshape(2 * nb, rows, y.shape[-1]) for y in ys]
    return _combine(ys2, gw, x1, mod, mod_row0, g_post)


def _cast_kernel(w_ref, o_ref):
    o_ref[...] = w_ref[...].astype(o_ref.dtype)


def _to_bf16(w):
    shape = w.shape
    w3 = w.reshape((-1,) + shape[-2:])
    n, r, c = w3.shape
    rb = r
    while rb * c * 4 > CAST_BLOCK_BYTES and rb % 16 == 0:
        rb //= 2
    spec = pl.BlockSpec((1, rb, c), lambda e, i: (e, i, 0))
    out = pl.pallas_call(
        _cast_kernel,
        grid=(n, r // rb),
        in_specs=[spec],
        out_specs=spec,
        out_shape=jax.ShapeDtypeStruct(w3.shape, BF16),
        compiler_params=_params(("parallel", "parallel"), 2 * rb * c * 6),
        name="cast_bf16",
    )(w3)
    return out.reshape(shape)
def _pack_w_in(w):
    a = 2 * GLA_KDIM + 2 * GLA_WIDTH
    lr = 2 * GLA_GATE_RANK
    reps = LR_PAD // lr
    return jnp.concatenate([w[:, :a], w[:, a + lr:]] + [w[:, a:a + lr]] * reps, axis=1).astype(BF16)


def _pack_gate(w_gate, b_gate):
    lr = 2 * GLA_GATE_RANK
    packed = []
    for d in range(2):
        wd = jnp.zeros((lr, GLA_KDIM), F32).at[d * GLA_GATE_RANK:(d + 1) * GLA_GATE_RANK].set(w_gate[d])
        hi, lo = _split(wd)
        packed.append(jnp.concatenate([hi, hi, lo, lo], axis=0))
    return jnp.stack(packed), b_gate.reshape(2, 1, GLA_KDIM)


def kernel(x, c, ctx, c_ctx, w_ada, b_ada, g_pre_mix, g_post_mix, g_pre_ffn, g_post_ffn, w_in,
           gla_w_gate, gla_b_gate, gla_g_norm, na_rpb, w_out, ffn_w_gate, ffn_w_up, ffn_w_down,
           moe_w_router, moe_w_gate, moe_w_up, moe_w_down):
    B, L, D = x.shape
    Lc = ctx.shape[1]
    depth = w_ada.shape[0]
    rows = L // GRID_W
    ctx_row = B
    n_cond = -(-(B + 1) // 8) * 8
    cond = jnp.zeros((n_cond, D), F32).at[:B].set(c).at[B].set(c_ctx)
    mod = _modulation(cond, w_ada, b_ada)
    tables = _rope_tables(L)
    ctx = ctx.reshape(1, B * Lc, D)

    for i in range(depth):
        last = i == depth - 1
        j = i // 2
        w = _pack_w_in(w_in[i])
        wgp, bgp = _pack_gate(gla_w_gate[i], gla_b_gate[i])
        qk, gv, gr, nq, nk, nv, bf, bb = _inproj(x, mod[i], 0, g_pre_mix[i], w, wgp, bgp, tables)
        cparts = _inproj(ctx, mod[i], ctx_row, g_pre_mix[i], w, wgp, bgp)
        cqk, cgv, cgr, cnq, cnk, cnv, cbf, cbb = [t.reshape(B, Lc, t.shape[-1]) for t in cparts]
        ga, gac = _gla(qk, gv, gr, bf, bb, cqk, cgv, cgr, cbf, cbb, gla_g_norm[i])
        na = _na(nq, nk, nv, cnk, cnv, _na_bias_table(na_rpb[i], rows))
        if i % 2 == 0:
            ffn = (_to_bf16(ffn_w_gate[j]), _to_bf16(ffn_w_up[j]), _to_bf16(ffn_w_down[j]), g_post_ffn[i])

            def mix_and_ffn(ga_, na_, x_, row0):
                return _outproj(ga_, na_, x_, mod[i], row0, w_out[i], g_post_mix[i], g_pre_ffn[i], ffn=ffn)
        else:
            experts = (_to_bf16(moe_w_gate[j]), _to_bf16(moe_w_up[j]), _to_bf16(moe_w_down[j]))

            def mix_and_ffn(ga_, na_, x_, row0):
                x1, *routed = _outproj(ga_, na_, x_, mod[i], row0, w_out[i], g_post_mix[i], g_pre_ffn[i],
                                       w_router=moe_w_router[j])
                return _moe_routed(*routed, x1=x1, mod=mod[i], mod_row0=row0, experts=experts, g_post=g_post_ffn[i])
        x = mix_and_ffn(ga, na, x, 0)
        if not last:
            nac = _ctx_attn(cnq, cnk, cnv)
            ctx = mix_and_ffn(gac.reshape(1, B * Lc, GLA_WIDTH), nac.reshape(1, B * Lc, NA_WIDTH), ctx, ctx_row)
    return x
```

```python
import functools

import numpy as np
import jax
import jax.numpy as jnp
from jax import lax
from jax.experimental import pallas as pl
from jax.experimental.pallas import tpu as pltpu
from jax.experimental.pallas import tpu_sc as plsc

F32 = jnp.float32
BF16 = jnp.bfloat16

GRID_W = 64
GLA_HEADS = 4
GLA_DV = 128
GLA_DK = 64
GLA_KDIM = GLA_HEADS * GLA_DK
GLA_WIDTH = GLA_HEADS * GLA_DV
GLA_GATE_RANK = 16
GLA_GATE_NORM = 16.0
NA_HEADS = 8
NA_DH = 64
NA_WIDTH = NA_HEADS * NA_DH
NA_WIN_H = 8
NA_WIN_W = 16
ROPE_BASE = 10000.0
N_EXPERTS = 8
EPS = 1e-6

V7X_LANES = 128
V7X_VMEM_BYTES = 64 * 1024 * 1024
V7X_VMEM_USABLE = V7X_VMEM_BYTES - 8 * 1024 * 1024

TOKEN_TILE = 512
GLA_CHUNK = 128
INPROJ_ROW_BLOCKS = 2
GLA_UNROLL = 4
NA_ROWS_PER_STEP = 8
NA_ROWS_PER_ITER = 2
MASK_VALUE = -1e30
MOE_FF_CHUNK = 1792
MOE_ROW_TILE = 512
SC_ROW_WINDOW = 128
CAST_BLOCK_BYTES = 4 << 20
SC_ROW_WORDS = 256
LR_PAD = V7X_LANES


def _vmem_limit(estimate_bytes):
    return int(min(V7X_VMEM_USABLE, estimate_bytes * 5 // 4 + (4 << 20)))


def _params(semantics, vmem_estimate):
    return pltpu.CompilerParams(dimension_semantics=semantics, vmem_limit_bytes=_vmem_limit(vmem_estimate))


def _dot(a, b):
    return jnp.dot(a, b, preferred_element_type=F32)


def _dot_nt(a, b):
    return lax.dot_general(a, b, (((1,), (1,)), ((), ())), preferred_element_type=F32)


def _dot_tn(a, b):
    return lax.dot_general(a, b, (((0,), (0,)), ((), ())), preferred_element_type=F32)


def _split(x):
    hi = x.astype(BF16)
    lo = (x - hi.astype(F32)).astype(BF16)
    return hi, lo


def _dot3(a, b):
    ah, al = _split(a)
    bh, bl = _split(b)
    return _dot(ah, bh) + (_dot(al, bh) + _dot(ah, bl))


def _sigmoid(x):
    return 1.0 / (1.0 + jnp.exp(-x))


def _silu(x):
    return x * _sigmoid(x)


def _rms(x):
    return x * lax.rsqrt(jnp.mean(x * x, axis=-1, keepdims=True) + EPS)


def _mod_kernel(c_ref, w_ref, b_ref, o_ref):
    o_ref[0] = _dot3(_silu(c_ref[...]), w_ref[0]) + b_ref[0]


def _modulation(cond, w_ada, b_ada):
    depth, d, n = w_ada.shape
    rows = cond.shape[0]
    tn = 512
    out = pl.pallas_call(
        _mod_kernel,
        grid=(depth, n // tn),
        in_specs=[
            pl.BlockSpec((rows, d), lambda l, j: (0, 0)),
            pl.BlockSpec((1, d, tn), lambda l, j: (l, 0, j)),
            pl.BlockSpec((1, 1, tn), lambda l, j: (l, 0, j)),
        ],
        out_specs=pl.BlockSpec((1, rows, tn), lambda l, j: (l, 0, j)),
        out_shape=jax.ShapeDtypeStruct((depth, rows, n), F32),
        compiler_params=_params(("parallel", "parallel"), 3 * d * tn * 4 * 2),
        name="modulation",
    )(cond, w_ada, b_ada.reshape(depth, 1, n))
    return out.reshape(depth, rows, 6, d)


def _log_decay(logit):
    return (jnp.minimum(logit, 0.0) - jnp.log(1.0 + jnp.exp(-jnp.abs(logit)))) * (1.0 / GLA_GATE_NORM)


def _inproj_kernel(*refs, rope):
    if rope:
        x_ref, mod_ref, g_ref, w_ref, wg_ref, bg_ref, cos_ref, sin_ref = refs[:8]
    else:
        x_ref, mod_ref, g_ref, w_ref, wg_ref, bg_ref = refs[:6]
    qk_ref, gv_ref, gr_ref, nq_ref, nk_ref, nv_ref, bf_ref, bb_ref = refs[-8:]
    m = mod_ref[0]
    tm = x_ref.shape[1]
    plain_refs = (gv_ref, gr_ref, nq_ref, nk_ref, nv_ref)
    offs = np.cumsum([0, qk_ref.shape[-1]] + [r.shape[-1] for r in plain_refs])
    C = GLA_CHUNK
    row = lax.broadcasted_iota(jnp.int32, (C, 2 * C), 0)
    colm = lax.broadcasted_iota(jnp.int32, (C, 2 * C), 1) % C

    def rows_block(lo, n_rows):
        rs = slice(lo, lo + n_rows)
        h = (_rms(x_ref[0, rs, :]) * g_ref[...] * (1.0 + m[1:2]) + m[0:1]).astype(BF16)

        def project(k):
            ref = plain_refs[k]
            ref[0, rs, :] = _dot(h, w_ref[:, offs[k + 1]:offs[k + 2]]).astype(ref.dtype)

        project(0)
        lr = _dot(h, w_ref[:, offs[-1]:offs[-1] + LR_PAD])
        lr_hi, lr_lo = _split(lr)
        lr_lane = lax.broadcasted_iota(jnp.int32, (1, LR_PAD), 1)
        lr_mix = jnp.where((lr_lane // (2 * GLA_GATE_RANK)) % 2 == 0, lr_hi, lr_lo)
        logits = [_dot(lr_mix, wg_ref[d]) + bg_ref[d] for d in range(2)]
        n = qk_ref.shape[-1]
        qk = _dot(h, w_ref[:, 0:n])
        lane = lax.broadcasted_iota(jnp.int32, (1, n), 1)
        if rope:
            reps = n // cos_ref.shape[-1]
            cos = jnp.concatenate([cos_ref[rs, :]] * reps, axis=1)
            sin = jnp.concatenate([sin_ref[rs, :]] * reps, axis=1)
            quarter = GLA_DK // 4
            first = (lane % (2 * quarter)) < quarter
            partner = jnp.where(first, pltpu.roll(qk, n - quarter, 1), pltpu.roll(qk, quarter, 1))
            qk = qk * cos + partner * sin
        qk_ref[0, rs, :] = jnp.where(lane < GLA_KDIM, qk * (GLA_DK ** -0.5), qk).astype(qk_ref.dtype)
        for d, out_ref in enumerate((bf_ref, bb_ref)):
            gh, gl = _split(_log_decay(logits[d]))
            project(1 + 2 * d)
            tri = (row >= colm) if d == 0 else (row <= colm)
            tmat = jnp.where(tri, 1.0, 0.0).astype(BF16)
            for c in range(n_rows // C):
                sl = slice(c * C, (c + 1) * C)
                out_ref[0, lo + c * C:lo + (c + 1) * C, :] = _dot(tmat, jnp.concatenate([gh[sl], gl[sl]], axis=0))
            project(2 + 2 * d)

    n_blocks = INPROJ_ROW_BLOCKS if tm % (INPROJ_ROW_BLOCKS * C) == 0 else 1
    for blk in range(n_blocks):
        rows_block(blk * (tm // n_blocks), tm // n_blocks)


def _inproj(x, mod, mod_row0, g, w, wgp, bgp, rope_tables=None):
    nb, rows, d = x.shape
    tm = min(TOKEN_TILE, rows)
    rope = rope_tables is not None
    widths = (2 * GLA_KDIM, GLA_WIDTH, GLA_WIDTH, NA_WIDTH, NA_WIDTH, NA_WIDTH, GLA_KDIM, GLA_KDIM)
    dtypes = (BF16,) * 6 + (F32, F32)
    tok = lambda n: pl.BlockSpec((1, tm, n), lambda b, i: (b, i, 0))
    const = lambda shape: pl.BlockSpec(shape, lambda b, i: (0,) * len(shape))
    args = [x, mod, g.reshape(1, d), w, wgp, bgp]
    in_specs = [tok(d), pl.BlockSpec((1, 6, d), lambda b, i: (b + mod_row0, 0, 0)), const((1, d)),
                const(w.shape), const(wgp.shape), const(bgp.shape)]
    if rope:
        args += list(rope_tables)
        in_specs += [pl.BlockSpec((tm, t.shape[1]), lambda b, i: (i, 0)) for t in rope_tables]
    est = 2 * (tm * d * 4 + d * w.shape[1] * 2 + sum(widths) * tm * 4) + tm * d * 8
    return pl.pallas_call(
        functools.partial(_inproj_kernel, rope=rope),
        grid=(nb, rows // tm),
        in_specs=in_specs,
        out_specs=[tok(n) for n in widths],
        out_shape=[jax.ShapeDtypeStruct((nb, rows, n), dt) for n, dt in zip(widths, dtypes)],
        compiler_params=_params(("parallel", "parallel"), est),
        name="inproj",
    )(*args)


def _gla_kernel(q_ref, k_ref, v_ref, r_ref, bf_ref, bb_ref, cq_ref, ck_ref, cv_ref, cr_ref, cbf_ref, cbb_ref,
                gain_ref, o_ref, oc_ref, of_ref, ocf_ref, st_ref):
    C = GLA_CHUNK
    L = q_ref.shape[1]
    Lc = cq_ref.shape[1]
    row = lax.broadcasted_iota(jnp.int32, (C, C), 0)
    col = lax.broadcasted_iota(jnp.int32, (C, C), 1)
    lane = lax.broadcasted_iota(jnp.int32, (1, 2 * GLA_DK), 1)
    head0 = lane < GLA_DK
    st_row = lax.broadcasted_iota(jnp.int32, (2 * GLA_DV, 2 * GLA_DK), 0) // GLA_DV
    st_col = lax.broadcasted_iota(jnp.int32, (2 * GLA_DV, 2 * GLA_DK), 1) // GLA_DK
    blockdiag = st_row == st_col
    gain = gain_ref[...]

    def chunk_steps(refs, chunks, acc_ref):
        rq, rk, rv = refs[:3]
        dirs = (0, 1)
        sls = [pl.ds(pl.multiple_of(i * C, C), C) for i in chunks]
        tris = [row >= col, row <= col]
        vs = [rv[0, sl, :] for sl in sls]
        qfs = [rq[0, sl, :].astype(F32) for sl in sls]
        kfs = [rk[0, sl, :].astype(F32) for sl in sls]
        bs = [refs[3 + d][0, sls[d], :] for d in dirs]
        b_mids = [b[C // 2:C // 2 + 1] for b in bs]
        b_edges = [bs[0][C - 1:C], bs[1][0:1]]
        qes = [(qfs[d] * jnp.exp(bs[d] - b_mids[d])).astype(BF16) for d in dirs]
        kes = [(kfs[d] * jnp.exp(b_mids[d] - bs[d])).astype(BF16) for d in dirs]
        zero = jnp.zeros_like(qes[0])
        lhss = [jnp.concatenate([jnp.where(head0, qe, zero), jnp.where(head0, zero, qe)], axis=0) for qe in qes]
        scores = [_dot_nt(lhss[d], kes[d]) for d in dirs]
        qbs = [(qfs[d] * jnp.exp(bs[d])).astype(BF16) for d in dirs]
        kds = [(kfs[d] * jnp.exp(b_edges[d] - bs[d])).astype(BF16) for d in dirs]
        sts = [st_ref[d] for d in dirs]
        inter = [_dot_nt(qbs[d], sts[d].astype(BF16)) for d in dirs]
        dss = [_dot_tn(vs[d], kds[d]) for d in dirs]
        ams = [jnp.where(jnp.concatenate([tris[d], tris[d]], axis=0), scores[d], 0.0).astype(BF16) for d in dirs]
        intra = [jnp.concatenate([_dot(ams[d][:C], vs[d][:, :GLA_DV]), _dot(ams[d][C:], vs[d][:, GLA_DV:])], axis=1)
                 for d in dirs]
        for d in dirs:
            st_ref[d] = sts[d] * jnp.exp(b_edges[d]) + jnp.where(blockdiag, dss[d], 0.0)
            acc_ref[d, sls[d], :] = intra[d] + inter[d]

    def finish(total, r):
        outs = []
        for h in range(2):
            oh = _rms(total[:, h * GLA_DV:(h + 1) * GLA_DV]) * gain
            outs.append(oh * _silu(r[:, h * GLA_DV:(h + 1) * GLA_DV].astype(F32)))
        return jnp.concatenate(outs, axis=1)

    def scan(refs, n, acc_ref, r_ref_, out_ref):
        def body(j, carry):
            chunk_steps(refs, (j, n - 1 - j), acc_ref)
            return carry

        lax.fori_loop(0, n, body, 0, unroll=min(GLA_UNROLL, n))

        def fin(i, carry):
            sl = pl.ds(pl.multiple_of(i * C, C), C)
            out_ref[0, sl, :] = finish(acc_ref[0, sl, :] + acc_ref[1, sl, :], r_ref_[0, sl, :]).astype(out_ref.dtype)
            return carry

        lax.fori_loop(0, n, fin, 0, unroll=min(GLA_UNROLL, n))

    st_ref[...] = jnp.zeros_like(st_ref)
    scan((cq_ref, ck_ref, cv_ref, cbf_ref, cbb_ref), Lc // C, ocf_ref, cr_ref, oc_ref)
    scan((q_ref, k_ref, v_ref, bf_ref, bb_ref), L // C, of_ref, r_ref, o_ref)


def _gla(qk, gv, gr, bf, bb, cqk, cgv, cgr, cbf, cbb, gain):
    B, L, _ = qk.shape
    Lc = cqk.shape[1]
    pair = 2 * GLA_DK
    pv = 2 * GLA_DV
    n_pair = GLA_HEADS // 2
    seq = lambda rows, n, off: pl.BlockSpec((1, rows, n), lambda b, p: (b, 0, p + off))
    const = lambda shape: pl.BlockSpec(shape, lambda b, p: (0,) * len(shape))
    est = (2 * (2 * L * pair * 2 + 2 * L * pv * 2 + 2 * L * pair * 4 + L * pv * 2)
           + 2 * L * pv * 4 + (8 << 20))
    return pl.pallas_call(
        _gla_kernel,
        grid=(B, n_pair),
        in_specs=[
            seq(L, pair, 0), seq(L, pair, n_pair), seq(L, pv, 0), seq(L, pv, 0), seq(L, pair, 0), seq(L, pair, 0),
            seq(Lc, pair, 0), seq(Lc, pair, n_pair), seq(Lc, pv, 0), seq(Lc, pv, 0), seq(Lc, pair, 0),
            seq(Lc, pair, 0),
            const((1, GLA_DV)),
        ],
        out_specs=[seq(L, pv, 0), seq(Lc, pv, 0)],
        out_shape=[jax.ShapeDtypeStruct((B, L, GLA_WIDTH), BF16), jax.ShapeDtypeStruct((B, Lc, GLA_WIDTH), BF16)],
        scratch_shapes=[pltpu.VMEM((2, L, pv), F32), pltpu.VMEM((2, Lc, pv), F32), pltpu.VMEM((2, pv, pair), F32)],
        compiler_params=_params(("parallel", "parallel"), est),
        name="gla",
    )(qk, qk, gv, gr, bf, bb, cqk, cqk, cgv, cgr, cbf, cbb, gain.reshape(1, GLA_DV))


def _rope_tables(L):
    pos = np.arange(L)
    half = GLA_DK // 4
    inv = ROPE_BASE ** (-np.arange(half, dtype=np.float64) / half)
    lane = np.arange(2 * GLA_DK)
    jj = lane % (GLA_DK // 2)
    use_col = (lane % GLA_DK) >= GLA_DK // 2
    p = np.where(use_col[None, :], (pos % GRID_W)[:, None], (pos // GRID_W)[:, None]).astype(np.float64)
    ang = p * inv[jj % half][None, :]
    first = jj < half
    cos = np.cos(ang)
    sin = np.where(first[None, :], -np.sin(ang), np.sin(ang))
    return jnp.asarray(cos, F32), jnp.asarray(sin, F32)


def _softmax_pv(s_parts, v_parts):
    m = s_parts[0].max(axis=-1, keepdims=True)
    for s in s_parts[1:]:
        m = jnp.maximum(m, s.max(axis=-1, keepdims=True))
    den = None
    acc = None
    for s, v in zip(s_parts, v_parts):
        p = jnp.exp(s - m)
        den = p.sum(axis=-1, keepdims=True) if den is None else den + p.sum(axis=-1, keepdims=True)
        pv = _dot(p.astype(BF16), v)
        acc = pv if acc is None else acc + pv
    return acc / den


def _na_window_start():
    cols = np.arange(GRID_W)
    return np.clip(cols - NA_WIN_W // 2, 0, GRID_W - NA_WIN_W)


def _na_kernel(q_ref, k_ref, v_ref, ck_ref, cv_ref, bias_ref, o_ref):
    W = GRID_W
    rows = k_ref.shape[1] // W
    n_loc = bias_ref.shape[2]
    kh = n_loc // W
    r0 = pl.program_id(1) * NA_ROWS_PER_STEP
    pair = 2 * NA_DH
    lane = lax.broadcasted_iota(jnp.int32, (1, pair), 1)
    head0 = lane < NA_DH
    scale = jnp.asarray(NA_DH ** -0.5, q_ref.dtype)

    n_pairs = NA_HEADS // 2
    lanes = [slice(p * pair, (p + 1) * pair) for p in range(n_pairs)]

    def rows_body(it, carry):
        units = []
        for j in range(NA_ROWS_PER_ITER):
            rr = it * NA_ROWS_PER_ITER + j
            r = r0 + rr
            rs = jnp.clip(r - kh // 2, 0, rows - kh)
            qs = pl.ds(pl.multiple_of(rr * W, W), W)
            ks = pl.ds(pl.multiple_of(rs * W, W), n_loc)
            units += [(qs, ks, rs - r + NA_WIN_H - 1, p) for p in range(n_pairs)]
        sts = []
        for qs, ks, dr, p in units:
            q = q_ref[0, qs, lanes[p]] * scale
            zero = jnp.zeros_like(q)
            q2 = jnp.concatenate([jnp.where(head0, q, zero), jnp.where(head0, zero, q)], axis=0)
            sts.append((_dot_nt(k_ref[0, ks, lanes[p]], q2), _dot_nt(ck_ref[0, :, lanes[p]], q2)))
        pts = []
        for (qs, ks, dr, p), (sl, sc) in zip(units, sts):
            sl = sl + bias_ref[p, dr]
            m = jnp.maximum(sl.max(axis=0, keepdims=True), sc.max(axis=0, keepdims=True))
            pts.append((jnp.exp((sl - m).astype(BF16)), jnp.exp((sc - m).astype(BF16))))
        outs = []
        ones_l = jnp.ones((n_loc, pair), BF16)
        ones_c = jnp.ones((ck_ref.shape[1], pair), BF16)
        for (qs, ks, dr, p), (pl_, pc) in zip(units, pts):
            pv = (_dot_tn(pl_, jnp.concatenate([v_ref[0, ks, lanes[p]], ones_l], axis=1))
                  + _dot_tn(pc, jnp.concatenate([cv_ref[0, :, lanes[p]], ones_c], axis=1)))
            o2 = pv[:, :pair] / pv[:, pair:]
            outs.append(jnp.where(head0, o2[:W], o2[W:]))
        for j in range(NA_ROWS_PER_ITER):
            qs = units[j * n_pairs][0]
            o_ref[0, qs, :] = jnp.concatenate(outs[j * n_pairs:(j + 1) * n_pairs], axis=1).astype(o_ref.dtype)
        return carry

    lax.fori_loop(0, NA_ROWS_PER_STEP // NA_ROWS_PER_ITER, rows_body, 0, unroll=4)


def _na(nq, nk, nv, cnk, cnv, bias):
    B, L, n = nq.shape
    Lc = cnk.shape[1]
    tq = NA_ROWS_PER_STEP * GRID_W
    full = lambda rows: pl.BlockSpec((1, rows, n), lambda b, i: (b, 0, 0))
    est = 2 * (2 * L * n * 2 + 2 * Lc * n * 2 + bias.size * 4 + 2 * tq * n * 2) + (8 << 20)
    return pl.pallas_call(
        _na_kernel,
        grid=(B, L // tq),
        in_specs=[
            pl.BlockSpec((1, tq, n), lambda b, i: (b, i, 0)),
            full(L), full(L), full(Lc), full(Lc),
            pl.BlockSpec(bias.shape, lambda b, i: (0, 0, 0, 0)),
        ],
        out_specs=pl.BlockSpec((1, tq, n), lambda b, i: (b, i, 0)),
        out_shape=jax.ShapeDtypeStruct((B, L, n), BF16),
        compiler_params=_params(("parallel", "parallel"), est),
        name="neighbourhood_attention",
    )(nq, nk, nv, cnk, cnv, bias)


def _na_bias_table(rpb, rows):
    kh = min(NA_WIN_H, rows)
    start = _na_window_start()
    kc = np.arange(GRID_W)
    inside = (kc[None, :] >= start[:, None]) & (kc[None, :] < start[:, None] + NA_WIN_W)
    sel = np.zeros((2 * NA_WIN_W - 1, GRID_W, GRID_W), np.float32)
    qq, kk = np.nonzero(inside)
    sel[kk - qq + NA_WIN_W - 1, qq, kk] = 1.0
    by_row = jnp.stack([rpb[:, d:d + kh, :] for d in range(NA_WIN_H)], axis=1)
    t = jnp.einsum('hdic,cqk->hdikq', by_row, jnp.asarray(sel), precision=lax.Precision.HIGHEST)
    t = t + jnp.asarray(np.where(inside, 0.0, MASK_VALUE).T, F32)[None, None, None, :, :]
    t = t.reshape(NA_HEADS // 2, 2, NA_WIN_H, kh * GRID_W, GRID_W)
    return jnp.transpose(t, (0, 2, 3, 1, 4)).reshape(NA_HEADS // 2, NA_WIN_H, kh * GRID_W, 2 * GRID_W)


def _ctx_attn_kernel(q_ref, k_ref, v_ref, o_ref):
    pair = 2 * NA_DH
    lane = lax.broadcasted_iota(jnp.int32, (1, pair), 1)
    head0 = lane < NA_DH
    Lc = q_ref.shape[1]
    outs = []
    for p in range(NA_HEADS // 2):
        ls = slice(p * pair, (p + 1) * pair)
        q = q_ref[0, :, ls]
        zero = jnp.zeros_like(q)
        q2 = jnp.concatenate([jnp.where(head0, q, zero), jnp.where(head0, zero, q)], axis=0)
        s = _dot_nt(q2, k_ref[0, :, ls]) * (NA_DH ** -0.5)
        o2 = _softmax_pv([s], [v_ref[0, :, ls]])
        outs.append(jnp.where(head0, o2[:Lc], o2[Lc:]))
    o_ref[0] = jnp.concatenate(outs, axis=1).astype(o_ref.dtype)


def _ctx_attn(cnq, cnk, cnv):
    B, Lc, n = cnq.shape
    spec = pl.BlockSpec((1, Lc, n), lambda b: (b, 0, 0))
    return pl.pallas_call(
        _ctx_attn_kernel,
        grid=(B,),
        in_specs=[spec, spec, spec],
        out_specs=spec,
        out_shape=jax.ShapeDtypeStruct((B, Lc, n), BF16),
        compiler_params=_params(("parallel",), 8 * Lc * n * 2 + (8 << 20)),
        name="context_attention",
    )(cnq, cnk, cnv)


def _pack_bf16_pairs(x):
    n = x.shape[1] // 2
    lo = lax.bitcast_convert_type(x[:, :n].astype(BF16).astype(F32), jnp.int32)
    hi = lax.bitcast_convert_type(x[:, n:].astype(BF16).astype(F32), jnp.int32)
    return lax.shift_right_logical(lo, 16) | (hi & jnp.int32(-65536))


def _unpack_bf16_pairs(p):
    lo = lax.bitcast_convert_type(lax.shift_left(p, 16), F32)
    hi = lax.bitcast_convert_type(p & jnp.int32(-65536), F32)
    return jnp.concatenate([lo, hi], axis=1)


def _store_packed(refs, x):
    n = 2 * SC_ROW_WORDS
    for p, ref in enumerate(refs):
        ref[...] = _pack_bf16_pairs(x[:, p * n:(p + 1) * n]).reshape(ref.shape)


def _load_packed(refs):
    return jnp.concatenate([_unpack_bf16_pairs(ref[...].reshape(ref.shape[-2:])) for ref in refs], axis=1)


def _route(logits):
    lane = lax.broadcasted_iota(jnp.int32, logits.shape, 1)
    big = jnp.int32(logits.shape[1])
    t1 = logits.max(axis=-1, keepdims=True)
    i1 = jnp.where(logits == t1, lane, big).min(axis=-1, keepdims=True)
    rest = jnp.where(lane == i1, -jnp.inf, logits)
    t2 = rest.max(axis=-1, keepdims=True)
    i2 = jnp.where(rest == t2, lane, big).min(axis=-1, keepdims=True)
    e2 = jnp.exp(t2 - t1)
    return i1, i2, 1.0 / (1.0 + e2), e2 / (1.0 + e2)


def _outproj_kernel(*refs, dense_ffn):
    ga_ref, na_ref, x_ref, mod_ref, wa_ref, wb_ref, gp_ref, gf_ref = refs[:8]
    m = mod_ref[0]
    tm = x_ref.shape[1]
    th = tm // 2 if dense_ffn else tm
    halves = [slice(lo, lo + th) for lo in range(0, tm, th)]
    ys = [_dot(ga_ref[0, hs, :], wa_ref[...]) + _dot(na_ref[0, hs, :], wb_ref[...]) for hs in halves]
    x1s = [x_ref[0, hs, :] + m[2:3] * (_rms(y) * gp_ref[...]) for hs, y in zip(halves, ys)]
    h2s = [_rms(x1) * gf_ref[...] * (1.0 + m[4:5]) + m[3:4] for x1 in x1s]
    if dense_ffn:
        wg_ref, wu_ref, wd_ref, g2_ref, o_ref = refs[8:]
        ups = [(_dot(h, wg_ref[...]), _dot(h, wu_ref[...])) for h in [h2.astype(BF16) for h2 in h2s]]
        hids = [(_silu(g) * u).astype(BF16) for g, u in ups]
        downs = [_dot(hid, wd_ref[...]) for hid in hids]
        for hs, x1, dn in zip(halves, x1s, downs):
            o_ref[0, hs, :] = x1 + m[5:6] * (_rms(dn) * g2_ref[...])
    else:
        wr_ref, x1_ref, gw_ref, eid_ref, cnt_ref, *h2_refs = refs[8:]
        lanes = wr_ref.shape[1] // 2
        prods = [_dot(jnp.concatenate(_split(h2), axis=0), wr_ref[...]) for h2 in h2s]
        lane = lax.broadcasted_iota(jnp.int32, (th, lanes), 1)
        counts = None
        for hs, x1, h2, prod in zip(halves, x1s, h2s, prods):
            x1_ref[0, hs, :] = x1
            for p, ref in enumerate(h2_refs):
                n = 2 * SC_ROW_WORDS
                ref[0, hs, :] = _pack_bf16_pairs(h2[:, p * n:(p + 1) * n])
            logits = (prod[:th, :lanes] + prod[th:, :lanes]) + (prod[:th, lanes:] + prod[th:, lanes:])
            i1, i2, w1, w2 = _route(jnp.where(lane < N_EXPERTS, logits, -jnp.inf))
            gw_ref[0, hs, :] = jnp.where(lane == 0, w1, jnp.where(lane == 1, w2, 0.0))
            eid_ref[0, hs, :] = jnp.where(lane == 0, i1, jnp.where(lane == 1, i2, 0))
            chosen = jnp.where(lane == i1, 1.0, jnp.where(lane == i2, 1.0, 0.0)).sum(axis=0, keepdims=True)
            counts = chosen if counts is None else counts + chosen
        cnt_ref[0] = jnp.broadcast_to(counts, cnt_ref.shape[1:])


def _outproj(ga, na, x, mod, mod_row0, w_out, g_post, g_ffn, *, ffn=None, w_router=None):
    nb, rows, d = x.shape
    tm = min(TOKEN_TILE, rows)
    dense_ffn = ffn is not None
    tok = lambda n: pl.BlockSpec((1, tm, n), lambda b, i: (b, i, 0))
    const = lambda shape: pl.BlockSpec(shape, lambda b, i: (0,) * len(shape))
    once = lambda shape: pl.BlockSpec(shape, lambda b, i: (0,) * len(shape), pipeline_mode=pl.Buffered(1))
    wa = w_out[:GLA_WIDTH].astype(BF16)
    wb = w_out[GLA_WIDTH:].astype(BF16)
    args = [ga, na, x, mod, wa, wb, g_post.reshape(1, d), g_ffn.reshape(1, d)]
    in_specs = [tok(GLA_WIDTH), tok(NA_WIDTH), tok(d),
                pl.BlockSpec((1, 6, d), lambda b, i: (b + mod_row0, 0, 0)),
                const(wa.shape), const(wb.shape), const((1, d)), const((1, d))]
    est = 2 * (tm * d * (4 + 4 + 2) + 2 * tm * GLA_WIDTH * 2 + d * d * 2) + 4 * tm * d * 4
    if dense_ffn:
        wg, wu, wd, g2 = ffn
        dff = wg.shape[1]
        args += [wg, wu, wd, g2.reshape(1, d)]
        in_specs += [once(wg.shape), once(wu.shape), once(wd.shape), const((1, d))]
        out_specs = tok(d)
        out_shape = jax.ShapeDtypeStruct((nb, rows, d), F32)
        est += 3 * d * dff * 2 + 3 * tm * dff * 4
    else:
        nt = rows // tm
        wr = jnp.zeros((d, V7X_LANES), F32).at[:, :N_EXPERTS].set(w_router)
        wr = jnp.concatenate(_split(wr), axis=1)
        args.append(wr)
        in_specs.append(const(wr.shape))
        n_parts = d // (2 * SC_ROW_WORDS)
        out_specs = [tok(d), tok(V7X_LANES), tok(V7X_LANES),
                     pl.BlockSpec((1, 8, V7X_LANES), lambda b, i: (b * nt + i, 0, 0))] + [tok(SC_ROW_WORDS)] * n_parts
        out_shape = [jax.ShapeDtypeStruct((nb, rows, d), F32),
                     jax.ShapeDtypeStruct((nb, rows, V7X_LANES), F32),
                     jax.ShapeDtypeStruct((nb, rows, V7X_LANES), jnp.int32),
                     jax.ShapeDtypeStruct((nb * nt, 8, V7X_LANES), F32)]
        out_shape += [jax.ShapeDtypeStruct((nb, rows, SC_ROW_WORDS), jnp.int32)] * n_parts
    return pl.pallas_call(
        functools.partial(_outproj_kernel, dense_ffn=dense_ffn),
        grid=(nb, rows // tm),
        in_specs=in_specs,
        out_specs=out_specs,
        out_shape=out_shape,
        compiler_params=_params(("parallel", "parallel"), est),
        name="outproj_ffn" if dense_ffn else "outproj_router",
    )(*args)


def _slot_kernel(eid_ref, base_ref, pos_ref):
    eid = eid_ref[...]
    tm, lanes = eid.shape
    lane = lax.broadcasted_iota(jnp.int32, (tm, lanes), 1)
    i1 = eid[:, 0:1]
    i2 = eid[:, 1:2]
    chosen = jnp.where(lane == i1, 1.0, jnp.where(lane == i2, 1.0, 0.0)).astype(BF16)
    row = lax.broadcasted_iota(jnp.int32, (tm, tm), 0)
    col = lax.broadcasted_iota(jnp.int32, (tm, tm), 1)
    incl = jnp.where(row >= col, 1.0, 0.0).astype(BF16)
    slot = base_ref[0][0:1] + _dot(incl, chosen) - 1.0
    p1 = jnp.where(lane == i1, slot, 0.0).sum(axis=-1, keepdims=True).astype(jnp.int32)
    p2 = jnp.where(lane == i2, slot, 0.0).sum(axis=-1, keepdims=True).astype(jnp.int32)
    pos_ref[...] = jnp.where(lane == 0, p1, jnp.where(lane == 1, p2, 0))


def _slots(eid, tile_base, tm):
    t, lanes = eid.shape
    return pl.pallas_call(
        _slot_kernel,
        grid=(t // tm,),
        in_specs=[pl.BlockSpec((tm, lanes), lambda i: (i, 0)),
                  pl.BlockSpec((1, 8, lanes), lambda i: (i, 0, 0))],
        out_specs=pl.BlockSpec((tm, lanes), lambda i: (i, 0)),
        out_shape=jax.ShapeDtypeStruct((t, lanes), jnp.int32),
        compiler_params=_params(("parallel",), 8 * tm * lanes * 4 + 4 * tm * tm),
        name="moe_slots",
    )(eid, tile_base)


def _route_plan(cnt, tm_tokens):
    counts = cnt[:, 0, :N_EXPERTS].astype(jnp.int32)
    total = counts.sum(axis=0)
    padded = -(-total // MOE_ROW_TILE) * MOE_ROW_TILE
    start = jnp.cumsum(padded) - padded
    before = jnp.cumsum(counts, axis=0) - counts
    tile_base = (start[None, :] + before).astype(F32)
    tile_base = jnp.zeros((cnt.shape[0], 8, V7X_LANES), F32).at[:, :, :N_EXPERTS].set(tile_base[:, None, :])
    n_slots = tm_tokens * 2 + N_EXPERTS * MOE_ROW_TILE
    first_row = jnp.arange(n_slots // MOE_ROW_TILE, dtype=jnp.int32) * MOE_ROW_TILE
    tile_expert = jnp.minimum((first_row[:, None] >= (start + padded)[None, :]).sum(axis=1), N_EXPERTS - 1)
    n_valid = jnp.clip(start[tile_expert] + total[tile_expert] - first_row, 0, MOE_ROW_TILE)
    return tile_base, tile_expert.astype(jnp.int32), n_valid.astype(jnp.int32), n_slots


def _sc_mesh():
    return plsc.VectorSubcoreMesh(core_axis_name="core", subcore_axis_name="subcore")


def _scatter_rows(x, idx, n_out):
    t, w = x.shape
    n = idx.shape[0]
    win = SC_ROW_WINDOW
    n_blk = t // win

    @functools.partial(pl.kernel, out_type=jax.ShapeDtypeStruct((n_out, w), x.dtype), mesh=_sc_mesh(),
                       scratch_types=[], name="moe_dispatch")
    def scatter(x_hbm, i_hbm, o_hbm):
        def body(x_vmem, i_vmem):
            pltpu.sync_copy(x_vmem, o_hbm.at[i_vmem.at[0]])

        pltpu.emit_pipeline(
            body,
            grid=(n // win,),
            in_specs=[pl.BlockSpec((win, w), lambda i: (i % n_blk, 0)),
                      pl.BlockSpec((1, win), lambda i: (0, i))],
            out_specs=[],
            core_axis_name=("core", "subcore"),
            dimension_semantics=(pltpu.PARALLEL,),
        )(x_hbm, i_hbm)

    return scatter(x, idx.reshape(1, n))


def _gather_rows(x, idx):
    n = idx.shape[0]
    w = x.shape[1]
    win = SC_ROW_WINDOW

    @functools.partial(pl.kernel, out_type=jax.ShapeDtypeStruct((n, w), x.dtype), mesh=_sc_mesh(),
                       scratch_types=[], name="moe_combine_gather")
    def gather(x_hbm, i_hbm, o_hbm):
        def body(i_vmem, o_vmem):
            pltpu.sync_copy(x_hbm.at[i_vmem.at[0]], o_vmem)

        pltpu.emit_pipeline(
            body,
            grid=(n // win,),
            in_specs=[pl.BlockSpec((1, win), lambda i: (0, i))],
            out_specs=[pl.BlockSpec((win, w), lambda i: (i, 0))],
            core_axis_name=("core", "subcore"),
            dimension_semantics=(pltpu.PARALLEL,),
        )(i_hbm, o_hbm)

    return gather(x, idx.reshape(1, n))


def _experts_kernel(te_ref, nv_ref, *refs, n_parts):
    x_refs = refs[:n_parts]
    wg_ref, wu_ref, wd_ref = refs[n_parts:n_parts + 3]
    y_refs = refs[n_parts + 3:2 * n_parts + 3]
    i = pl.program_id(0)
    n_valid = nv_ref[i]
    tm = x_refs[0].shape[0]
    dff = wg_ref.shape[2]

    @pl.when(n_valid > 0)
    def _():
        row = lax.broadcasted_iota(jnp.int32, (tm, 1), 0)
        x = jnp.where(row < n_valid, _load_packed(x_refs), 0.0).astype(BF16)
        acc = None
        for lo in range(0, dff, MOE_FF_CHUNK):
            sl = slice(lo, lo + MOE_FF_CHUNK)
            hid = (_silu(_dot(x, wg_ref[0, :, sl])) * _dot(x, wu_ref[0, :, sl])).astype(BF16)
            part = _dot(hid, wd_ref[0, sl, :])
            acc = part if acc is None else acc + part
        _store_packed(y_refs, acc)

    @pl.when(n_valid == 0)
    def _():
        for ref in y_refs:
            ref[...] = jnp.zeros_like(ref)


def _experts(xs_parts, tile_expert, n_valid, wg, wu, wd):
    n_parts = len(xs_parts)
    n_slots, words = xs_parts[0].shape
    n_e, d, dff = wg.shape
    tm = MOE_ROW_TILE
    rows_spec = pl.BlockSpec((tm, words), lambda i, te, nv: (i, 0))
    weights = lambda shape: pl.BlockSpec((1,) + shape, lambda i, te, nv: (te[i], 0, 0))
    est = 2 * 3 * d * dff * 2 + 4 * tm * d * 2 + 3 * tm * MOE_FF_CHUNK * 4 + 3 * tm * d * 4
    grid_spec = pltpu.PrefetchScalarGridSpec(
        num_scalar_prefetch=2,
        grid=(n_slots // tm,),
        in_specs=[rows_spec] * n_parts + [weights((d, dff)), weights((d, dff)), weights((dff, d))],
        out_specs=[rows_spec] * n_parts,
    )
    return pl.pallas_call(
        functools.partial(_experts_kernel, n_parts=n_parts),
        grid_spec=grid_spec,
        out_shape=[jax.ShapeDtypeStruct((n_slots, words), jnp.int32)] * n_parts,
        compiler_params=_params(("arbitrary",), est),
        name="moe_experts",
    )(tile_expert, n_valid, *xs_parts, wg, wu, wd)


def _combine_kernel(*refs, n_parts):
    y1_refs = refs[:n_parts]
    y2_refs = refs[n_parts:2 * n_parts]
    gw_ref, x_ref, mod_ref, g_ref, o_ref = refs[2 * n_parts:]
    gw = gw_ref[0]
    y = gw[:, 0:1] * _load_packed(y1_refs) + gw[:, 1:2] * _load_packed(y2_refs)
    o_ref[0] = x_ref[0] + mod_ref[0][5:6] * (_rms(y) * g_ref[...])


def _combine(ys2_parts, gw, x1, mod, mod_row0, g_post):
    n_parts = len(ys2_parts)
    nb, rows, d = x1.shape
    words = ys2_parts[0].shape[-1]
    tm = min(TOKEN_TILE, rows)
    tok = lambda n: pl.BlockSpec((1, tm, n), lambda b, i: (b, i, 0))
    ysp = lambda k: pl.BlockSpec((1, tm, words), lambda b, i: (k * nb + b, i, 0))
    est = 2 * tm * (d * 2 * 4 + d * 4 + V7X_LANES * 4) + 3 * tm * d * 4
    return pl.pallas_call(
        functools.partial(_combine_kernel, n_parts=n_parts),
        grid=(nb, rows // tm),
        in_specs=[ysp(0)] * n_parts + [ysp(1)] * n_parts + [
            tok(V7X_LANES), tok(d),
            pl.BlockSpec((1, 6, d), lambda b, i: (b + mod_row0, 0, 0)),
            pl.BlockSpec((1, d), lambda b, i: (0, 0))],
        out_specs=tok(d),
        out_shape=jax.ShapeDtypeStruct((nb, rows, d), F32),
        compiler_params=_params(("parallel", "parallel"), est),
        name="moe_combine",
    )(*ys2_parts, *ys2_parts, gw, x1, mod, g_post.reshape(1, d))


def _moe_routed(gw, eid, cnt, *h2_parts, x1, mod, mod_row0, experts, g_post):
    nb, rows, d = x1.shape
    t = nb * rows
    tm = min(TOKEN_TILE, rows)
    tile_base, tile_expert, n_valid, n_slots = _route_plan(cnt, t)
    pos = _slots(eid.reshape(t, V7X_LANES), tile_base, tm)
    idx = jnp.concatenate([pos[:, 0], pos[:, 1]])
    xs = [_scatter_rows(h.reshape(t, h.shape[-1]), idx, n_slots) for h in h2_parts]
    ys = _experts(xs, tile_expert, n_valid, *experts)
    ys2 = [_gather_rows(y, idx).reshape(2 * nb, rows, y.shape[-1]) for y in ys]
    return _combine(ys2, gw, x1, mod, mod_row0, g_post)


def _cast_kernel(w_ref, o_ref):
    o_ref[...] = w_ref[...].astype(o_ref.dtype)


def _to_bf16(w):
    shape = w.shape
    w3 = w.reshape((-1,) + shape[-2:])
    n, r, c = w3.shape
    rb = r
    while rb * c * 4 > CAST_BLOCK_BYTES and rb % 16 == 0:
        rb //= 2
    spec = pl.BlockSpec((1, rb, c), lambda e, i: (e, i, 0))
    out = pl.pallas_call(
        _cast_kernel,
        grid=(n, r // rb),
        in_specs=[spec],
        out_specs=spec,
        out_shape=jax.ShapeDtypeStruct(w3.shape, BF16),
        compiler_params=_params(("parallel", "parallel"), 2 * rb * c * 6),
        name="cast_bf16",
    )(w3)
    return out.reshape(shape)
def _pack_w_in(w):
    a = 2 * GLA_KDIM + 2 * GLA_WIDTH
    lr = 2 * GLA_GATE_RANK
    reps = LR_PAD // lr
    return jnp.concatenate([w[:, :a], w[:, a + lr:]] + [w[:, a:a + lr]] * reps, axis=1).astype(BF16)


def _pack_gate(w_gate, b_gate):
    lr = 2 * GLA_GATE_RANK
    packed = []
    for d in range(2):
        wd = jnp.zeros((lr, GLA_KDIM), F32).at[d * GLA_GATE_RANK:(d + 1) * GLA_GATE_RANK].set(w_gate[d])
        hi, lo = _split(wd)
        packed.append(jnp.concatenate([hi, hi, lo, lo], axis=0))
    return jnp.stack(packed), b_gate.reshape(2, 1, GLA_KDIM)


def kernel(x, c, ctx, c_ctx, w_ada, b_ada, g_pre_mix, g_post_mix, g_pre_ffn, g_post_ffn, w_in,
           gla_w_gate, gla_b_gate, gla_g_norm, na_rpb, w_out, ffn_w_gate, ffn_w_up, ffn_w_down,
           moe_w_router, moe_w_gate, moe_w_up, moe_w_down):
    B, L, D = x.shape
    Lc = ctx.shape[1]
    depth = w_ada.shape[0]
    rows = L // GRID_W
    ctx_row = B
    n_cond = -(-(B + 1) // 8) * 8
    cond = jnp.zeros((n_cond, D), F32).at[:B].set(c).at[B].set(c_ctx)
    mod = _modulation(cond, w_ada, b_ada)
    tables = _rope_tables(L)
    ctx = ctx.reshape(1, B * Lc, D)

    for i in range(depth):
        last = i == depth - 1
        j = i // 2
        w = _pack_w_in(w_in[i])
        wgp, bgp = _pack_gate(gla_w_gate[i], gla_b_gate[i])
        qk, gv, gr, nq, nk, nv, bf, bb = _inproj(x, mod[i], 0, g_pre_mix[i], w, wgp, bgp, tables)
        cparts = _inproj(ctx, mod[i], ctx_row, g_pre_mix[i], w, wgp, bgp)
        cqk, cgv, cgr, cnq, cnk, cnv, cbf, cbb = [t.reshape(B, Lc, t.shape[-1]) for t in cparts]
        ga, gac = _gla(qk, gv, gr, bf, bb, cqk, cgv, cgr, cbf, cbb, gla_g_norm[i])
        na = _na(nq, nk, nv, cnk, cnv, _na_bias_table(na_rpb[i], rows))
        if i % 2 == 0:
            ffn = (_to_bf16(ffn_w_gate[j]), _to_bf16(ffn_w_up[j]), _to_bf16(ffn_w_down[j]), g_post_ffn[i])

            def mix_and_ffn(ga_, na_, x_, row0):
                return _outproj(ga_, na_, x_, mod[i], row0, w_out[i], g_post_mix[i], g_pre_ffn[i], ffn=ffn)
        else:
            experts = (_to_bf16(moe_w_gate[j]), _to_bf16(moe_w_up[j]), _to_bf16(moe_w_down[j]))

            def mix_and_ffn(ga_, na_, x_, row0):
                x1, *routed = _outproj(ga_, na_, x_, mod[i], row0, w_out[i], g_post_mix[i], g_pre_ffn[i],
                                       w_router=moe_w_router[j])
                return _moe_routed(*routed, x1=x1, mod=mod[i], mod_row0=row0, experts=experts, g_post=g_post_ffn[i])
        x = mix_and_ffn(ga, na, x, 0)
        if not last:
            nac = _ctx_attn(cnq, cnk, cnv)
            ctx = mix_and_ffn(gac.reshape(1, B * Lc, GLA_WIDTH), nac.reshape(1, B * Lc, NA_WIDTH), ctx, ctx_row)
    return x
```
